```python
import jax, jax.numpy as jnp
from jax import lax
import numpy as np

D_MODEL = 1024
BATCH = 1
SEQ = 16384
DEPTH = 2
DEC_BATCH = 32
DEC_SEQ = 1
PAST_LEN = 16384
PAGE_SIZE = 128

N_BRANCH = 4
BRANCH_W = D_MODEL // 4
CHUNK = 128
A_GROUPS = 4
A_GDIM = BRANCH_W // A_GROUPS
SB_HEADS = 4
SB_HEAD_DIM = BRANCH_W // SB_HEADS
SB_BLOCK = 128
SB_BIAS_INIT = (-2.0, -4.0, -6.0, -8.0)
CONV_W = 31
D_HEADS = 4
D_HEAD_DIM = BRANCH_W // D_HEADS
D_FF = 4 * D_MODEL
A_COLS = 2 * BRANCH_W
B_COLS = 3 * BRANCH_W
C_COLS = 2 * BRANCH_W
D_COLS = 4 * BRANCH_W
G_COLS = N_BRANCH * D_MODEL
IN_COLS = A_COLS + B_COLS + C_COLS + D_COLS + G_COLS
SPLIT_POINTS = (A_COLS, A_COLS + B_COLS, A_COLS + B_COLS + C_COLS, A_COLS + B_COLS + C_COLS + D_COLS)
EPS = 1e-6

kernel_name = 'hybrid_gated_branch_decoder_step'


def rms_norm(x, g):
    xf = x.astype(jnp.float32)
    y = xf * lax.rsqrt(jnp.mean(xf * xf, axis=-1, keepdims=True) + EPS)
    return y.astype(x.dtype) * g


def layer_norm(x, g, b):
    xf = x.astype(jnp.float32)
    mu = jnp.mean(xf, axis=-1, keepdims=True)
    var = jnp.mean(jnp.square(xf - mu), axis=-1, keepdims=True)
    return ((xf - mu) * lax.rsqrt(var + EPS)).astype(x.dtype) * g + b


def adaln_params(c, w, b):
    m = (jax.nn.silu(c) @ w + b)[:, None, :]
    return jnp.split(m, 6, axis=-1)


def chunk_spatial_gating(u, v, w_s, b_s):
    bsz, length, _ = v.shape
    pad = (-length) % CHUNK
    if pad:
        u = jnp.pad(u, ((0, 0), (0, pad), (0, 0)))
        v = jnp.pad(v, ((0, 0), (0, pad), (0, 0)))
    n = (length + pad) // CHUNK
    vc = v.reshape(bsz, n, CHUNK, A_GROUPS, A_GDIM)
    w_causal = jnp.where(jnp.tril(jnp.ones((CHUNK, CHUNK), dtype=bool)), w_s, 0)
    mixed = jnp.einsum('gts,bnsgc->bntgc', w_causal, vc) + b_s.T[:, :, None]
    return (u * mixed.reshape(bsz, n * CHUNK, BRANCH_W))[:, :length]


def stick_breaking(q, k, v, bias, q_offset):
    bsz, lq, nh, dh = q.shape
    sk = k.shape[1]
    blk = SB_BLOCK if lq % SB_BLOCK == 0 else lq
    nb = lq // blk
    kpos = jnp.arange(sk)
    qb = q.reshape(bsz, nb, blk, nh, dh).swapaxes(0, 1)
    scale = dh ** -0.5
    bias_f = bias.astype(jnp.float32)[None, :, None, None]

    def one_block(args):
        qi, bi = args
        qpos = q_offset + bi * blk + jnp.arange(blk)
        z = jnp.einsum('bqhd,bshd->bhqs', qi, k).astype(jnp.float32) * scale + bias_f
        causal = kpos[None, :] < qpos[:, None]
        log_1mb = jnp.where(causal, jax.nn.log_sigmoid(-z), 0.0)
        suffix = lax.cumsum(log_1mb, axis=3, reverse=True) - log_1mb
        w = jnp.where(causal, jnp.exp(jax.nn.log_sigmoid(z) + suffix), 0.0)
        return jnp.einsum('bhqs,bshd->bqhd', w.astype(v.dtype), v)

    o = lax.map(one_block, (qb, jnp.arange(nb)))
    return o.swapaxes(0, 1).reshape(bsz, lq, nh, dh)


def depthwise_conv(xp, w, b):
    y = lax.conv_general_dilated(xp, w[:, None, :], window_strides=(1,), padding='VALID',
                                 dimension_numbers=('NWC', 'WIO', 'NWC'),
                                 feature_group_count=BRANCH_W)
    return y + b


def forget_lower_bounds(logits):
    p = jax.nn.softmax(logits.astype(jnp.float32), axis=0)
    cp = jnp.cumsum(p, axis=0)
    return cp - cp[0:1]


def hgrn_recurrence(q, logf, k, i, s0):
    bsz, length, nh, dk = q.shape
    dv = i.shape[-1]
    blk = CHUNK if length % CHUNK == 0 else length
    nb = length // blk
    out_dtype, st_dtype = q.dtype, s0.dtype
    tril = jnp.tril(jnp.ones((blk, blk), dtype=bool))[None, :, :, None, None]

    def to_blocks(t):
        return t.reshape(bsz, nb, blk, nh, t.shape[-1]).swapaxes(0, 1)

    def step(s, xs):
        qb, fb, kb, ib = [t.astype(jnp.float32) for t in xs]
        sf = s.astype(jnp.float32)
        a = jnp.cumsum(fb, axis=1)
        decay = jnp.exp(jnp.where(tril, a[:, :, None] - a[:, None, :], -jnp.inf))
        scores = jnp.einsum('bthk,btshk,bshk->bhts', qb, decay, kb)
        o = jnp.einsum('bthk,bhkv->bthv', qb * jnp.exp(a), sf) + jnp.einsum('bhts,bshv->bthv', scores, ib)
        a_last = a[:, -1]
        s_new = jnp.exp(a_last)[..., None] * sf + jnp.einsum('bshk,bshv->bhkv', kb * jnp.exp(a_last[:, None] - a), ib)
        return s_new.astype(st_dtype), o.astype(out_dtype)

    s_end, o = lax.scan(step, s0, (to_blocks(q), to_blocks(logf), to_blocks(k), to_blocks(i)))
    return o.swapaxes(0, 1).reshape(bsz, length, nh, dv), s_end


def decoder_layer(x, c, k_past, v_past, conv_buf, s0, lb, p):
    bsz, length, _ = x.shape
    sh1, sc1, g1, sh2, sc2, g2 = adaln_params(c, p['ada_w'], p['ada_b'])
    h = rms_norm(x, p['norm1_g']) * (1 + sc1) + sh1
    z = h @ p['w_in']
    za, zb, zc, zd, zg = jnp.split(z, SPLIT_POINTS, axis=-1)

    u, v = jnp.split(jax.nn.gelu(za), 2, axis=-1)
    v = layer_norm(v, p['a_ln_g'], p['a_ln_b'])
    out_a = chunk_spatial_gating(u, v, p['a_ws'], p['a_bs'])

    qs, ks, vs = [t.reshape(bsz, length, SB_HEADS, SB_HEAD_DIM) for t in jnp.split(zb, 3, axis=-1)]
    qs = rms_norm(qs, p['b_qn_g'])
    ks = rms_norm(ks, p['b_kn_g'])
    k_all = jnp.concatenate([k_past, ks], axis=1)
    v_all = jnp.concatenate([v_past, vs], axis=1)
    out_b = stick_breaking(qs, k_all, v_all, p['b_bias'], k_past.shape[1]).reshape(bsz, length, BRANCH_W)

    ca, cb = jnp.split(zc, 2, axis=-1)
    cin = ca * jax.nn.sigmoid(cb)
    xp = jnp.concatenate([conv_buf, cin], axis=1)
    conv_new = xp[:, xp.shape[1] - (CONV_W - 1):]
    out_c = jax.nn.silu(layer_norm(depthwise_conv(xp, p['c_conv_w'], p['c_conv_b']), p['c_ln_g'], p['c_ln_b']))

    dq, df, di, dg = jnp.split(zd, 4, axis=-1)
    heads = lambda t: t.reshape(bsz, length, D_HEADS, D_HEAD_DIM)
    qd = jax.nn.silu(heads(dq)) * (D_HEAD_DIM ** -0.5)
    logf = jnp.logaddexp(jnp.log(lb), jnp.log1p(-lb) + jax.nn.log_sigmoid(df.astype(jnp.float32)))
    logf = heads(logf)
    kd = -jnp.expm1(logf)
    od, s_end = hgrn_recurrence(qd, logf, kd, heads(di), s0)
    out_d = (rms_norm(od, p['d_norm_g']) * jax.nn.silu(heads(dg))).reshape(bsz, length, BRANCH_W)

    gates = jax.nn.sigmoid(zg).reshape(bsz, length, N_BRANCH, D_MODEL)
    wb = p['w_branch']
    merged = (gates[:, :, 0] * (out_a @ wb[0]) + gates[:, :, 1] * (out_b @ wb[1])
              + gates[:, :, 2] * (out_c @ wb[2]) + gates[:, :, 3] * (out_d @ wb[3]))
    x = x + g1 * (merged @ p['w_out'])

    h2 = rms_norm(x, p['norm2_g']) * (1 + sc2) + sh2
    x = x + g2 * (jnp.square(jax.nn.relu(h2 @ p['w_ff1'])) @ p['w_ff2'])
    return x, ks, vs, conv_new, s_end, v


def setup_inputs(seed: int = 0) -> dict:
    key = jax.random.key(seed)
    ks = jax.random.split(key, 32)
    f32 = jnp.float32
    n_pages = PAST_LEN // PAGE_SIZE
    n_pool = (5 * DEC_BATCH * n_pages) // 4
    nrm = lambda k, shape, s: jax.random.normal(k, shape, f32) * s
    page_table = jax.random.permutation(ks[8], n_pool)[:DEC_BATCH * n_pages].reshape(DEC_BATCH, n_pages).astype(jnp.int32)
    return {
        'x_prompt': nrm(ks[0], (BATCH, SEQ, D_MODEL), 1.0),
        'x_sample': nrm(ks[1], (DEC_BATCH, DEC_SEQ, D_MODEL), 1.0),
        'c_prompt': nrm(ks[2], (BATCH, D_MODEL), 1.0),
        'c_sample': nrm(ks[3], (DEC_BATCH, D_MODEL), 1.0),
        'cache_k': nrm(ks[4], (n_pool, DEPTH, PAGE_SIZE, SB_HEADS, SB_HEAD_DIM), 1.0),
        'cache_v': nrm(ks[5], (n_pool, DEPTH, PAGE_SIZE, SB_HEADS, SB_HEAD_DIM), 1.0),
        'state_conv': nrm(ks[6], (DEC_BATCH, DEPTH, CONV_W - 1, BRANCH_W), 0.5),
        'state_hgrn': nrm(ks[7], (DEC_BATCH, DEPTH, D_HEADS, D_HEAD_DIM, D_HEAD_DIM), 0.3),
        'page_table': page_table,
        'ada_w': nrm(ks[9], (DEPTH, D_MODEL, 6 * D_MODEL), 0.5 * D_MODEL ** -0.5),
        'ada_b': nrm(ks[10], (DEPTH, 6 * D_MODEL), 0.02),
        'norm1_g': 1.0 + nrm(ks[11], (DEPTH, D_MODEL), 0.02),
        'norm2_g': 1.0 + nrm(ks[12], (DEPTH, D_MODEL), 0.02),
        'w_in': nrm(ks[13], (DEPTH, D_MODEL, IN_COLS), D_MODEL ** -0.5),
        'a_ln_g': 1.0 + nrm(ks[14], (DEPTH, BRANCH_W), 0.02),
        'a_ln_b': nrm(ks[15], (DEPTH, BRANCH_W), 0.02),
        'a_ws': nrm(ks[16], (DEPTH, A_GROUPS, CHUNK, CHUNK), CHUNK ** -0.5),
        'a_bs': 1.0 + nrm(ks[17], (DEPTH, A_GROUPS, CHUNK), 0.02),
        'b_qn_g': 1.0 + nrm(ks[18], (DEPTH, SB_HEAD_DIM), 0.02),
        'b_kn_g': 1.0 + nrm(ks[19], (DEPTH, SB_HEAD_DIM), 0.02),
        'b_bias': jnp.asarray(SB_BIAS_INIT, f32)[None, :] + nrm(ks[30], (DEPTH, SB_HEADS), 0.1),
        'c_conv_w': nrm(ks[20], (DEPTH, CONV_W, BRANCH_W), CONV_W ** -0.5),
        'c_conv_b': nrm(ks[21], (DEPTH, BRANCH_W), 0.02),
        'c_ln_g': 1.0 + nrm(ks[22], (DEPTH, BRANCH_W), 0.02),
        'c_ln_b': nrm(ks[23], (DEPTH, BRANCH_W), 0.02),
        'd_lb_logits': nrm(ks[24], (DEPTH, BRANCH_W), 1.0),
        'd_norm_g': 1.0 + nrm(ks[25], (DEPTH, D_HEAD_DIM), 0.02),
        'w_branch': nrm(ks[26], (DEPTH, N_BRANCH, BRANCH_W, D_MODEL), BRANCH_W ** -0.5),
        'w_out': nrm(ks[27], (DEPTH, D_MODEL, D_MODEL), D_MODEL ** -0.5),
        'w_ff1': nrm(ks[28], (DEPTH, D_MODEL, D_FF), D_MODEL ** -0.5),
        'w_ff2': nrm(ks[29], (DEPTH, D_FF, D_MODEL), D_FF ** -0.5),
    }


def reference(x_prompt, x_sample, c_prompt, c_sample, cache_k, cache_v, state_conv, state_hgrn, page_table,
              ada_w, ada_b, norm1_g, norm2_g, w_in, a_ln_g, a_ln_b, a_ws, a_bs, b_qn_g, b_kn_g, b_bias,
              c_conv_w, c_conv_b, c_ln_g, c_ln_b, d_lb_logits, d_norm_g, w_branch, w_out, w_ff1, w_ff2):
    lbs = forget_lower_bounds(d_lb_logits)
    dec_batch, n_pages = page_table.shape
    past_len = n_pages * cache_k.shape[2]
    bp = x_prompt.shape[0]
    dt = x_prompt.dtype
    empty_kv = jnp.zeros((bp, 0, SB_HEADS, SB_HEAD_DIM), dt)
    conv0 = jnp.zeros((bp, CONV_W - 1, BRANCH_W), dt)
    s_zero = jnp.zeros((bp, D_HEADS, D_HEAD_DIM, D_HEAD_DIM), dt)

    yp, ys = x_prompt, x_sample
    kp_l, vp_l, cp_l, sp_l = [], [], [], []
    ks_l, vs_l, cs_l, ss_l, av_l = [], [], [], [], []
    for l in range(DEPTH):
        p = {'ada_w': ada_w[l], 'ada_b': ada_b[l], 'norm1_g': norm1_g[l], 'norm2_g': norm2_g[l],
             'w_in': w_in[l], 'a_ln_g': a_ln_g[l], 'a_ln_b': a_ln_b[l], 'a_ws': a_ws[l], 'a_bs': a_bs[l],
             'b_qn_g': b_qn_g[l], 'b_kn_g': b_kn_g[l], 'b_bias': b_bias[l],
             'c_conv_w': c_conv_w[l], 'c_conv_b': c_conv_b[l],
             'c_ln_g': c_ln_g[l], 'c_ln_b': c_ln_b[l], 'd_norm_g': d_norm_g[l], 'w_branch': w_branch[l],
             'w_out': w_out[l], 'w_ff1': w_ff1[l], 'w_ff2': w_ff2[l]}
        yp, kp, vp, cp, sp, _ = decoder_layer(yp, c_prompt, empty_kv, empty_kv, conv0, s_zero, lbs[l], p)
        k_past = cache_k[page_table, l].reshape(dec_batch, past_len, SB_HEADS, SB_HEAD_DIM)
        v_past = cache_v[page_table, l].reshape(dec_batch, past_len, SB_HEADS, SB_HEAD_DIM)
        ys, ksn, vsn, csn, ssn, avn = decoder_layer(ys, c_sample, k_past, v_past, state_conv[:, l],
                                                    state_hgrn[:, l], lbs[l], p)
        kp_l.append(kp); vp_l.append(vp); cp_l.append(cp); sp_l.append(sp)
        ks_l.append(ksn); vs_l.append(vsn); cs_l.append(csn); ss_l.append(ssn); av_l.append(avn)

    return (yp, ys,
            jnp.stack(kp_l, axis=1), jnp.stack(vp_l, axis=1),
            jnp.stack(ks_l, axis=1), jnp.stack(vs_l, axis=1),
            jnp.stack(cp_l, axis=1), jnp.stack(cs_l, axis=1),
            jnp.stack(sp_l, axis=1), jnp.stack(ss_l, axis=1),
            jnp.stack(av_l, axis=1))
```

```python
import functools

import jax
import jax.numpy as jnp
from jax import lax
from jax.experimental import pallas as pl
from jax.experimental.pallas import tpu as pltpu

F32 = jnp.float32
BF16 = jnp.bfloat16

D_MODEL = 1024
BRANCH_W = 256
HEADS = 4
HEAD_DIM = 64
CHUNK = 128
CONV_W = 31
D_FF = 4096
EPS = 1e-6
QK_SCALE = HEAD_DIM ** -0.5

ROW_TILE = 256
SUB = 16
CONV_HALO = 32
ATT_BLK = 128
PAGES_PER_STEP = 8
VMEM_LIMIT = 56 * 1024 * 1024


def _dot(a, b):
    return jnp.dot(a, b, preferred_element_type=F32)


def _dot_exact(a, b):
    return jnp.dot(a, b, preferred_element_type=F32, precision=lax.Precision.HIGHEST)


def _dot_nt(a, b):
    return lax.dot_general(a, b, (((1,), (1,)), ((), ())), preferred_element_type=F32)


def _dot_tn(a, b):
    return lax.dot_general(a, b, (((0,), (0,)), ((), ())), preferred_element_type=F32)


def _sigmoid(x):
    return 1.0 / (1.0 + jnp.exp(-x))


def _silu(x):
    return x * _sigmoid(x)


def _softplus(x):
    return jnp.maximum(x, 0.0) + jnp.log1p(jnp.exp(-jnp.abs(x)))


def _rms_rows(x):
    return x * lax.rsqrt(jnp.mean(x * x, axis=-1, keepdims=True) + EPS)


def _layer_norm(x, g, b):
    mu = jnp.mean(x, axis=-1, keepdims=True)
    xc = x - mu
    var = jnp.mean(xc * xc, axis=-1, keepdims=True)
    return xc * lax.rsqrt(var + EPS) * g + b


def _head_rms(x, hmask, g):
    ms = _dot_exact(x * x, hmask) * (1.0 / HEAD_DIM)
    return x * lax.rsqrt(ms + EPS) * g


def _log_forget(df, lb):
    ls = -_softplus(-df)
    b = jnp.log1p(-lb) + ls
    pos = lb > 0.0
    a = jnp.log(jnp.where(pos, lb, 1.0))
    lae = jnp.maximum(a, b) + jnp.log1p(jnp.exp(-jnp.abs(a - b)))
    return jnp.where(pos, lae, b)


def _one_minus_forget(df, lb):
    return (1.0 - lb) * _sigmoid(-df)


def _const_spec(shape):
    nd = len(shape)
    return pl.BlockSpec(shape, lambda *_: (0,) * nd)


def _ada_kernel(c_ref, w_ref, b_ref, o_ref):
    s = _silu(c_ref[...]).astype(BF16)
    o_ref[...] = _dot(s, w_ref[...].astype(BF16)) + b_ref[...]


def _ada_call(c_all, ada_w, ada_b):
    depth = ada_w.shape[0]
    rows = c_all.shape[0]
    n_col = ada_w.shape[2] // D_MODEL
    return pl.pallas_call(
        _ada_kernel,
        out_shape=jax.ShapeDtypeStruct((depth, rows, ada_w.shape[2]), F32),
        grid=(depth, n_col),
        in_specs=[
            pl.BlockSpec((rows, D_MODEL), lambda l, j: (0, 0)),
            pl.BlockSpec((None, D_MODEL, D_MODEL), lambda l, j: (l, 0, j)),
            pl.BlockSpec((None, 1, D_MODEL), lambda l, j: (l, 0, j)),
        ],
        out_specs=pl.BlockSpec((None, rows, D_MODEL), lambda l, j: (l, 0, j)),
        compiler_params=pltpu.CompilerParams(vmem_limit_bytes=VMEM_LIMIT),
        name="ada_params",
    )(c_all, ada_w, ada_b.reshape(depth, 1, -1))


P_ALN_G, P_ALN_B, P_QN_G, P_KN_G, P_CONV_B, P_CLN_G, P_CLN_B, P_LB, P_DN_G = range(9)


def _gated_local_merge(zg, out_a, out_c, out_d, wbr_ref):
    ga = _sigmoid(zg[:, 0 * D_MODEL:1 * D_MODEL])
    gb = _sigmoid(zg[:, 1 * D_MODEL:2 * D_MODEL])
    gc = _sigmoid(zg[:, 2 * D_MODEL:3 * D_MODEL])
    gd = _sigmoid(zg[:, 3 * D_MODEL:4 * D_MODEL])
    macd = (ga * _dot(out_a.astype(BF16), wbr_ref[0])
            + gc * _dot(out_c.astype(BF16), wbr_ref[2])
            + gd * _dot(out_d.astype(BF16), wbr_ref[3]))
    return macd, gb


def _front_kernel(x_ref, g1_ref, sc_ref, sh_ref, wa_ref, wb_ref, wc_ref, wd_ref, wg_ref, wbr_ref,
                  p_ref, aws_ref, abias_ref, convw_ref, hmask_ref, tri_ref, sel_ref,
                  q_ref, k_ref, v_ref, kt_ref, vb_ref, macd_ref, gb_ref, convnew_ref, state_ref,
                  convbuf, st_ref, b_s, qd_s, kd_s, di_s, od_s):
    tm = x_ref.shape[0]
    step = pl.program_id(0)

    @pl.when(step == 0)
    def _():
        convbuf[0:CONV_HALO, :] = jnp.zeros((CONV_HALO, BRANCH_W), F32)
        st_ref[...] = jnp.zeros_like(st_ref)

    prm = p_ref[...]
    row = lambda i: prm[i:i + 1, :]
    hmask = hmask_ref[...]
    hmask_bf = hmask.astype(BF16)

    x = x_ref[...]
    h = (_rms_rows(x) * g1_ref[...] * (1.0 + sc_ref[...]) + sh_ref[...]).astype(BF16)

    za = _dot(h, wa_ref[...])
    ga = jax.nn.gelu(za)
    u = ga[:, :BRANCH_W]
    vn = _layer_norm(ga[:, BRANCH_W:], row(P_ALN_G), row(P_ALN_B))
    lane_grp = lax.broadcasted_iota(jnp.int32, (CHUNK, BRANCH_W), 1) // (BRANCH_W // HEADS)
    tril = (lax.broadcasted_iota(jnp.int32, (CHUNK, CHUNK), 1)
            <= lax.broadcasted_iota(jnp.int32, (CHUNK, CHUNK), 0))
    w_causal = [jnp.where(tril, aws_ref[g], 0.0).astype(BF16) for g in range(HEADS)]
    mixed_chunks = []
    for c in range(tm // CHUNK):
        vc = vn[c * CHUNK:(c + 1) * CHUNK, :].astype(BF16)
        mixed = None
        for g in range(HEADS):
            mg = _dot(w_causal[g], vc)
            mixed = mg if mixed is None else jnp.where(lane_grp == g, mg, mixed)
        mixed_chunks.append(mixed + abias_ref[...])
    out_a = u * jnp.concatenate(mixed_chunks, axis=0)

    zb = _dot(h, wb_ref[...])
    qn = _head_rms(zb[:, :BRANCH_W], hmask, row(P_QN_G))
    kn = _head_rms(zb[:, BRANCH_W:2 * BRANCH_W], hmask, row(P_KN_G))
    vv = zb[:, 2 * BRANCH_W:]
    q_ref[...] = (qn * QK_SCALE).astype(BF16)
    k_ref[...] = kn
    v_ref[...] = vv
    vb_ref[...] = vv.astype(BF16)
    knt = kn.T.astype(BF16)
    for c in range(tm // ATT_BLK):
        kt_ref[c] = knt[:, c * ATT_BLK:(c + 1) * ATT_BLK]

    zc = _dot(h, wc_ref[...])
    cin = zc[:, :BRANCH_W] * _sigmoid(zc[:, BRANCH_W:])
    convbuf[CONV_HALO:CONV_HALO + tm, :] = cin
    off0 = CONV_HALO - (CONV_W - 1)
    acc = jnp.broadcast_to(row(P_CONV_B), (tm, BRANCH_W))
    for j in range(CONV_W):
        acc = acc + convw_ref[j:j + 1, :] * convbuf[off0 + j:off0 + j + tm, :]
    out_c = _silu(_layer_norm(acc, row(P_CLN_G), row(P_CLN_B)))
    tail = convbuf[tm:tm + CONV_HALO, :]
    convnew_ref[...] = tail
    convbuf[0:CONV_HALO, :] = tail

    zd = _dot(h, wd_ref[...])
    qd_s[...] = _silu(zd[:, :BRANCH_W]) * QK_SCALE
    logf = _log_forget(zd[:, BRANCH_W:2 * BRANCH_W], row(P_LB))
    kd_s[...] = _one_minus_forget(zd[:, BRANCH_W:2 * BRANCH_W], row(P_LB))
    di_s[...] = zd[:, 2 * BRANCH_W:3 * BRANCH_W]
    b_s[...] = _dot_exact(tri_ref[...], logf)
    sub_row = lax.broadcasted_iota(jnp.int32, (SUB, BRANCH_W), 0)
    sel_bf = sel_ref[...].astype(BF16)

    def sub_chunk(c, carry):
        r0 = pl.multiple_of(c * SUB, SUB)
        bc = b_s[pl.ds(r0, SUB), :]
        qc = qd_s[pl.ds(r0, SUB), :]
        kc = kd_s[pl.ds(r0, SUB), :]
        ic = di_s[pl.ds(r0, SUB), :]
        bl = bc[SUB - 1:SUB, :]
        st = st_ref[...]
        o_inter = _dot_nt((qc * jnp.exp(bc)).astype(BF16), st.astype(BF16))
        kt = kc * jnp.exp(bl - bc)
        m = _dot_tn(ic.astype(BF16), kt.astype(BF16))
        st_ref[...] = st * jnp.exp(bl) + m * hmask
        pieces = []
        for t in range(SUB):
            e = jnp.exp(jnp.minimum(bc[t:t + 1, :] - bc, 0.0))
            pieces.append(jnp.where(sub_row <= t, e * kc * qc[t:t + 1, :], 0.0))
        wts = jnp.concatenate(pieces, axis=0).astype(BF16)
        coef = _dot(wts, hmask_bf)
        gmat = (coef * jnp.tile(ic, (SUB, 1))).astype(BF16)
        od_s[pl.ds(r0, SUB), :] = o_inter + _dot(sel_bf, gmat)
        return carry

    lax.fori_loop(0, tm // SUB, sub_chunk, 0)
    state_ref[...] = st_ref[...]
    od = od_s[...]
    out_d = _head_rms(od, hmask, row(P_DN_G)) * _silu(zd[:, 3 * BRANCH_W:])

    zg = _dot(h, wg_ref[...])
    macd, gb = _gated_local_merge(zg, out_a, out_c, out_d, wbr_ref)
    macd_ref[...] = macd
    gb_ref[...] = gb


def _front_call(x, g1, sc, sh, wts, consts):
    length = x.shape[0]
    tm = ROW_TILE
    n_steps = length // tm
    row_spec = lambda w: pl.BlockSpec((tm, w), lambda i: (i, 0))
    in_arrays = [x, g1, sc, sh, wts['wa'], wts['wb'], wts['wc'], wts['wd'], wts['wg'], wts['wbr'],
                 wts['ptab'], wts['a_ws'], wts['a_bias'], wts['conv_w'],
                 consts['hmask'], consts['tri'], consts['sel']]
    in_specs = [row_spec(D_MODEL)] + [_const_spec(a.shape) for a in in_arrays[1:]]
    out_shape = [
        jax.ShapeDtypeStruct((length, BRANCH_W), BF16),
        jax.ShapeDtypeStruct((length, BRANCH_W), F32),
        jax.ShapeDtypeStruct((length, BRANCH_W), F32),
        jax.ShapeDtypeStruct((length // ATT_BLK, BRANCH_W, ATT_BLK), BF16),
        jax.ShapeDtypeStruct((length, BRANCH_W), BF16),
        jax.ShapeDtypeStruct((length, D_MODEL), F32),
        jax.ShapeDtypeStruct((length, D_MODEL), F32),
        jax.ShapeDtypeStruct((CONV_HALO, BRANCH_W), F32),
        jax.ShapeDtypeStruct((BRANCH_W, BRANCH_W), F32),
    ]
    out_specs = [
        row_spec(BRANCH_W), row_spec(BRANCH_W), row_spec(BRANCH_W),
        pl.BlockSpec((tm // ATT_BLK, BRANCH_W, ATT_BLK), lambda i: (i, 0, 0)),
        row_spec(BRANCH_W), row_spec(D_MODEL), row_spec(D_MODEL),
        _const_spec((CONV_HALO, BRANCH_W)), _const_spec((BRANCH_W, BRANCH_W)),
    ]
    scratch = [
        pltpu.VMEM((tm + CONV_HALO, BRANCH_W), F32),
        pltpu.VMEM((BRANCH_W, BRANCH_W), F32),
    ] + [pltpu.VMEM((tm, BRANCH_W), F32) for _ in range(5)]
    return pl.pallas_call(
        _front_kernel,
        out_shape=out_shape,
        grid=(n_steps,),
        in_specs=in_specs,
        out_specs=out_specs,
        scratch_shapes=scratch,
        compiler_params=pltpu.CompilerParams(
            dimension_semantics=("arbitrary",), vmem_limit_bytes=VMEM_LIMIT),
        name="prompt_front",
    )(*in_arrays)


def _sb_block(z, r, uo, causal):
    sp = _softplus(z)
    log_b = z - sp
    if causal is not None:
        sp = jnp.where(causal, sp, 0.0)
    hi = sp.astype(BF16)
    lo = (sp - hi.astype(F32)).astype(BF16)
    cs = _dot(jnp.concatenate([hi, lo], axis=1), uo)
    w = jnp.exp(log_b - cs[:, :ATT_BLK] - r)
    if causal is not None:
        w = jnp.where(causal, w, 0.0)
    return w, r + cs[:, ATT_BLK:]


def _attn_kernel(bias_ref, uo_ref, q_ref, kt_ref, v_ref, o_ref, acc_ref, r_ref):
    i = pl.program_id(0)
    q = q_ref[...]
    qh = [q[:, HEAD_DIM * h:HEAD_DIM * (h + 1)] for h in range(HEADS)]
    acc_ref[...] = jnp.zeros_like(acc_ref)
    r_ref[...] = jnp.zeros_like(r_ref)
    lane_head = lax.broadcasted_iota(jnp.int32, (ATT_BLK, BRANCH_W), 1) // HEAD_DIM
    head_lanes = [jnp.where(lane_head == h, 1.0, 0.0).astype(BF16) for h in range(HEADS)]
    causal = (lax.broadcasted_iota(jnp.int32, (ATT_BLK, ATT_BLK), 1)
              < lax.broadcasted_iota(jnp.int32, (ATT_BLK, ATT_BLK), 0))

    def key_block(j, mask):
        off = pl.multiple_of(j * ATT_BLK, ATT_BLK)
        vb = v_ref[pl.ds(off, ATT_BLK), :]
        ktb = kt_ref[j]
        ws = []
        for h in range(HEADS):
            z = _dot(qh[h], ktb[HEAD_DIM * h:HEAD_DIM * (h + 1), :]) + bias_ref[h:h + 1, :]
            w, r_new = _sb_block(z, r_ref[h], uo_ref[...], mask)
            r_ref[h] = r_new
            ws.append(w.astype(BF16))
        vbd = jnp.concatenate([vb * head_lanes[h] for h in range(HEADS)], axis=0)
        acc_ref[...] += _dot(jnp.concatenate(ws, axis=1), vbd)

    key_block(i, causal)

    def older(t, carry):
        key_block(i - 1 - t, None)
        return carry

    lax.fori_loop(0, i, older, 0)
    o_ref[...] = acc_ref[...]


def _attn_call(bias_rows, uo, q, kt, vb):
    length = q.shape[0]
    n_blk = length // ATT_BLK
    return pl.pallas_call(
        _attn_kernel,
        out_shape=jax.ShapeDtypeStruct((length, BRANCH_W), F32),
        grid=(n_blk,),
        in_specs=[
            _const_spec(bias_rows.shape), _const_spec(uo.shape),
            pl.BlockSpec((ATT_BLK, BRANCH_W), lambda i: (i, 0)),
            _const_spec(kt.shape), _const_spec(vb.shape),
        ],
        out_specs=pl.BlockSpec((ATT_BLK, BRANCH_W), lambda i: (i, 0)),
        scratch_shapes=[
            pltpu.VMEM((ATT_BLK, BRANCH_W), F32),
            pltpu.VMEM((HEADS, ATT_BLK, ATT_BLK), F32),
        ],
        compiler_params=pltpu.CompilerParams(
            dimension_semantics=("arbitrary",), vmem_limit_bytes=VMEM_LIMIT),
        name="prompt_attn",
    )(bias_rows, uo, q, kt, vb)


def _back_kernel(x_ref, ob_ref, macd_ref, gb_ref, g1_ref, n2_ref, sc2_ref, sh2_ref, g2_ref,
                 wb1_ref, wo_ref, w1_ref, w2_ref, y_ref):
    merged = macd_ref[...] + gb_ref[...] * _dot(ob_ref[...].astype(BF16), wb1_ref[...])
    x1 = x_ref[...] + g1_ref[...] * _dot(merged.astype(BF16), wo_ref[...])
    h2 = (_rms_rows(x1) * n2_ref[...] * (1.0 + sc2_ref[...]) + sh2_ref[...]).astype(BF16)
    hid = jnp.maximum(_dot(h2, w1_ref[...]), 0.0)
    y_ref[...] = x1 + g2_ref[...] * _dot((hid * hid).astype(BF16), w2_ref[...])


def _back_call(x, ob, macd, gb, g1, n2, sc2, sh2, g2, wts, tm):
    rows = x.shape[0]
    mod_rows = g1.shape[0]
    if mod_rows == 1:
        mod_spec = _const_spec((1, D_MODEL))
    else:
        mod_spec = pl.BlockSpec((tm, D_MODEL), lambda i: (i, 0))
    row_spec = lambda w: pl.BlockSpec((tm, w), lambda i: (i, 0))
    weights = [wts['wb1'], wts['wo'], wts['w1'], wts['w2']]
    return pl.pallas_call(
        _back_kernel,
        out_shape=jax.ShapeDtypeStruct((rows, D_MODEL), F32),
        grid=(rows // tm,),
        in_specs=[row_spec(D_MODEL), row_spec(BRANCH_W), row_spec(D_MODEL), row_spec(D_MODEL),
                  mod_spec, _const_spec((1, D_MODEL)), mod_spec, mod_spec, mod_spec]
                 + [_const_spec(w.shape) for w in weights],
        out_specs=row_spec(D_MODEL),
        compiler_params=pltpu.CompilerParams(
            dimension_semantics=("arbitrary",), vmem_limit_bytes=VMEM_LIMIT),
        name="layer_back",
    )(x, ob, macd, gb, g1, n2, sc2, sh2, g2, *weights)


def _sample_front_kernel(x_ref, g1_ref, sc_ref, sh_ref, wa_ref, wb_ref, wc_ref, wd_ref, wg_ref, wbr_ref,
                         p_ref, a0_ref, convw_ref, hmask_ref, tdiag_ref, conv_ref, s0_ref,
                         q_ref, k_ref, v_ref, macd_ref, gb_ref, convnew_ref, snew_ref, av_ref):
    nb = x_ref.shape[0]
    prm = p_ref[...]
    row = lambda i: prm[i:i + 1, :]
    hmask = hmask_ref[...]

    x = x_ref[...]
    h = (_rms_rows(x) * g1_ref[...] * (1.0 + sc_ref[...]) + sh_ref[...]).astype(BF16)

    ga = jax.nn.gelu(_dot(h, wa_ref[...]))
    vn = _layer_norm(ga[:, BRANCH_W:], row(P_ALN_G), row(P_ALN_B))
    av_ref[...] = vn
    out_a = ga[:, :BRANCH_W] * (a0_ref[0:1, :] * vn + a0_ref[1:2, :])

    zb = _dot(h, wb_ref[...])
    qn = _head_rms(zb[:, :BRANCH_W], hmask, row(P_QN_G))
    kn = _head_rms(zb[:, BRANCH_W:2 * BRANCH_W], hmask, row(P_KN_G))
    q_ref[...] = qn * QK_SCALE
    k_ref[...] = kn
    v_ref[...] = zb[:, 2 * BRANCH_W:]

    zc = _dot(h, wc_ref[...])
    cin = zc[:, :BRANCH_W] * _sigmoid(zc[:, BRANCH_W:])
    acc = row(P_CONV_B) + convw_ref[CONV_W - 1:CONV_W, :] * cin
    for j in range(CONV_W - 1):
        acc = acc + convw_ref[j:j + 1, :] * conv_ref[j]
        if j >= 1:
            convnew_ref[j - 1] = conv_ref[j]
    convnew_ref[CONV_W - 2] = cin
    out_c = _silu(_layer_norm(acc, row(P_CLN_G), row(P_CLN_B)))

    zd = _dot(h, wd_ref[...])
    qd = _silu(zd[:, :BRANCH_W]) * QK_SCALE
    logf = _log_forget(zd[:, BRANCH_W:2 * BRANCH_W], row(P_LB))
    ea = jnp.exp(logf)
    kd = _one_minus_forget(zd[:, BRANCH_W:2 * BRANCH_W], row(P_LB))
    di = zd[:, 2 * BRANCH_W:3 * BRANCH_W]
    tdiag = tdiag_ref[...]

    def spread(r):
        t3 = jnp.broadcast_to(r[:, None, :], (nb, HEAD_DIM, BRANCH_W)) * tdiag[None]
        flat = _dot_exact(t3.reshape(nb * HEAD_DIM, BRANCH_W), hmask)
        return flat.reshape(nb, HEAD_DIM, BRANCH_W)

    s0 = s0_ref[...]
    od = jnp.sum(spread(qd * ea) * s0, axis=1) + _dot_exact(qd * kd, hmask) * di
    snew_ref[...] = spread(ea) * s0 + spread(kd) * di[:, None, :]
    out_d = _head_rms(od, hmask, row(P_DN_G)) * _silu(zd[:, 3 * BRANCH_W:])

    zg = _dot(h, wg_ref[...])
    macd, gb = _gated_local_merge(zg, out_a, out_c, out_d, wbr_ref)
    macd_ref[...] = macd
    gb_ref[...] = gb


def _sample_front_call(x, g1, sc, sh, wts, consts, conv_t, s0_r):
    nb = x.shape[0]
    in_arrays = [x, g1, sc, sh, wts['wa'], wts['wb'], wts['wc'], wts['wd'], wts['wg'], wts['wbr'],
                 wts['ptab'], wts['a0'], wts['conv_w'], consts['hmask'], consts['tdiag'], conv_t, s0_r]
    out_shape = [
        jax.ShapeDtypeStruct((nb, BRANCH_W), F32),
        jax.ShapeDtypeStruct((nb, BRANCH_W), F32),
        jax.ShapeDtypeStruct((nb, BRANCH_W), F32),
        jax.ShapeDtypeStruct((nb, D_MODEL), F32),
        jax.ShapeDtypeStruct((nb, D_MODEL), F32),
        jax.ShapeDtypeStruct(conv_t.shape, F32),
        jax.ShapeDtypeStruct(s0_r.shape, F32),
        jax.ShapeDtypeStruct((nb, BRANCH_W), F32),
    ]
    return pl.pallas_call(
        _sample_front_kernel,
        out_shape=out_shape,
        grid=(1,),
        in_specs=[_const_spec(a.shape) for a in in_arrays],
        out_specs=[_const_spec(s.shape) for s in out_shape],
        compiler_params=pltpu.CompilerParams(vmem_limit_bytes=VMEM_LIMIT),
        name="sample_front",
    )(*in_arrays)


Q_ROWS = 16


def _paged_attn_kernel(pt_ref, q_ref, bias_ref, uo_ref, *refs):
    k_refs = refs[:PAGES_PER_STEP]
    v_refs = refs[PAGES_PER_STEP:2 * PAGES_PER_STEP]
    o_ref, acc_ref, r_ref = refs[2 * PAGES_PER_STEP:]
    jc = pl.program_id(1)

    @pl.when(jc == 0)
    def _():
        acc_ref[...] = jnp.zeros_like(acc_ref)
        r_ref[...] = jnp.zeros_like(r_ref)

    row_id = lax.broadcasted_iota(jnp.int32, (Q_ROWS, BRANCH_W), 0)
    own_head = row_id == lax.broadcasted_iota(jnp.int32, (Q_ROWS, BRANCH_W), 1) // HEAD_DIM
    qrows = jnp.where(own_head, jnp.broadcast_to(q_ref[...], (Q_ROWS, BRANCH_W)), 0.0).astype(BF16)
    acc = acc_ref[...]
    r = r_ref[...]
    for p in reversed(range(PAGES_PER_STEP)):
        z = _dot_nt(qrows, k_refs[p][...].astype(BF16)) + bias_ref[...]
        w, r = _sb_block(z, r, uo_ref[...], None)
        acc = acc + _dot(w.astype(BF16), v_refs[p][...].astype(BF16))
    acc_ref[...] = acc
    r_ref[...] = r

    @pl.when(jc == pl.num_programs(1) - 1)
    def _():
        o_ref[...] = jnp.sum(jnp.where(own_head, acc, 0.0), axis=0, keepdims=True)


def _paged_attn_call(page_table, q, bias_rows, uo, cache_k, cache_v, layer):
    nb, n_pages = page_table.shape
    n_chunks = n_pages // PAGES_PER_STEP

    def page_spec(p):
        def index_map(b, jc, pt):
            return (pt[b, (n_chunks - 1 - jc) * PAGES_PER_STEP + p], layer, 0, 0)
        return pl.BlockSpec((None, None, CHUNK, BRANCH_W), index_map)

    page_specs = [page_spec(p) for p in range(PAGES_PER_STEP)]
    grid_spec = pltpu.PrefetchScalarGridSpec(
        num_scalar_prefetch=1,
        grid=(nb, n_chunks),
        in_specs=[
            pl.BlockSpec((None, 1, BRANCH_W), lambda b, jc, pt: (b, 0, 0)),
            pl.BlockSpec(bias_rows.shape, lambda b, jc, pt: (0, 0)),
            pl.BlockSpec(uo.shape, lambda b, jc, pt: (0, 0)),
        ] + page_specs + page_specs,
        out_specs=pl.BlockSpec((None, 1, BRANCH_W), lambda b, jc, pt: (b, 0, 0)),
        scratch_shapes=[
            pltpu.VMEM((Q_ROWS, BRANCH_W), F32),
            pltpu.VMEM((Q_ROWS, CHUNK), F32),
        ],
    )
    return pl.pallas_call(
        _paged_attn_kernel,
        out_shape=jax.ShapeDtypeStruct((nb, 1, BRANCH_W), F32),
        grid_spec=grid_spec,
        compiler_params=pltpu.CompilerParams(
            dimension_semantics=("arbitrary", "arbitrary"), vmem_limit_bytes=VMEM_LIMIT),
        name="sample_paged_attn",
    )(page_table, q.reshape(nb, 1, BRANCH_W), bias_rows, uo,
      *([cache_k] * PAGES_PER_STEP), *([cache_v] * PAGES_PER_STEP))


def _constants():
    lane = jnp.arange(BRANCH_W)
    hmask = (lane[:, None] // HEAD_DIM == lane[None, :] // HEAD_DIM).astype(F32)
    r = jnp.arange(ROW_TILE)
    tri = ((r[:, None] // SUB == r[None, :] // SUB) & (r[None, :] <= r[:, None])).astype(F32)
    sel = (jnp.arange(SUB)[:, None] == jnp.arange(SUB * SUB)[None, :] // SUB).astype(F32)
    tdiag = (jnp.arange(HEAD_DIM)[:, None] == lane[None, :] % HEAD_DIM).astype(F32)
    j = jnp.arange(2 * ATT_BLK) % ATT_BLK
    c = jnp.arange(2 * ATT_BLK)
    uo = ((c[None, :] >= ATT_BLK) | (j[:, None] > c[None, :])).astype(BF16)
    return {'hmask': hmask, 'tri': tri, 'sel': sel, 'tdiag': tdiag, 'uo': uo}


def _layer_weights(l, lb, w_in, a_ln_g, a_ln_b, a_ws, a_bs, b_qn_g, b_kn_g, c_conv_w, c_conv_b,
                   c_ln_g, c_ln_b, d_norm_g, w_branch, w_out, w_ff1, w_ff2):
    wi = w_in[l].astype(BF16)
    c0 = 2 * BRANCH_W
    c1 = c0 + 3 * BRANCH_W
    c2 = c1 + 2 * BRANCH_W
    c3 = c2 + 4 * BRANCH_W
    tile4 = lambda g: jnp.tile(g, HEADS)
    rows = [a_ln_g[l], a_ln_b[l], tile4(b_qn_g[l]), tile4(b_kn_g[l]), c_conv_b[l], c_ln_g[l], c_ln_b[l],
            lb, tile4(d_norm_g[l])]
    ptab = jnp.zeros((16, BRANCH_W), F32).at[:len(rows)].set(jnp.stack(rows))
    grp = BRANCH_W // HEADS
    wbr = w_branch[l].astype(BF16)
    return {
        'wa': wi[:, :c0], 'wb': wi[:, c0:c1], 'wc': wi[:, c1:c2], 'wd': wi[:, c2:c3], 'wg': wi[:, c3:],
        'wbr': wbr, 'wb1': wbr[1], 'ptab': ptab,
        'a_ws': a_ws[l], 'a_bias': jnp.repeat(a_bs[l].T, grp, axis=1),
        'a0': jnp.stack([jnp.repeat(a_ws[l][:, 0, 0], grp), jnp.repeat(a_bs[l][:, 0], grp)]),
        'conv_w': jnp.zeros((32, BRANCH_W), F32).at[:CONV_W].set(c_conv_w[l]),
        'wo': w_out[l].astype(BF16), 'w1': w_ff1[l].astype(BF16), 'w2': w_ff2[l].astype(BF16),
    }


def kernel(x_prompt, x_sample, c_prompt, c_sample, cache_k, cache_v, state_conv, state_hgrn, page_table,
           ada_w, ada_b, norm1_g, norm2_g, w_in, a_ln_g, a_ln_b, a_ws, a_bs, b_qn_g, b_kn_g, b_bias,
           c_conv_w, c_conv_b, c_ln_g, c_ln_b, d_lb_logits, d_norm_g, w_branch, w_out, w_ff1, w_ff2):
    depth = w_in.shape[0]
    seq = x_prompt.shape[1]
    nb = x_sample.shape[0]
    n_pool = cache_k.shape[0]
    consts = _constants()

    cp = jnp.cumsum(jax.nn.softmax(d_lb_logits.astype(F32), axis=0), axis=0)
    lbs = cp - cp[0:1]

    ada_rows = 8 * ((1 + nb + 7) // 8)
    c_all = jnp.zeros((ada_rows, D_MODEL), F32).at[0:1].set(c_prompt).at[1:1 + nb].set(c_sample)
    mods = _ada_call(c_all, ada_w, ada_b)

    ck = cache_k.reshape(n_pool, depth, CHUNK, BRANCH_W)
    cv = cache_v.reshape(n_pool, depth, CHUNK, BRANCH_W)

    yp = x_prompt[0]
    ys = x_sample[:, 0]
    outs = {n: [] for n in ('kp', 'vp', 'ks', 'vs', 'cp', 'cs', 'sp', 'ss', 'av')}
    for l in range(depth):
        wts = _layer_weights(l, lbs[l], w_in, a_ln_g, a_ln_b, a_ws, a_bs, b_qn_g, b_kn_g, c_conv_w,
                             c_conv_b, c_ln_g, c_ln_b, d_norm_g, w_branch, w_out, w_ff1, w_ff2)
        mp = [mods[l, 0:1, i * D_MODEL:(i + 1) * D_MODEL] for i in range(6)]
        ms = [mods[l, 1:1 + nb, i * D_MODEL:(i + 1) * D_MODEL] for i in range(6)]
        n1 = norm1_g[l][None]
        n2 = norm2_g[l][None]
        bias_rows = jnp.zeros((Q_ROWS, ATT_BLK), F32).at[:HEADS].set(
            jnp.broadcast_to(b_bias[l][:, None], (HEADS, ATT_BLK)))

        q, k, v, kt, vb, macd, gb, conv_tail, st = _front_call(yp, n1, mp[1], mp[0], wts, consts)
        ob = _attn_call(bias_rows, consts['uo'], q, kt, vb)
        yp = _back_call(yp, ob, macd, gb, mp[2], n2, mp[4], mp[3], mp[5], wts, ROW_TILE)
        outs['kp'].append(k.reshape(1, seq, HEADS, HEAD_DIM))
        outs['vp'].append(v.reshape(1, seq, HEADS, HEAD_DIM))
        outs['cp'].append(conv_tail[CONV_HALO - (CONV_W - 1):][None])
        st4 = st.reshape(HEADS, HEAD_DIM, HEADS, HEAD_DIM)
        outs['sp'].append(jnp.stack([st4[h, :, h, :].T for h in range(HEADS)])[None])

        conv_t = state_conv[:, l].transpose(1, 0, 2)
        s0_r = state_hgrn[:, l].transpose(0, 2, 1, 3).reshape(nb, HEAD_DIM, BRANCH_W)
        qs, ksn, vsn, macd_s, gb_s, conv_new, s_new, avn = _sample_front_call(
            ys, n1, ms[1], ms[0], wts, consts, conv_t, s0_r)
        obs = _paged_attn_call(page_table, qs, bias_rows, consts['uo'], ck, cv, l)
        ys = _back_call(ys, obs[:, 0], macd_s, gb_s, ms[2], n2, ms[4], ms[3], ms[5], wts, nb)
        outs['ks'].append(ksn.reshape(nb, 1, HEADS, HEAD_DIM))
        outs['vs'].append(vsn.reshape(nb, 1, HEADS, HEAD_DIM))
        outs['cs'].append(conv_new.transpose(1, 0, 2))
        outs['ss'].append(s_new.reshape(nb, HEAD_DIM, HEADS, HEAD_DIM).transpose(0, 2, 1, 3))
        outs['av'].append(avn[:, None, :])

    stack = lambda n: jnp.stack(outs[n], axis=1)
    return (yp[None], ys[:, None, :],
            stack('kp'), stack('vp'), stack('ks'), stack('vs'),
            stack('cp'), stack('cs'), stack('sp'), stack('ss'), stack('av'))
```

```python
import functools

import jax
import jax.numpy as jnp
from jax import lax
from jax.experimental import pallas as pl
from jax.experimental.pallas import tpu as pltpu

F32 = jnp.float32
BF16 = jnp.bfloat16

D_MODEL = 1024
BRANCH_W = 256
HEADS = 4
HEAD_DIM = 64
CHUNK = 128
CONV_W = 31
D_FF = 4096
EPS = 1e-6
QK_SCALE = HEAD_DIM ** -0.5
LOG2E = 1.4426950408889634
MASKED_EXPONENT = -1e30

ROW_TILE = 256
SUB = 16
CONV_HALO = 32
CONV_SHIFTS = 8
CONV_BASE = CONV_HALO + CONV_SHIFTS
ATT_BLK = 128
ATT_Q = 256
PAGES_PER_STEP = 8
VMEM_LIMIT = 56 * 1024 * 1024


def _dot(a, b):
    return jnp.dot(a, b, preferred_element_type=F32)


def _bf16_pieces(a, n):
    pieces, rem = [], a
    for _ in range(n):
        p = rem.astype(BF16)
        pieces.append(p)
        rem = rem - p.astype(F32)
    return pieces


def _dot_split_lhs(a, ones_bf, n):
    return _dot(jnp.concatenate(_bf16_pieces(a, n), axis=1), jnp.concatenate([ones_bf] * n, axis=0))


def _dot_split_rhs(ones_bf, b, n):
    return _dot(jnp.concatenate([ones_bf] * n, axis=1), jnp.concatenate(_bf16_pieces(b, n), axis=0))


def _dot_nt(a, b):
    return lax.dot_general(a, b, (((1,), (1,)), ((), ())), preferred_element_type=F32)


def _dot_tn(a, b):
    return lax.dot_general(a, b, (((0,), (0,)), ((), ())), preferred_element_type=F32)


def _sigmoid(x):
    return 0.5 * jnp.tanh(0.5 * x) + 0.5


def _silu(x):
    return x * _sigmoid(x)


def _softplus(x):
    return jnp.maximum(x, 0.0) + jnp.log1p(jnp.exp(-jnp.abs(x)))


def _rms_rows(x):
    return x * lax.rsqrt(jnp.mean(x * x, axis=-1, keepdims=True) + EPS)


def _layer_norm(x, g, b):
    mu = jnp.mean(x, axis=-1, keepdims=True)
    xc = x - mu
    var = jnp.mean(xc * xc, axis=-1, keepdims=True)
    return xc * lax.rsqrt(var + EPS) * g + b


def _head_rms(x, hmask, g):
    ms = _dot_split_lhs(x * x, hmask, 2) * (1.0 / HEAD_DIM)
    return x * lax.rsqrt(ms + EPS) * g


def _log_forget(df, lb):
    ls = -_softplus(-df)
    b = jnp.log1p(-lb) + ls
    pos = lb > 0.0
    a = jnp.log(jnp.where(pos, lb, 1.0))
    lae = jnp.maximum(a, b) + jnp.log1p(jnp.exp(-jnp.abs(a - b)))
    return jnp.where(pos, lae, b)


def _one_minus_forget(df, lb):
    return (1.0 - lb) * _sigmoid(-df)


def _const_spec(shape):
    nd = len(shape)
    return pl.BlockSpec(shape, lambda *_: (0,) * nd)


def _ada_kernel(c_ref, w_ref, b_ref, o_ref):
    s = _silu(c_ref[...]).astype(BF16)
    o_ref[...] = _dot(s, w_ref[...].astype(BF16)) + b_ref[...]


def _ada_call(c_all, ada_w, ada_b):
    depth = ada_w.shape[0]
    rows = c_all.shape[0]
    n_col = ada_w.shape[2] // D_MODEL
    return pl.pallas_call(
        _ada_kernel,
        out_shape=jax.ShapeDtypeStruct((depth, rows, ada_w.shape[2]), F32),
        grid=(depth, n_col),
        in_specs=[
            pl.BlockSpec((rows, D_MODEL), lambda l, j: (0, 0)),
            pl.BlockSpec((None, D_MODEL, D_MODEL), lambda l, j: (l, 0, j)),
            pl.BlockSpec((None, 1, D_MODEL), lambda l, j: (l, 0, j)),
        ],
        out_specs=pl.BlockSpec((None, rows, D_MODEL), lambda l, j: (l, 0, j)),
        compiler_params=pltpu.CompilerParams(vmem_limit_bytes=VMEM_LIMIT),
        name="ada_params",
    )(c_all, ada_w, ada_b.reshape(depth, 1, -1))


P_ALN_G, P_ALN_B, P_QN_G, P_KN_G, P_CONV_B, P_CLN_G, P_CLN_B, P_LB, P_DN_G = range(9)


def _gated_local_merge(h, wg_ref, out_a, out_c, out_d, wbr_ref):
    gate = lambda k: _sigmoid(_dot(h, wg_ref[:, k * D_MODEL:(k + 1) * D_MODEL]))
    macd = gate(0) * _dot(out_a.astype(BF16), wbr_ref[0])
    macd = macd + gate(2) * _dot(out_c.astype(BF16), wbr_ref[2])
    macd = macd + gate(3) * _dot(out_d.astype(BF16), wbr_ref[3])
    return macd, gate(1)


def _front_kernel(x_ref, g1_ref, sc_ref, sh_ref, wa_ref, wb_ref, wc_ref, wd_ref, wg_ref, wbr_ref,
                  p_ref, aws_ref, abias_ref, convw_ref, hmask_ref, tri_ref,
                  q_ref, k_ref, v_ref, kt_ref, vb_ref, macd_ref, gb_ref, convnew_ref, state_ref,
                  convbuf, st_ref):
    tm = x_ref.shape[0]
    step = pl.program_id(0)

    @pl.when(step == 0)
    def _():
        convbuf[...] = jnp.zeros_like(convbuf)
        st_ref[...] = jnp.zeros_like(st_ref)

    prm = p_ref[...]
    row = lambda i: prm[i:i + 1, :]
    hmask = hmask_ref[...]
    hmask_f32 = hmask.astype(F32)

    x = x_ref[...]
    h = (_rms_rows(x) * g1_ref[...] * (1.0 + sc_ref[...]) + sh_ref[...]).astype(BF16)

    za = _dot(h, wa_ref[...])
    zb = _dot(h, wb_ref[...])
    zc = _dot(h, wc_ref[...])
    zd = _dot(h, wd_ref[...])
    gate = lambda k: _sigmoid(_dot(h, wg_ref[:, k * D_MODEL:(k + 1) * D_MODEL]))

    n_c = tm // SUB
    in_chunks = lambda a: a.reshape(n_c, SUB, BRANCH_W)
    qd = _silu(zd[:, :BRANCH_W]) * QK_SCALE
    logf = _log_forget(zd[:, BRANCH_W:2 * BRANCH_W], row(P_LB))
    kd = _one_minus_forget(zd[:, BRANCH_W:2 * BRANCH_W], row(P_LB))
    di = zd[:, 2 * BRANCH_W:3 * BRANCH_W]
    b = _dot_split_rhs(tri_ref[...], logf, 3)
    b3, qd3, kd3, di3 = in_chunks(b), in_chunks(qd), in_chunks(kd), in_chunks(di)
    bl3 = b3[:, SUB - 1:SUB, :]
    qt = (qd * jnp.exp(b)).astype(BF16)
    kt = (kd3 * jnp.exp(bl3 - b3)).reshape(tm, BRANCH_W).astype(BF16)
    el = jnp.exp(bl3)
    di_bf = di.astype(BF16)
    increments = [_dot_tn(di_bf[c * SUB:(c + 1) * SUB], kt[c * SUB:(c + 1) * SUB]) * hmask_f32
                  for c in range(n_c)]

    ga = jax.nn.gelu(za)
    u = ga[:, :BRANCH_W]
    vn = _layer_norm(ga[:, BRANCH_W:], row(P_ALN_G), row(P_ALN_B))
    lane_grp = lax.broadcasted_iota(jnp.int32, (CHUNK, BRANCH_W), 1) // (BRANCH_W // HEADS)
    tril = (lax.broadcasted_iota(jnp.int32, (CHUNK, CHUNK), 1)
            <= lax.broadcasted_iota(jnp.int32, (CHUNK, CHUNK), 0))
    w_causal = [jnp.where(tril, aws_ref[g], 0.0).astype(BF16) for g in range(HEADS)]
    mixed_chunks = []
    for c in range(tm // CHUNK):
        vc = vn[c * CHUNK:(c + 1) * CHUNK, :].astype(BF16)
        mixed = None
        for g in range(HEADS):
            mg = _dot(w_causal[g], vc)
            mixed = mg if mixed is None else jnp.where(lane_grp == g, mg, mixed)
        mixed_chunks.append(mixed + abias_ref[...])
    out_a = u * jnp.concatenate(mixed_chunks, axis=0)

    gb_ref[...] = gate(1)

    qn = _head_rms(zb[:, :BRANCH_W], hmask, row(P_QN_G))
    kn = _head_rms(zb[:, BRANCH_W:2 * BRANCH_W], hmask, row(P_KN_G))
    vv = zb[:, 2 * BRANCH_W:]
    q_ref[...] = (qn * QK_SCALE).astype(BF16)
    k_ref[...] = kn
    v_ref[...] = vv
    vb_ref[...] = vv.astype(BF16)
    knt = kn.T.astype(BF16)
    for c in range(tm // ATT_BLK):
        kt_ref[c] = knt[:, c * ATT_BLK:(c + 1) * ATT_BLK]

    gate_a = gate(0)

    cin = zc[:, :BRANCH_W] * _sigmoid(zc[:, BRANCH_W:])
    for r in range(CONV_SHIFTS):
        convbuf[r, CONV_BASE - r:CONV_BASE - r + tm, :] = cin
    acc = jnp.broadcast_to(row(P_CONV_B), (tm, BRANCH_W))
    for j in range(CONV_W):
        lead = CONV_BASE - (CONV_W - 1) + j
        r = lead % CONV_SHIFTS
        acc = acc + convw_ref[j:j + 1, :] * convbuf[r, lead - r:lead - r + tm, :]
    out_c = _silu(_layer_norm(acc, row(P_CLN_G), row(P_CLN_B)))
    convnew_ref[...] = convbuf[0, tm + CONV_BASE - CONV_HALO:tm + CONV_BASE, :]
    for r in range(CONV_SHIFTS):
        convbuf[r, 0:CONV_BASE, :] = convbuf[r, tm:tm + CONV_BASE, :]

    gate_c = gate(2)

    t_idx = lax.broadcasted_iota(jnp.int32, (n_c, SUB, BRANCH_W), 1)
    od = jnp.zeros((tm, BRANCH_W), F32)
    for s in range(SUB):
        e = jnp.exp(jnp.where(t_idx >= s, b3 - b3[:, s:s + 1, :], MASKED_EXPONENT))
        wgt = e * qd3 * kd3[:, s:s + 1, :]
        coef = _dot(wgt.reshape(tm, BRANCH_W).astype(BF16), hmask)
        i_s = jnp.broadcast_to(di3[:, s:s + 1, :], (n_c, SUB, BRANCH_W)).reshape(tm, BRANCH_W)
        od = od + coef * i_s
    gate_d = gate(3)

    st = st_ref[...]
    o_inter = []
    for c in range(n_c):
        o_inter.append(_dot_nt(qt[c * SUB:(c + 1) * SUB], st.astype(BF16)))
        st = st * el[c] + increments[c]
    st_ref[...] = st
    state_ref[...] = st
    od = od + jnp.concatenate(o_inter, axis=0)
    out_d = _head_rms(od, hmask, row(P_DN_G)) * _silu(zd[:, 3 * BRANCH_W:])

    macd = gate_a * _dot(out_a.astype(BF16), wbr_ref[0])
    macd = macd + gate_c * _dot(out_c.astype(BF16), wbr_ref[2])
    macd_ref[...] = macd + gate_d * _dot(out_d.astype(BF16), wbr_ref[3])


def _front_call(x, g1, sc, sh, wts, consts):
    length = x.shape[0]
    tm = ROW_TILE
    n_steps = length // tm
    row_spec = lambda w: pl.BlockSpec((tm, w), lambda i: (i, 0))
    in_arrays = [x, g1, sc, sh, wts['wa'], wts['wb'], wts['wc'], wts['wd'], wts['wg'], wts['wbr'],
                 wts['ptab'], wts['a_ws'], wts['a_bias'], wts['conv_w'],
                 consts['hmask'], consts['tri']]
    in_specs = [row_spec(D_MODEL)] + [_const_spec(a.shape) for a in in_arrays[1:]]
    out_shape = [
        jax.ShapeDtypeStruct((length, BRANCH_W), BF16),
        jax.ShapeDtypeStruct((length, BRANCH_W), F32),
        jax.ShapeDtypeStruct((length, BRANCH_W), F32),
        jax.ShapeDtypeStruct((length // ATT_BLK, BRANCH_W, ATT_BLK), BF16),
        jax.ShapeDtypeStruct((length, BRANCH_W), BF16),
        jax.ShapeDtypeStruct((length, D_MODEL), F32),
        jax.ShapeDtypeStruct((length, D_MODEL), F32),
        jax.ShapeDtypeStruct((CONV_HALO, BRANCH_W), F32),
        jax.ShapeDtypeStruct((BRANCH_W, BRANCH_W), F32),
    ]
    out_specs = [
        row_spec(BRANCH_W), row_spec(BRANCH_W), row_spec(BRANCH_W),
        pl.BlockSpec((tm // ATT_BLK, BRANCH_W, ATT_BLK), lambda i: (i, 0, 0)),
        row_spec(BRANCH_W), row_spec(D_MODEL), row_spec(D_MODEL),
        _const_spec((CONV_HALO, BRANCH_W)), _const_spec((BRANCH_W, BRANCH_W)),
    ]
    scratch = [
        pltpu.VMEM((CONV_SHIFTS, tm + CONV_BASE, BRANCH_W), F32),
        pltpu.VMEM((BRANCH_W, BRANCH_W), F32),
    ]
    return pl.pallas_call(
        _front_kernel,
        out_shape=out_shape,
        grid=(n_steps,),
        in_specs=in_specs,
        out_specs=out_specs,
        scratch_shapes=scratch,
        compiler_params=pltpu.CompilerParams(
            dimension_semantics=("arbitrary",), vmem_limit_bytes=VMEM_LIMIT),
        name="prompt_front",
    )(*in_arrays)


def _sb_scores(z, uo, causal):
    sp = jnp.maximum(z, 0.0) + jnp.log(1.0 + jnp.exp2(jnp.abs(z) * (-LOG2E)))
    if causal is not None:
        sp = jnp.where(causal, sp, 0.0)
    hi = sp.astype(BF16)
    lo = (sp - hi.astype(F32)).astype(BF16)
    cs = _dot(jnp.concatenate([hi, lo], axis=1), uo)
    return z - cs[:, :ATT_BLK], cs[:, ATT_BLK:]


def _sb_weights(d, r, causal):
    w = jnp.exp(d - r)
    if causal is not None:
        w = jnp.where(causal, w, 0.0)
    return w


def _attn_kernel(brows_ref, uo_ref, q_ref, kt_ref, v_ref, o_ref, acc_ref, r_ref, z_s, d_s, t_s):
    i = pl.program_id(0)
    n_sub = ATT_Q // ATT_BLK
    n_old = i * n_sub
    q = q_ref[...]
    unit_cols = jnp.where(lax.broadcasted_iota(jnp.int32, (ATT_Q, HEAD_DIM), 1) < 2, 1.0, 0.0).astype(BF16)
    q_ext = [jnp.concatenate([q[:, HEAD_DIM * h:HEAD_DIM * (h + 1)], unit_cols], axis=1)
             for h in range(HEADS)]
    acc_ref[...] = jnp.zeros_like(acc_ref)
    r_ref[...] = jnp.zeros_like(r_ref)
    lane_head = lax.broadcasted_iota(jnp.int32, (ATT_BLK, BRANCH_W), 1) // HEAD_DIM
    head_lanes = [jnp.where(lane_head == h, 1.0, 0.0).astype(BF16) for h in range(HEADS)]
    causal = (lax.broadcasted_iota(jnp.int32, (ATT_BLK, ATT_BLK), 1)
              < lax.broadcasted_iota(jnp.int32, (ATT_BLK, ATT_BLK), 0))

    def logits(j, h, row0, n_rows):
        kt_h = jnp.concatenate([kt_ref[j, HEAD_DIM * h:HEAD_DIM * (h + 1), :], brows_ref[h]], axis=0)
        return _dot(q_ext[h][row0:row0 + n_rows, :], kt_h)

    def values(j):
        vb = v_ref[pl.ds(pl.multiple_of(j * ATT_BLK, ATT_BLK), ATT_BLK), :]
        return jnp.concatenate([vb * head_lanes[h] for h in range(HEADS)], axis=0)

    def attend(j, row0, n_rows, mask):
        ws = []
        for h in range(HEADS):
            d, tot = _sb_scores(logits(j, h, row0, n_rows), uo_ref[...], mask)
            r = r_ref[h, row0:row0 + n_rows, :]
            ws.append(_sb_weights(d, r, mask).astype(BF16))
            r_ref[h, row0:row0 + n_rows, :] = r + tot
        acc_ref[row0:row0 + n_rows, :] += _dot(jnp.concatenate(ws, axis=1), values(j))

    for kb in reversed(range(n_sub)):
        attend(n_old + kb, kb * ATT_BLK, ATT_BLK, causal)
        if kb + 1 < n_sub:
            attend(n_old + kb, (kb + 1) * ATT_BLK, (n_sub - kb - 1) * ATT_BLK, None)

    def block_of(t):
        return jnp.maximum(n_old - 1 - t, 0)

    def stage1(t, slot):
        for h in range(HEADS):
            z_s[slot, h] = logits(block_of(t), h, 0, ATT_Q)

    def stage2(slot):
        for h in range(HEADS):
            d, tot = _sb_scores(z_s[slot, h], uo_ref[...], None)
            d_s[slot, h] = d
            t_s[slot, h] = tot

    def stage3(t, slot):
        ws = []
        for h in range(HEADS):
            r = r_ref[h]
            ws.append(_sb_weights(d_s[slot, h], r, None).astype(BF16))
            r_ref[h] = r + t_s[slot, h]
        acc_ref[...] += _dot(jnp.concatenate(ws, axis=1), values(block_of(t)))

    stage1(0, 0)
    stage2(0)
    stage1(1, 1)

    def older_pair(u, carry):
        for slot in range(2):
            t = 2 * u + slot
            stage1(t + 2, slot)
            stage3(t, slot)
            stage2(1 - slot)
        return carry

    lax.fori_loop(0, n_old // 2, older_pair, 0)
    o_ref[...] = acc_ref[...]


def _attn_call(brows, uo, q, kt, vb):
    length = q.shape[0]
    return pl.pallas_call(
        _attn_kernel,
        out_shape=jax.ShapeDtypeStruct((length, BRANCH_W), F32),
        grid=(length // ATT_Q,),
        in_specs=[
            _const_spec(brows.shape), _const_spec(uo.shape),
            pl.BlockSpec((ATT_Q, BRANCH_W), lambda i: (i, 0)),
            _const_spec(kt.shape), _const_spec(vb.shape),
        ],
        out_specs=pl.BlockSpec((ATT_Q, BRANCH_W), lambda i: (i, 0)),
        scratch_shapes=[
            pltpu.VMEM((ATT_Q, BRANCH_W), F32),
            pltpu.VMEM((HEADS, ATT_Q, ATT_BLK), F32),
            pltpu.VMEM((2, HEADS, ATT_Q, ATT_BLK), F32),
            pltpu.VMEM((2, HEADS, ATT_Q, ATT_BLK), F32),
            pltpu.VMEM((2, HEADS, ATT_Q, ATT_BLK), F32),
        ],
        compiler_params=pltpu.CompilerParams(
            dimension_semantics=("arbitrary",), vmem_limit_bytes=VMEM_LIMIT),
        name="prompt_attn",
    )(brows, uo, q, kt, vb)


def _back_kernel(x_ref, ob_ref, macd_ref, gb_ref, g1_ref, n2_ref, sc2_ref, sh2_ref, g2_ref,
                 wb1_ref, wo_ref, w1_ref, w2_ref, y_ref):
    merged = macd_ref[...] + gb_ref[...] * _dot(ob_ref[...].astype(BF16), wb1_ref[...])
    x1 = x_ref[...] + g1_ref[...] * _dot(merged.astype(BF16), wo_ref[...])
    h2 = (_rms_rows(x1) * n2_ref[...] * (1.0 + sc2_ref[...]) + sh2_ref[...]).astype(BF16)
    hid = jnp.maximum(_dot(h2, w1_ref[...]), 0.0)
    y_ref[...] = x1 + g2_ref[...] * _dot((hid * hid).astype(BF16), w2_ref[...])


def _back_call(x, ob, macd, gb, g1, n2, sc2, sh2, g2, wts, tm):
    rows = x.shape[0]
    mod_rows = g1.shape[0]
    if mod_rows == 1:
        mod_spec = _const_spec((1, D_MODEL))
    else:
        mod_spec = pl.BlockSpec((tm, D_MODEL), lambda i: (i, 0))
    row_spec = lambda w: pl.BlockSpec((tm, w), lambda i: (i, 0))
    weights = [wts['wb1'], wts['wo'], wts['w1'], wts['w2']]
    return pl.pallas_call(
        _back_kernel,
        out_shape=jax.ShapeDtypeStruct((rows, D_MODEL), F32),
        grid=(rows // tm,),
        in_specs=[row_spec(D_MODEL), row_spec(BRANCH_W), row_spec(D_MODEL), row_spec(D_MODEL),
                  mod_spec, _const_spec((1, D_MODEL)), mod_spec, mod_spec, mod_spec]
                 + [_const_spec(w.shape) for w in weights],
        out_specs=row_spec(D_MODEL),
        compiler_params=pltpu.CompilerParams(
            dimension_semantics=("arbitrary",), vmem_limit_bytes=VMEM_LIMIT),
        name="layer_back",
    )(x, ob, macd, gb, g1, n2, sc2, sh2, g2, *weights)


def _sample_front_kernel(x_ref, g1_ref, sc_ref, sh_ref, wa_ref, wb_ref, wc_ref, wd_ref, wg_ref, wbr_ref,
                         p_ref, a0_ref, convw_ref, hmask_ref, tdiag_ref, conv_ref, s0_ref,
                         q_ref, k_ref, v_ref, macd_ref, gb_ref, convnew_ref, snew_ref, av_ref):
    nb = x_ref.shape[0]
    prm = p_ref[...]
    row = lambda i: prm[i:i + 1, :]
    hmask = hmask_ref[...]

    x = x_ref[...]
    h = (_rms_rows(x) * g1_ref[...] * (1.0 + sc_ref[...]) + sh_ref[...]).astype(BF16)

    ga = jax.nn.gelu(_dot(h, wa_ref[...]))
    vn = _layer_norm(ga[:, BRANCH_W:], row(P_ALN_G), row(P_ALN_B))
    av_ref[...] = vn
    out_a = ga[:, :BRANCH_W] * (a0_ref[0:1, :] * vn + a0_ref[1:2, :])

    zb = _dot(h, wb_ref[...])
    qn = _head_rms(zb[:, :BRANCH_W], hmask, row(P_QN_G))
    kn = _head_rms(zb[:, BRANCH_W:2 * BRANCH_W], hmask, row(P_KN_G))
    q_ref[...] = qn * QK_SCALE
    k_ref[...] = kn
    v_ref[...] = zb[:, 2 * BRANCH_W:]

    zc = _dot(h, wc_ref[...])
    cin = zc[:, :BRANCH_W] * _sigmoid(zc[:, BRANCH_W:])
    acc = row(P_CONV_B) + convw_ref[CONV_W - 1:CONV_W, :] * cin
    for j in range(CONV_W - 1):
        acc = acc + convw_ref[j:j + 1, :] * conv_ref[j]
        if j >= 1:
            convnew_ref[j - 1] = conv_ref[j]
    convnew_ref[CONV_W - 2] = cin
    out_c = _silu(_layer_norm(acc, row(P_CLN_G), row(P_CLN_B)))

    zd = _dot(h, wd_ref[...])
    qd = _silu(zd[:, :BRANCH_W]) * QK_SCALE
    logf = _log_forget(zd[:, BRANCH_W:2 * BRANCH_W], row(P_LB))
    ea = jnp.exp(logf)
    kd = _one_minus_forget(zd[:, BRANCH_W:2 * BRANCH_W], row(P_LB))
    di = zd[:, 2 * BRANCH_W:3 * BRANCH_W]
    tdiag = tdiag_ref[...]

    def spread(r):
        t3 = jnp.broadcast_to(r[:, None, :], (nb, HEAD_DIM, BRANCH_W)) * tdiag[None]
        flat = _dot_split_lhs(t3.reshape(nb * HEAD_DIM, BRANCH_W), hmask, 3)
        return flat.reshape(nb, HEAD_DIM, BRANCH_W)

    s0 = s0_ref[...]
    od = jnp.sum(spread(qd * ea) * s0, axis=1) + _dot_split_lhs(qd * kd, hmask, 3) * di
    snew_ref[...] = spread(ea) * s0 + spread(kd) * di[:, None, :]
    out_d = _head_rms(od, hmask, row(P_DN_G)) * _silu(zd[:, 3 * BRANCH_W:])

    macd, gb = _gated_local_merge(h, wg_ref, out_a, out_c, out_d, wbr_ref)
    macd_ref[...] = macd
    gb_ref[...] = gb


def _sample_front_call(x, g1, sc, sh, wts, consts, conv_t, s0_r):
    nb = x.shape[0]
    in_arrays = [x, g1, sc, sh, wts['wa'], wts['wb'], wts['wc'], wts['wd'], wts['wg'], wts['wbr'],
                 wts['ptab'], wts['a0'], wts['conv_w'], consts['hmask'], consts['tdiag'], conv_t, s0_r]
    out_shape = [
        jax.ShapeDtypeStruct((nb, BRANCH_W), F32),
        jax.ShapeDtypeStruct((nb, BRANCH_W), F32),
        jax.ShapeDtypeStruct((nb, BRANCH_W), F32),
        jax.ShapeDtypeStruct((nb, D_MODEL), F32),
        jax.ShapeDtypeStruct((nb, D_MODEL), F32),
        jax.ShapeDtypeStruct(conv_t.shape, F32),
        jax.ShapeDtypeStruct(s0_r.shape, F32),
        jax.ShapeDtypeStruct((nb, BRANCH_W), F32),
    ]
    return pl.pallas_call(
        _sample_front_kernel,
        out_shape=out_shape,
        grid=(1,),
        in_specs=[_const_spec(a.shape) for a in in_arrays],
        out_specs=[_const_spec(s.shape) for s in out_shape],
        compiler_params=pltpu.CompilerParams(vmem_limit_bytes=VMEM_LIMIT),
        name="sample_front",
    )(*in_arrays)


Q_ROWS = 16


def _paged_attn_kernel(pt_ref, q_ref, bias_ref, uo_ref, *refs):
    k_refs = refs[:PAGES_PER_STEP]
    v_refs = refs[PAGES_PER_STEP:2 * PAGES_PER_STEP]
    o_ref, acc_ref, r_ref = refs[2 * PAGES_PER_STEP:]
    jc = pl.program_id(1)

    @pl.when(jc == 0)
    def _():
        acc_ref[...] = jnp.zeros_like(acc_ref)
        r_ref[...] = jnp.zeros_like(r_ref)

    row_id = lax.broadcasted_iota(jnp.int32, (Q_ROWS, BRANCH_W), 0)
    own_head = row_id == lax.broadcasted_iota(jnp.int32, (Q_ROWS, BRANCH_W), 1) // HEAD_DIM
    qrows = jnp.where(own_head, jnp.broadcast_to(q_ref[...], (Q_ROWS, BRANCH_W)), 0.0).astype(BF16)
    acc = acc_ref[...]
    r = r_ref[...]
    for p in reversed(range(PAGES_PER_STEP)):
        z = _dot_nt(qrows, k_refs[p][...].astype(BF16)) + bias_ref[...]
        d, tot = _sb_scores(z, uo_ref[...], None)
        w = _sb_weights(d, r, None)
        r = r + tot
        acc = acc + _dot(w.astype(BF16), v_refs[p][...].astype(BF16))
    acc_ref[...] = acc
    r_ref[...] = r

    @pl.when(jc == pl.num_programs(1) - 1)
    def _():
        o_ref[...] = jnp.sum(jnp.where(own_head, acc, 0.0), axis=0, keepdims=True)


def _paged_attn_call(page_table, q, bias_rows, uo, cache_k, cache_v, layer):
    nb, n_pages = page_table.shape
    n_chunks = n_pages // PAGES_PER_STEP

    def page_spec(p):
        def index_map(b, jc, pt):
            return (pt[b, (n_chunks - 1 - jc) * PAGES_PER_STEP + p], layer, 0, 0)
        return pl.BlockSpec((None, None, CHUNK, BRANCH_W), index_map)

    page_specs = [page_spec(p) for p in range(PAGES_PER_STEP)]
    grid_spec = pltpu.PrefetchScalarGridSpec(
        num_scalar_prefetch=1,
        grid=(nb, n_chunks),
        in_specs=[
            pl.BlockSpec((None, 1, BRANCH_W), lambda b, jc, pt: (b, 0, 0)),
            pl.BlockSpec(bias_rows.shape, lambda b, jc, pt: (0, 0)),
            pl.BlockSpec(uo.shape, lambda b, jc, pt: (0, 0)),
        ] + page_specs + page_specs,
        out_specs=pl.BlockSpec((None, 1, BRANCH_W), lambda b, jc, pt: (b, 0, 0)),
        scratch_shapes=[
            pltpu.VMEM((Q_ROWS, BRANCH_W), F32),
            pltpu.VMEM((Q_ROWS, CHUNK), F32),
        ],
    )
    return pl.pallas_call(
        _paged_attn_kernel,
        out_shape=jax.ShapeDtypeStruct((nb, 1, BRANCH_W), F32),
        grid_spec=grid_spec,
        compiler_params=pltpu.CompilerParams(
            dimension_semantics=("arbitrary", "arbitrary"), vmem_limit_bytes=VMEM_LIMIT),
        name="sample_paged_attn",
    )(page_table, q.reshape(nb, 1, BRANCH_W), bias_rows, uo,
      *([cache_k] * PAGES_PER_STEP), *([cache_v] * PAGES_PER_STEP))


def _constants():
    lane = jnp.arange(BRANCH_W)
    hmask = (lane[:, None] // HEAD_DIM == lane[None, :] // HEAD_DIM).astype(BF16)
    r = jnp.arange(ROW_TILE)
    tri = ((r[:, None] // SUB == r[None, :] // SUB) & (r[None, :] <= r[:, None])).astype(BF16)
    tdiag = (jnp.arange(HEAD_DIM)[:, None] == lane[None, :] % HEAD_DIM).astype(F32)
    j = jnp.arange(2 * ATT_BLK) % ATT_BLK
    c = jnp.arange(2 * ATT_BLK)
    uo = ((c[None, :] >= ATT_BLK) | (j[:, None] >= c[None, :])).astype(BF16)
    return {'hmask': hmask, 'tri': tri, 'tdiag': tdiag, 'uo': uo}


def _layer_weights(l, lb, w_in, a_ln_g, a_ln_b, a_ws, a_bs, b_qn_g, b_kn_g, c_conv_w, c_conv_b,
                   c_ln_g, c_ln_b, d_norm_g, w_branch, w_out, w_ff1, w_ff2):
    wi = w_in[l].astype(BF16)
    c0 = 2 * BRANCH_W
    c1 = c0 + 3 * BRANCH_W
    c2 = c1 + 2 * BRANCH_W
    c3 = c2 + 4 * BRANCH_W
    tile4 = lambda g: jnp.tile(g, HEADS)
    rows = [a_ln_g[l], a_ln_b[l], tile4(b_qn_g[l]), tile4(b_kn_g[l]), c_conv_b[l], c_ln_g[l], c_ln_b[l],
            lb, tile4(d_norm_g[l])]
    ptab = jnp.zeros((16, BRANCH_W), F32).at[:len(rows)].set(jnp.stack(rows))
    grp = BRANCH_W // HEADS
    wbr = w_branch[l].astype(BF16)
    return {
        'wa': wi[:, :c0], 'wb': wi[:, c0:c1], 'wc': wi[:, c1:c2], 'wd': wi[:, c2:c3], 'wg': wi[:, c3:],
        'wbr': wbr, 'wb1': wbr[1], 'ptab': ptab,
        'a_ws': a_ws[l], 'a_bias': jnp.repeat(a_bs[l].T, grp, axis=1),
        'a0': jnp.stack([jnp.repeat(a_ws[l][:, 0, 0], grp), jnp.repeat(a_bs[l][:, 0], grp)]),
        'conv_w': jnp.zeros((32, BRANCH_W), F32).at[:CONV_W].set(c_conv_w[l]),
        'wo': w_out[l].astype(BF16), 'w1': w_ff1[l].astype(BF16), 'w2': w_ff2[l].astype(BF16),
    }


def kernel(x_prompt, x_sample, c_prompt, c_sample, cache_k, cache_v, state_conv, state_hgrn, page_table,
           ada_w, ada_b, norm1_g, norm2_g, w_in, a_ln_g, a_ln_b, a_ws, a_bs, b_qn_g, b_kn_g, b_bias,
           c_conv_w, c_conv_b, c_ln_g, c_ln_b, d_lb_logits, d_norm_g, w_branch, w_out, w_ff1, w_ff2):
    depth = w_in.shape[0]
    seq = x_prompt.shape[1]
    nb = x_sample.shape[0]
    n_pool = cache_k.shape[0]
    consts = _constants()

    cp = jnp.cumsum(jax.nn.softmax(d_lb_logits.astype(F32), axis=0), axis=0)
    lbs = cp - cp[0:1]

    ada_rows = 8 * ((1 + nb + 7) // 8)
    c_all = jnp.zeros((ada_rows, D_MODEL), F32).at[0:1].set(c_prompt).at[1:1 + nb].set(c_sample)
    mods = _ada_call(c_all, ada_w, ada_b)

    ck = cache_k.reshape(n_pool, depth, CHUNK, BRANCH_W)
    cv = cache_v.reshape(n_pool, depth, CHUNK, BRANCH_W)

    yp = x_prompt[0]
    ys = x_sample[:, 0]
    outs = {n: [] for n in ('kp', 'vp', 'ks', 'vs', 'cp', 'cs', 'sp', 'ss', 'av')}
    for l in range(depth):
        wts = _layer_weights(l, lbs[l], w_in, a_ln_g, a_ln_b, a_ws, a_bs, b_qn_g, b_kn_g, c_conv_w,
                             c_conv_b, c_ln_g, c_ln_b, d_norm_g, w_branch, w_out, w_ff1, w_ff2)
        mp = [mods[l, 0:1, i * D_MODEL:(i + 1) * D_MODEL] for i in range(6)]
        ms = [mods[l, 1:1 + nb, i * D_MODEL:(i + 1) * D_MODEL] for i in range(6)]
        n1 = norm1_g[l][None]
        n2 = norm2_g[l][None]
        bias_rows = jnp.zeros((Q_ROWS, ATT_BLK), F32).at[:HEADS].set(
            jnp.broadcast_to(b_bias[l][:, None], (HEADS, ATT_BLK)))
        b_hi = b_bias[l].astype(BF16)
        b_lo = (b_bias[l] - b_hi.astype(F32)).astype(BF16)
        brows = jnp.zeros((HEADS, HEAD_DIM, ATT_BLK), BF16)
        brows = brows.at[:, 0, :].set(b_hi[:, None]).at[:, 1, :].set(b_lo[:, None])

        q, k, v, kt, vb, macd, gb, conv_tail, st = _front_call(yp, n1, mp[1], mp[0], wts, consts)
        ob = _attn_call(brows, consts['uo'], q, kt, vb)
        yp = _back_call(yp, ob, macd, gb, mp[2], n2, mp[4], mp[3], mp[5], wts, ROW_TILE)
        outs['kp'].append(k.reshape(1, seq, HEADS, HEAD_DIM))
        outs['vp'].append(v.reshape(1, seq, HEADS, HEAD_DIM))
        outs['cp'].append(conv_tail[CONV_HALO - (CONV_W - 1):][None])
        st4 = st.reshape(HEADS, HEAD_DIM, HEADS, HEAD_DIM)
        outs['sp'].append(jnp.stack([st4[h, :, h, :].T for h in range(HEADS)])[None])

        conv_t = state_conv[:, l].transpose(1, 0, 2)
        s0_r = state_hgrn[:, l].transpose(0, 2, 1, 3).reshape(nb, HEAD_DIM, BRANCH_W)
        qs, ksn, vsn, macd_s, gb_s, conv_new, s_new, avn = _sample_front_call(
            ys, n1, ms[1], ms[0], wts, consts, conv_t, s0_r)
        obs = _paged_attn_call(page_table, qs, bias_rows, consts['uo'], ck, cv, l)
        ys = _back_call(ys, obs[:, 0], macd_s, gb_s, ms[2], n2, ms[4], ms[3], ms[5], wts, nb)
        outs['ks'].append(ksn.reshape(nb, 1, HEADS, HEAD_DIM))
        outs['vs'].append(vsn.reshape(nb, 1, HEADS, HEAD_DIM))
        outs['cs'].append(conv_new.transpose(1, 0, 2))
        outs['ss'].append(s_new.reshape(nb, HEAD_DIM, HEADS, HEAD_DIM).transpose(0, 2, 1, 3))
        outs['av'].append(avn[:, None, :])

    stack = lambda n: jnp.stack(outs[n], axis=1)
    return (yp[None], ys[:, None, :],
            stack('kp'), stack('vp'), stack('ks'), stack('vs'),
            stack('cp'), stack('cs'), stack('sp'), stack('ss'), stack('av'))
```

```python
import functools

import jax
import jax.numpy as jnp
from jax import lax
from jax.experimental import pallas as pl
from jax.experimental.pallas import tpu as pltpu

F32 = jnp.float32
BF16 = jnp.bfloat16

D_MODEL = 1024
BRANCH_W = 256
HEADS = 4
HEAD_DIM = 64
CHUNK = 128
CONV_W = 31
D_FF = 4096
EPS = 1e-6
QK_SCALE = HEAD_DIM ** -0.5
LOG2E = 1.4426950408889634
MASKED_EXPONENT = -1e30

ROW_TILE = 256
SUB = 16
CONV_HALO = 32
CONV_SHIFTS = 8
CONV_BASE = CONV_HALO + CONV_SHIFTS
ATT_BLK = 128
ATT_Q = 256
PAGES_PER_STEP = 16
VMEM_LIMIT = 56 * 1024 * 1024


def _dot(a, b):
    return jnp.dot(a, b, preferred_element_type=F32)


def _bf16_pieces(a, n):
    pieces, rem = [], a
    for _ in range(n):
        p = rem.astype(BF16)
        pieces.append(p)
        rem = rem - p.astype(F32)
    return pieces


def _dot_split_lhs(a, ones_bf, n):
    return _dot(jnp.concatenate(_bf16_pieces(a, n), axis=1), jnp.concatenate([ones_bf] * n, axis=0))


def _dot_split_rhs(ones_bf, b, n):
    return _dot(jnp.concatenate([ones_bf] * n, axis=1), jnp.concatenate(_bf16_pieces(b, n), axis=0))


def _dot_nt(a, b):
    return lax.dot_general(a, b, (((1,), (1,)), ((), ())), preferred_element_type=F32)


def _dot_tn(a, b):
    return lax.dot_general(a, b, (((0,), (0,)), ((), ())), preferred_element_type=F32)


def _sigmoid(x):
    return 0.5 * jnp.tanh(0.5 * x) + 0.5


def _silu(x):
    return x * _sigmoid(x)


def _softplus(x):
    return jnp.maximum(x, 0.0) + jnp.log1p(jnp.exp(-jnp.abs(x)))


def _rms_rows(x):
    return x * lax.rsqrt(jnp.mean(x * x, axis=-1, keepdims=True) + EPS)


def _layer_norm(x, g, b):
    mu = jnp.mean(x, axis=-1, keepdims=True)
    xc = x - mu
    var = jnp.mean(xc * xc, axis=-1, keepdims=True)
    return xc * lax.rsqrt(var + EPS) * g + b


def _head_rms(x, hmask, g):
    ms = _dot_split_lhs(x * x, hmask, 2) * (1.0 / HEAD_DIM)
    return x * lax.rsqrt(ms + EPS) * g


def _log_forget(df, lb):
    ls = -_softplus(-df)
    b = jnp.log1p(-lb) + ls
    pos = lb > 0.0
    a = jnp.log(jnp.where(pos, lb, 1.0))
    lae = jnp.maximum(a, b) + jnp.log1p(jnp.exp(-jnp.abs(a - b)))
    return jnp.where(pos, lae, b)


def _one_minus_forget(df, lb):
    return (1.0 - lb) * _sigmoid(-df)


def _const_spec(shape):
    nd = len(shape)
    return pl.BlockSpec(shape, lambda *_: (0,) * nd)


def _ada_kernel(c_ref, w_ref, b_ref, o_ref):
    s = _silu(c_ref[...]).astype(BF16)
    o_ref[...] = _dot(s, w_ref[...].astype(BF16)) + b_ref[...]


def _ada_call(c_all, ada_w, ada_b):
    depth = ada_w.shape[0]
    rows = c_all.shape[0]
    n_col = ada_w.shape[2] // D_MODEL
    return pl.pallas_call(
        _ada_kernel,
        out_shape=jax.ShapeDtypeStruct((depth, rows, ada_w.shape[2]), F32),
        grid=(depth, n_col),
        in_specs=[
            pl.BlockSpec((rows, D_MODEL), lambda l, j: (0, 0)),
            pl.BlockSpec((None, D_MODEL, D_MODEL), lambda l, j: (l, 0, j)),
            pl.BlockSpec((None, 1, D_MODEL), lambda l, j: (l, 0, j)),
        ],
        out_specs=pl.BlockSpec((None, rows, D_MODEL), lambda l, j: (l, 0, j)),
        compiler_params=pltpu.CompilerParams(vmem_limit_bytes=VMEM_LIMIT),
        name="ada_params",
    )(c_all, ada_w, ada_b.reshape(depth, 1, -1))


P_ALN_G, P_ALN_B, P_QN_G, P_KN_G, P_CONV_B, P_CLN_G, P_CLN_B, P_LB, P_DN_G = range(9)


def _gated_local_merge(h, wg_ref, out_a, out_c, out_d, wbr_ref):
    gate = lambda k: _sigmoid(_dot(h, wg_ref[:, k * D_MODEL:(k + 1) * D_MODEL]))
    macd = gate(0) * _dot(out_a.astype(BF16), wbr_ref[0])
    macd = macd + gate(2) * _dot(out_c.astype(BF16), wbr_ref[2])
    macd = macd + gate(3) * _dot(out_d.astype(BF16), wbr_ref[3])
    return macd, gate(1)


def _front_kernel(x_ref, g1_ref, sc_ref, sh_ref, wa_ref, wb_ref, wc_ref, wd_ref, wg_ref, wbr_ref,
                  p_ref, aws_ref, abias_ref, convw_ref, hmask_ref, tri_ref,
                  q_ref, k_ref, v_ref, kt_ref, vb_ref, macd_ref, gb_ref, convnew_ref, state_ref,
                  convbuf, st_ref):
    tm = x_ref.shape[0]
    step = pl.program_id(0)

    @pl.when(step == 0)
    def _():
        convbuf[...] = jnp.zeros_like(convbuf)
        st_ref[...] = jnp.zeros_like(st_ref)

    prm = p_ref[...]
    row = lambda i: prm[i:i + 1, :]
    hmask = hmask_ref[...]
    hmask_f32 = hmask.astype(F32)

    x = x_ref[...]
    h = (_rms_rows(x) * g1_ref[...] * (1.0 + sc_ref[...]) + sh_ref[...]).astype(BF16)

    za = _dot(h, wa_ref[...])
    zb = _dot(h, wb_ref[...])
    zc = _dot(h, wc_ref[...])
    zd = _dot(h, wd_ref[...])
    gate = lambda k: _sigmoid(_dot(h, wg_ref[:, k * D_MODEL:(k + 1) * D_MODEL]))

    n_c = tm // SUB
    in_chunks = lambda a: a.reshape(n_c, SUB, BRANCH_W)
    qd = _silu(zd[:, :BRANCH_W]) * QK_SCALE
    logf = _log_forget(zd[:, BRANCH_W:2 * BRANCH_W], row(P_LB))
    kd = _one_minus_forget(zd[:, BRANCH_W:2 * BRANCH_W], row(P_LB))
    di = zd[:, 2 * BRANCH_W:3 * BRANCH_W]
    b = _dot_split_rhs(tri_ref[...], logf, 3)
    b3, qd3, kd3, di3 = in_chunks(b), in_chunks(qd), in_chunks(kd), in_chunks(di)
    bl3 = b3[:, SUB - 1:SUB, :]
    qt = (qd * jnp.exp(b)).astype(BF16)
    kt = (kd3 * jnp.exp(bl3 - b3)).reshape(tm, BRANCH_W).astype(BF16)
    el = jnp.exp(bl3)
    di_bf = di.astype(BF16)
    increments = [_dot_tn(di_bf[c * SUB:(c + 1) * SUB], kt[c * SUB:(c + 1) * SUB]) * hmask_f32
                  for c in range(n_c)]

    ga = jax.nn.gelu(za)
    u = ga[:, :BRANCH_W]
    vn = _layer_norm(ga[:, BRANCH_W:], row(P_ALN_G), row(P_ALN_B))
    lane_grp = lax.broadcasted_iota(jnp.int32, (CHUNK, BRANCH_W), 1) // (BRANCH_W // HEADS)
    tril = (lax.broadcasted_iota(jnp.int32, (CHUNK, CHUNK), 1)
            <= lax.broadcasted_iota(jnp.int32, (CHUNK, CHUNK), 0))
    w_causal = [jnp.where(tril, aws_ref[g], 0.0).astype(BF16) for g in range(HEADS)]
    mixed_chunks = []
    for c in range(tm // CHUNK):
        vc = vn[c * CHUNK:(c + 1) * CHUNK, :].astype(BF16)
        mixed = None
        for g in range(HEADS):
            mg = _dot(w_causal[g], vc)
            mixed = mg if mixed is None else jnp.where(lane_grp == g, mg, mixed)
        mixed_chunks.append(mixed + abias_ref[...])
    out_a = u * jnp.concatenate(mixed_chunks, axis=0)

    gb_ref[...] = gate(1)

    qn = _head_rms(zb[:, :BRANCH_W], hmask, row(P_QN_G))
    kn = _head_rms(zb[:, BRANCH_W:2 * BRANCH_W], hmask, row(P_KN_G))
    vv = zb[:, 2 * BRANCH_W:]
    q_ref[...] = (qn * QK_SCALE).astype(BF16)
    k_ref[...] = kn
    v_ref[...] = vv
    vb_ref[...] = vv.astype(BF16)
    knt = kn.T.astype(BF16)
    for c in range(tm // ATT_BLK):
        kt_ref[c] = knt[:, c * ATT_BLK:(c + 1) * ATT_BLK]

    gate_a = gate(0)

    cin = zc[:, :BRANCH_W] * _sigmoid(zc[:, BRANCH_W:])
    for r in range(CONV_SHIFTS):
        convbuf[r, CONV_BASE - r:CONV_BASE - r + tm, :] = cin
    acc = jnp.broadcast_to(row(P_CONV_B), (tm, BRANCH_W))
    for j in range(CONV_W):
        lead = CONV_BASE - (CONV_W - 1) + j
        r = lead % CONV_SHIFTS
        acc = acc + convw_ref[j:j + 1, :] * convbuf[r, lead - r:lead - r + tm, :]
    out_c = _silu(_layer_norm(acc, row(P_CLN_G), row(P_CLN_B)))
    convnew_ref[...] = convbuf[0, tm + CONV_BASE - CONV_HALO:tm + CONV_BASE, :]
    for r in range(CONV_SHIFTS):
        convbuf[r, 0:CONV_BASE, :] = convbuf[r, tm:tm + CONV_BASE, :]

    gate_c = gate(2)

    t_idx = lax.broadcasted_iota(jnp.int32, (n_c, SUB, BRANCH_W), 1)
    od = jnp.zeros((tm, BRANCH_W), F32)
    for s in range(SUB):
        e = jnp.exp(jnp.where(t_idx >= s, b3 - b3[:, s:s + 1, :], MASKED_EXPONENT))
        wgt = e * qd3 * kd3[:, s:s + 1, :]
        coef = _dot(wgt.reshape(tm, BRANCH_W).astype(BF16), hmask)
        i_s = jnp.broadcast_to(di3[:, s:s + 1, :], (n_c, SUB, BRANCH_W)).reshape(tm, BRANCH_W)
        od = od + coef * i_s
    gate_d = gate(3)

    st = st_ref[...]
    o_inter = []
    for c in range(n_c):
        o_inter.append(_dot_nt(qt[c * SUB:(c + 1) * SUB], st.astype(BF16)))
        st = st * el[c] + increments[c]
    st_ref[...] = st
    state_ref[...] = st
    od = od + jnp.concatenate(o_inter, axis=0)
    out_d = _head_rms(od, hmask, row(P_DN_G)) * _silu(zd[:, 3 * BRANCH_W:])

    macd = gate_a * _dot(out_a.astype(BF16), wbr_ref[0])
    macd = macd + gate_c * _dot(out_c.astype(BF16), wbr_ref[2])
    macd_ref[...] = macd + gate_d * _dot(out_d.astype(BF16), wbr_ref[3])


def _front_call(x, g1, sc, sh, wts, consts):
    length = x.shape[0]
    tm = ROW_TILE
    n_steps = length // tm
    row_spec = lambda w: pl.BlockSpec((tm, w), lambda i: (i, 0))
    in_arrays = [x, g1, sc, sh, wts['wa'], wts['wb'], wts['wc'], wts['wd'], wts['wg'], wts['wbr'],
                 wts['ptab'], wts['a_ws'], wts['a_bias'], wts['conv_w'],
                 consts['hmask'], consts['tri']]
    in_specs = [row_spec(D_MODEL)] + [_const_spec(a.shape) for a in in_arrays[1:]]
    out_shape = [
        jax.ShapeDtypeStruct((length, BRANCH_W), BF16),
        jax.ShapeDtypeStruct((length, BRANCH_W), F32),
        jax.ShapeDtypeStruct((length, BRANCH_W), F32),
        jax.ShapeDtypeStruct((length // ATT_BLK, BRANCH_W, ATT_BLK), BF16),
        jax.ShapeDtypeStruct((length, BRANCH_W), BF16),
        jax.ShapeDtypeStruct((length, D_MODEL), F32),
        jax.ShapeDtypeStruct((length, D_MODEL), F32),
        jax.ShapeDtypeStruct((CONV_HALO, BRANCH_W), F32),
        jax.ShapeDtypeStruct((BRANCH_W, BRANCH_W), F32),
    ]
    out_specs = [
        row_spec(BRANCH_W), row_spec(BRANCH_W), row_spec(BRANCH_W),
        pl.BlockSpec((tm // ATT_BLK, BRANCH_W, ATT_BLK), lambda i: (i, 0, 0)),
        row_spec(BRANCH_W), row_spec(D_MODEL), row_spec(D_MODEL),
        _const_spec((CONV_HALO, BRANCH_W)), _const_spec((BRANCH_W, BRANCH_W)),
    ]
    scratch = [
        pltpu.VMEM((CONV_SHIFTS, tm + CONV_BASE, BRANCH_W), F32),
        pltpu.VMEM((BRANCH_W, BRANCH_W), F32),
    ]
    return pl.pallas_call(
        _front_kernel,
        out_shape=out_shape,
        grid=(n_steps,),
        in_specs=in_specs,
        out_specs=out_specs,
        scratch_shapes=scratch,
        compiler_params=pltpu.CompilerParams(
            dimension_semantics=("arbitrary",), vmem_limit_bytes=VMEM_LIMIT),
        name="prompt_front",
    )(*in_arrays)


def _sb_scores(z, uo, causal):
    sp = jnp.maximum(z, 0.0) + jnp.log(1.0 + jnp.exp2(jnp.abs(z) * (-LOG2E)))
    if causal is not None:
        sp = jnp.where(causal, sp, 0.0)
    hi = sp.astype(BF16)
    lo = (sp - hi.astype(F32)).astype(BF16)
    cs = _dot(jnp.concatenate([hi, lo], axis=1), uo)
    return z - cs[:, :ATT_BLK], cs[:, ATT_BLK:]


def _sb_weights(d, r, causal):
    w = jnp.exp(d - r)
    if causal is not None:
        w = jnp.where(causal, w, 0.0)
    return w


def _attn_kernel(brows_ref, uo_ref, q_ref, kt_ref, v_ref, o_ref, acc_ref, r_ref, z_s, d_s, t_s):
    i = pl.program_id(0)
    n_sub = ATT_Q // ATT_BLK
    n_old = i * n_sub
    q = q_ref[...]
    unit_cols = jnp.where(lax.broadcasted_iota(jnp.int32, (ATT_Q, HEAD_DIM), 1) < 2, 1.0, 0.0).astype(BF16)
    q_ext = [jnp.concatenate([q[:, HEAD_DIM * h:HEAD_DIM * (h + 1)], unit_cols], axis=1)
             for h in range(HEADS)]
    acc_ref[...] = jnp.zeros_like(acc_ref)
    r_ref[...] = jnp.zeros_like(r_ref)
    lane_head = lax.broadcasted_iota(jnp.int32, (ATT_BLK, BRANCH_W), 1) // HEAD_DIM
    head_lanes = [jnp.where(lane_head == h, 1.0, 0.0).astype(BF16) for h in range(HEADS)]
    causal = (lax.broadcasted_iota(jnp.int32, (ATT_BLK, ATT_BLK), 1)
              < lax.broadcasted_iota(jnp.int32, (ATT_BLK, ATT_BLK), 0))

    def logits(j, h, row0, n_rows):
        kt_h = jnp.concatenate([kt_ref[j, HEAD_DIM * h:HEAD_DIM * (h + 1), :], brows_ref[h]], axis=0)
        return _dot(q_ext[h][row0:row0 + n_rows, :], kt_h)

    def values(j):
        vb = v_ref[pl.ds(pl.multiple_of(j * ATT_BLK, ATT_BLK), ATT_BLK), :]
        return jnp.concatenate([vb * head_lanes[h] for h in range(HEADS)], axis=0)

    def attend(j, row0, n_rows, mask):
        ws = []
        for h in range(HEADS):
            d, tot = _sb_scores(logits(j, h, row0, n_rows), uo_ref[...], mask)
            r = r_ref[h, row0:row0 + n_rows, :]
            ws.append(_sb_weights(d, r, mask).astype(BF16))
            r_ref[h, row0:row0 + n_rows, :] = r + tot
        acc_ref[row0:row0 + n_rows, :] += _dot(jnp.concatenate(ws, axis=1), values(j))

    for kb in reversed(range(n_sub)):
        attend(n_old + kb, kb * ATT_BLK, ATT_BLK, causal)
        if kb + 1 < n_sub:
            attend(n_old + kb, (kb + 1) * ATT_BLK, (n_sub - kb - 1) * ATT_BLK, None)

    def block_of(t):
        return jnp.maximum(n_old - 1 - t, 0)

    def stage1(t, slot):
        for h in range(HEADS):
            z_s[slot, h] = logits(block_of(t), h, 0, ATT_Q)

    def stage2(slot):
        for h in range(HEADS):
            d, tot = _sb_scores(z_s[slot, h], uo_ref[...], None)
            d_s[slot, h] = d
            t_s[slot, h] = tot

    def stage3(t, slot):
        ws = []
        for h in range(HEADS):
            r = r_ref[h]
            ws.append(_sb_weights(d_s[slot, h], r, None).astype(BF16))
            r_ref[h] = r + t_s[slot, h]
        acc_ref[...] += _dot(jnp.concatenate(ws, axis=1), values(block_of(t)))

    stage1(0, 0)
    stage2(0)
    stage1(1, 1)

    def older_pair(u, carry):
        for slot in range(2):
            t = 2 * u + slot
            stage1(t + 2, slot)
            stage3(t, slot)
            stage2(1 - slot)
        return carry

    lax.fori_loop(0, n_old // 2, older_pair, 0)
    o_ref[...] = acc_ref[...]


def _attn_call(brows, uo, q, kt, vb):
    length = q.shape[0]
    return pl.pallas_call(
        _attn_kernel,
        out_shape=jax.ShapeDtypeStruct((length, BRANCH_W), F32),
        grid=(length // ATT_Q,),
        in_specs=[
            _const_spec(brows.shape), _const_spec(uo.shape),
            pl.BlockSpec((ATT_Q, BRANCH_W), lambda i: (i, 0)),
            _const_spec(kt.shape), _const_spec(vb.shape),
        ],
        out_specs=pl.BlockSpec((ATT_Q, BRANCH_W), lambda i: (i, 0)),
        scratch_shapes=[
            pltpu.VMEM((ATT_Q, BRANCH_W), F32),
            pltpu.VMEM((HEADS, ATT_Q, ATT_BLK), F32),
            pltpu.VMEM((2, HEADS, ATT_Q, ATT_BLK), F32),
            pltpu.VMEM((2, HEADS, ATT_Q, ATT_BLK), F32),
            pltpu.VMEM((2, HEADS, ATT_Q, ATT_BLK), F32),
        ],
        compiler_params=pltpu.CompilerParams(
            dimension_semantics=("arbitrary",), vmem_limit_bytes=VMEM_LIMIT),
        name="prompt_attn",
    )(brows, uo, q, kt, vb)


def _back_kernel(x_ref, ob_ref, macd_ref, gb_ref, g1_ref, n2_ref, sc2_ref, sh2_ref, g2_ref,
                 wb1_ref, wo_ref, w1_ref, w2_ref, y_ref):
    merged = macd_ref[...] + gb_ref[...] * _dot(ob_ref[...].astype(BF16), wb1_ref[...])
    x1 = x_ref[...] + g1_ref[...] * _dot(merged.astype(BF16), wo_ref[...])
    h2 = (_rms_rows(x1) * n2_ref[...] * (1.0 + sc2_ref[...]) + sh2_ref[...]).astype(BF16)
    hid = jnp.maximum(_dot(h2, w1_ref[...]), 0.0)
    y_ref[...] = x1 + g2_ref[...] * _dot((hid * hid).astype(BF16), w2_ref[...])


def _back_call(x, ob, macd, gb, g1, n2, sc2, sh2, g2, wts, tm):
    rows = x.shape[0]
    mod_rows = g1.shape[0]
    if mod_rows == 1:
        mod_spec = _const_spec((1, D_MODEL))
    else:
        mod_spec = pl.BlockSpec((tm, D_MODEL), lambda i: (i, 0))
    row_spec = lambda w: pl.BlockSpec((tm, w), lambda i: (i, 0))
    weights = [wts['wb1'], wts['wo'], wts['w1'], wts['w2']]
    return pl.pallas_call(
        _back_kernel,
        out_shape=jax.ShapeDtypeStruct((rows, D_MODEL), F32),
        grid=(rows // tm,),
        in_specs=[row_spec(D_MODEL), row_spec(BRANCH_W), row_spec(D_MODEL), row_spec(D_MODEL),
                  mod_spec, _const_spec((1, D_MODEL)), mod_spec, mod_spec, mod_spec]
                 + [_const_spec(w.shape) for w in weights],
        out_specs=row_spec(D_MODEL),
        compiler_params=pltpu.CompilerParams(
            dimension_semantics=("arbitrary",), vmem_limit_bytes=VMEM_LIMIT),
        name="layer_back",
    )(x, ob, macd, gb, g1, n2, sc2, sh2, g2, *weights)


def _sample_front_kernel(x_ref, g1_ref, sc_ref, sh_ref, wa_ref, wb_ref, wc_ref, wd_ref, wg_ref, wbr_ref,
                         p_ref, a0_ref, convw_ref, hmask_ref, tdiag_ref, conv_ref, s0_ref,
                         q_ref, k_ref, v_ref, macd_ref, gb_ref, convnew_ref, snew_ref, av_ref):
    nb = x_ref.shape[0]
    prm = p_ref[...]
    row = lambda i: prm[i:i + 1, :]
    hmask = hmask_ref[...]

    x = x_ref[...]
    h = (_rms_rows(x) * g1_ref[...] * (1.0 + sc_ref[...]) + sh_ref[...]).astype(BF16)

    ga = jax.nn.gelu(_dot(h, wa_ref[...]))
    vn = _layer_norm(ga[:, BRANCH_W:], row(P_ALN_G), row(P_ALN_B))
    av_ref[...] = vn
    out_a = ga[:, :BRANCH_W] * (a0_ref[0:1, :] * vn + a0_ref[1:2, :])

    zb = _dot(h, wb_ref[...])
    qn = _head_rms(zb[:, :BRANCH_W], hmask, row(P_QN_G))
    kn = _head_rms(zb[:, BRANCH_W:2 * BRANCH_W], hmask, row(P_KN_G))
    q_ref[...] = qn * QK_SCALE
    k_ref[...] = kn
    v_ref[...] = zb[:, 2 * BRANCH_W:]

    zc = _dot(h, wc_ref[...])
    cin = zc[:, :BRANCH_W] * _sigmoid(zc[:, BRANCH_W:])
    acc = row(P_CONV_B) + convw_ref[CONV_W - 1:CONV_W, :] * cin
    for j in range(CONV_W - 1):
        acc = acc + convw_ref[j:j + 1, :] * conv_ref[j]
        if j >= 1:
            convnew_ref[j - 1] = conv_ref[j]
    convnew_ref[CONV_W - 2] = cin
    out_c = _silu(_layer_norm(acc, row(P_CLN_G), row(P_CLN_B)))

    zd = _dot(h, wd_ref[...])
    qd = _silu(zd[:, :BRANCH_W]) * QK_SCALE
    logf = _log_forget(zd[:, BRANCH_W:2 * BRANCH_W], row(P_LB))
    ea = jnp.exp(logf)
    kd = _one_minus_forget(zd[:, BRANCH_W:2 * BRANCH_W], row(P_LB))
    di = zd[:, 2 * BRANCH_W:3 * BRANCH_W]
    tdiag = tdiag_ref[...]

    def spread(r):
        t3 = jnp.broadcast_to(r[:, None, :], (nb, HEAD_DIM, BRANCH_W)) * tdiag[None]
        flat = _dot_split_lhs(t3.reshape(nb * HEAD_DIM, BRANCH_W), hmask, 3)
        return flat.reshape(nb, HEAD_DIM, BRANCH_W)

    s0 = s0_ref[...]
    od = jnp.sum(spread(qd * ea) * s0, axis=1) + _dot_split_lhs(qd * kd, hmask, 3) * di
    snew_ref[...] = spread(ea) * s0 + spread(kd) * di[:, None, :]
    out_d = _head_rms(od, hmask, row(P_DN_G)) * _silu(zd[:, 3 * BRANCH_W:])

    macd, gb = _gated_local_merge(h, wg_ref, out_a, out_c, out_d, wbr_ref)
    macd_ref[...] = macd
    gb_ref[...] = gb


def _sample_front_call(x, g1, sc, sh, wts, consts, conv_t, s0_r):
    nb = x.shape[0]
    in_arrays = [x, g1, sc, sh, wts['wa'], wts['wb'], wts['wc'], wts['wd'], wts['wg'], wts['wbr'],
                 wts['ptab'], wts['a0'], wts['conv_w'], consts['hmask'], consts['tdiag'], conv_t, s0_r]
    out_shape = [
        jax.ShapeDtypeStruct((nb, BRANCH_W), F32),
        jax.ShapeDtypeStruct((nb, BRANCH_W), F32),
        jax.ShapeDtypeStruct((nb, BRANCH_W), F32),
        jax.ShapeDtypeStruct((nb, D_MODEL), F32),
        jax.ShapeDtypeStruct((nb, D_MODEL), F32),
        jax.ShapeDtypeStruct(conv_t.shape, F32),
        jax.ShapeDtypeStruct(s0_r.shape, F32),
        jax.ShapeDtypeStruct((nb, BRANCH_W), F32),
    ]
    return pl.pallas_call(
        _sample_front_kernel,
        out_shape=out_shape,
        grid=(1,),
        in_specs=[_const_spec(a.shape) for a in in_arrays],
        out_specs=[_const_spec(s.shape) for s in out_shape],
        compiler_params=pltpu.CompilerParams(vmem_limit_bytes=VMEM_LIMIT),
        name="sample_front",
    )(*in_arrays)


Q_ROWS = 16


def _paged_attn_kernel(pt_ref, q_ref, bias_ref, uo_ref, *refs):
    k_refs = refs[:PAGES_PER_STEP]
    v_refs = refs[PAGES_PER_STEP:2 * PAGES_PER_STEP]
    o_ref, acc_ref, r_ref = refs[2 * PAGES_PER_STEP:]
    jc = pl.program_id(1)

    @pl.when(jc == 0)
    def _():
        acc_ref[...] = jnp.zeros_like(acc_ref)
        r_ref[...] = jnp.zeros_like(r_ref)

    row_id = lax.broadcasted_iota(jnp.int32, (Q_ROWS, BRANCH_W), 0)
    own_head = row_id == lax.broadcasted_iota(jnp.int32, (Q_ROWS, BRANCH_W), 1) // HEAD_DIM
    qrows = jnp.where(own_head, jnp.broadcast_to(q_ref[...], (Q_ROWS, BRANCH_W)), 0.0).astype(BF16)
    as_matrix = lambda ref: ref[...].reshape(BRANCH_W, CHUNK).astype(BF16)
    pages = range(PAGES_PER_STEP)
    kt_all = jnp.concatenate([as_matrix(k_refs[p]) for p in pages], axis=1)
    z = _dot(qrows, kt_all)
    z_rows = jnp.concatenate(
        [z[:, p * CHUNK:(p + 1) * CHUNK] + bias_ref[...] for p in pages], axis=0)
    d, tot = _sb_scores(z_rows, uo_ref[...], None)
    r = r_ref[...]
    r_pages = [None] * PAGES_PER_STEP
    for p in reversed(pages):
        r_pages[p] = r
        r = r + tot[p * Q_ROWS:(p + 1) * Q_ROWS, :]
    r_ref[...] = r
    w = _sb_weights(d, jnp.concatenate(r_pages, axis=0), None).astype(BF16)
    w_all = jnp.concatenate([w[p * Q_ROWS:(p + 1) * Q_ROWS, :] for p in pages], axis=1)
    vt_all = jnp.concatenate([as_matrix(v_refs[p]) for p in pages], axis=1)
    acc = acc_ref[...] + _dot_nt(w_all, vt_all)
    acc_ref[...] = acc

    @pl.when(jc == pl.num_programs(1) - 1)
    def _():
        o_ref[...] = jnp.sum(jnp.where(own_head, acc, 0.0), axis=0, keepdims=True)


def _paged_attn_call(page_table, q, bias_rows, uo, cache_k, cache_v, layer):
    nb, n_pages = page_table.shape
    n_chunks = n_pages // PAGES_PER_STEP

    def page_spec(p):
        def index_map(b, jc, pt):
            return (pt[b, (n_chunks - 1 - jc) * PAGES_PER_STEP + p], layer, 0, 0, 0)
        return pl.BlockSpec((None, None, HEADS, HEAD_DIM, CHUNK), index_map)

    page_specs = [page_spec(p) for p in range(PAGES_PER_STEP)]
    grid_spec = pltpu.PrefetchScalarGridSpec(
        num_scalar_prefetch=1,
        grid=(nb, n_chunks),
        in_specs=[
            pl.BlockSpec((None, 1, BRANCH_W), lambda b, jc, pt: (b, 0, 0)),
            pl.BlockSpec(bias_rows.shape, lambda b, jc, pt: (0, 0)),
            pl.BlockSpec(uo.shape, lambda b, jc, pt: (0, 0)),
        ] + page_specs + page_specs,
        out_specs=pl.BlockSpec((None, 1, BRANCH_W), lambda b, jc, pt: (b, 0, 0)),
        scratch_shapes=[
            pltpu.VMEM((Q_ROWS, BRANCH_W), F32),
            pltpu.VMEM((Q_ROWS, CHUNK), F32),
        ],
    )
    return pl.pallas_call(
        _paged_attn_kernel,
        out_shape=jax.ShapeDtypeStruct((nb, 1, BRANCH_W), F32),
        grid_spec=grid_spec,
        compiler_params=pltpu.CompilerParams(
            dimension_semantics=("arbitrary", "arbitrary"), vmem_limit_bytes=VMEM_LIMIT),
        name="sample_paged_attn",
    )(page_table, q.reshape(nb, 1, BRANCH_W), bias_rows, uo,
      *([cache_k] * PAGES_PER_STEP), *([cache_v] * PAGES_PER_STEP))


def _constants():
    lane = jnp.arange(BRANCH_W)
    hmask = (lane[:, None] // HEAD_DIM == lane[None, :] // HEAD_DIM).astype(BF16)
    r = jnp.arange(ROW_TILE)
    tri = ((r[:, None] // SUB == r[None, :] // SUB) & (r[None, :] <= r[:, None])).astype(BF16)
    tdiag = (jnp.arange(HEAD_DIM)[:, None] == lane[None, :] % HEAD_DIM).astype(F32)
    j = jnp.arange(2 * ATT_BLK) % ATT_BLK
    c = jnp.arange(2 * ATT_BLK)
    uo = ((c[None, :] >= ATT_BLK) | (j[:, None] >= c[None, :])).astype(BF16)
    return {'hmask': hmask, 'tri': tri, 'tdiag': tdiag, 'uo': uo}


def _layer_weights(l, lb, w_in, a_ln_g, a_ln_b, a_ws, a_bs, b_qn_g, b_kn_g, c_conv_w, c_conv_b,
                   c_ln_g, c_ln_b, d_norm_g, w_branch, w_out, w_ff1, w_ff2):
    wi = w_in[l].astype(BF16)
    c0 = 2 * BRANCH_W
    c1 = c0 + 3 * BRANCH_W
    c2 = c1 + 2 * BRANCH_W
    c3 = c2 + 4 * BRANCH_W
    tile4 = lambda g: jnp.tile(g, HEADS)
    rows = [a_ln_g[l], a_ln_b[l], tile4(b_qn_g[l]), tile4(b_kn_g[l]), c_conv_b[l], c_ln_g[l], c_ln_b[l],
            lb, tile4(d_norm_g[l])]
    ptab = jnp.zeros((16, BRANCH_W), F32).at[:len(rows)].set(jnp.stack(rows))
    grp = BRANCH_W // HEADS
    wbr = w_branch[l].astype(BF16)
    return {
        'wa': wi[:, :c0], 'wb': wi[:, c0:c1], 'wc': wi[:, c1:c2], 'wd': wi[:, c2:c3], 'wg': wi[:, c3:],
        'wbr': wbr, 'wb1': wbr[1], 'ptab': ptab,
        'a_ws': a_ws[l], 'a_bias': jnp.repeat(a_bs[l].T, grp, axis=1),
        'a0': jnp.stack([jnp.repeat(a_ws[l][:, 0, 0], grp), jnp.repeat(a_bs[l][:, 0], grp)]),
        'conv_w': jnp.zeros((32, BRANCH_W), F32).at[:CONV_W].set(c_conv_w[l]),
        'wo': w_out[l].astype(BF16), 'w1': w_ff1[l].astype(BF16), 'w2': w_ff2[l].astype(BF16),
    }


def kernel(x_prompt, x_sample, c_prompt, c_sample, cache_k, cache_v, state_conv, state_hgrn, page_table,
           ada_w, ada_b, norm1_g, norm2_g, w_in, a_ln_g, a_ln_b, a_ws, a_bs, b_qn_g, b_kn_g, b_bias,
           c_conv_w, c_conv_b, c_ln_g, c_ln_b, d_lb_logits, d_norm_g, w_branch, w_out, w_ff1, w_ff2):
    depth = w_in.shape[0]
    seq = x_prompt.shape[1]
    nb = x_sample.shape[0]
    n_pool = cache_k.shape[0]
    consts = _constants()

    cp = jnp.cumsum(jax.nn.softmax(d_lb_logits.astype(F32), axis=0), axis=0)
    lbs = cp - cp[0:1]

    ada_rows = 8 * ((1 + nb + 7) // 8)
    c_all = jnp.zeros((ada_rows, D_MODEL), F32).at[0:1].set(c_prompt).at[1:1 + nb].set(c_sample)
    mods = _ada_call(c_all, ada_w, ada_b)

    ck = cache_k.transpose(0, 1, 3, 4, 2)
    cv = cache_v.transpose(0, 1, 3, 4, 2)

    yp = x_prompt[0]
    ys = x_sample[:, 0]
    outs = {n: [] for n in ('kp', 'vp', 'ks', 'vs', 'cp', 'cs', 'sp', 'ss', 'av')}
    for l in range(depth):
        wts = _layer_weights(l, lbs[l], w_in, a_ln_g, a_ln_b, a_ws, a_bs, b_qn_g, b_kn_g, c_conv_w,
                             c_conv_b, c_ln_g, c_ln_b, d_norm_g, w_branch, w_out, w_ff1, w_ff2)
        mp = [mods[l, 0:1, i * D_MODEL:(i + 1) * D_MODEL] for i in range(6)]
        ms = [mods[l, 1:1 + nb, i * D_MODEL:(i + 1) * D_MODEL] for i in range(6)]
        n1 = norm1_g[l][None]
        n2 = norm2_g[l][None]
        bias_rows = jnp.zeros((Q_ROWS, ATT_BLK), F32).at[:HEADS].set(
            jnp.broadcast_to(b_bias[l][:, None], (HEADS, ATT_BLK)))
        b_hi = b_bias[l].astype(BF16)
        b_lo = (b_bias[l] - b_hi.astype(F32)).astype(BF16)
        brows = jnp.zeros((HEADS, HEAD_DIM, ATT_BLK), BF16)
        brows = brows.at[:, 0, :].set(b_hi[:, None]).at[:, 1, :].set(b_lo[:, None])

        q, k, v, kt, vb, macd, gb, conv_tail, st = _front_call(yp, n1, mp[1], mp[0], wts, consts)
        ob = _attn_call(brows, consts['uo'], q, kt, vb)
        yp = _back_call(yp, ob, macd, gb, mp[2], n2, mp[4], mp[3], mp[5], wts, ROW_TILE)
        outs['kp'].append(k.reshape(1, seq, HEADS, HEAD_DIM))
        outs['vp'].append(v.reshape(1, seq, HEADS, HEAD_DIM))
        outs['cp'].append(conv_tail[CONV_HALO - (CONV_W - 1):][None])
        st4 = st.reshape(HEADS, HEAD_DIM, HEADS, HEAD_DIM)
        outs['sp'].append(jnp.stack([st4[h, :, h, :].T for h in range(HEADS)])[None])

        conv_t = state_conv[:, l].transpose(1, 0, 2)
        s0_r = state_hgrn[:, l].transpose(0, 2, 1, 3).reshape(nb, HEAD_DIM, BRANCH_W)
        qs, ksn, vsn, macd_s, gb_s, conv_new, s_new, avn = _sample_front_call(
            ys, n1, ms[1], ms[0], wts, consts, conv_t, s0_r)
        obs = _paged_attn_call(page_table, qs, bias_rows, consts['uo'], ck, cv, l)
        ys = _back_call(ys, obs[:, 0], macd_s, gb_s, ms[2], n2, ms[4], ms[3], ms[5], wts, nb)
        outs['ks'].append(ksn.reshape(nb, 1, HEADS, HEAD_DIM))
        outs['vs'].append(vsn.reshape(nb, 1, HEADS, HEAD_DIM))
        outs['cs'].append(conv_new.transpose(1, 0, 2))
        outs['ss'].append(s_new.reshape(nb, HEAD_DIM, HEADS, HEAD_DIM).transpose(0, 2, 1, 3))
        outs['av'].append(avn[:, None, :])

    stack = lambda n: jnp.stack(outs[n], axis=1)
    return (yp[None], ys[:, None, :],
            stack('kp'), stack('vp'), stack('ks'), stack('vs'),
            stack('cp'), stack('cs'), stack('sp'), stack('ss'), stack('av'))
```

```python
import functools

import jax
import jax.numpy as jnp
from jax import lax
from jax.experimental import pallas as pl
from jax.experimental.pallas import tpu as pltpu

F32 = jnp.float32
BF16 = jnp.bfloat16

D_MODEL = 1024
BRANCH_W = 256
HEADS = 4
HEAD_DIM = 64
CHUNK = 128
CONV_W = 31
D_FF = 4096
EPS = 1e-6
QK_SCALE = HEAD_DIM ** -0.5
LOG2E = 1.4426950408889634
MASKED_EXPONENT = -1e30

ROW_TILE = 256
BACK_TILE = 512
SUB = 16
CONV_HALO = 32
CONV_SHIFTS = 8
CONV_BASE = CONV_HALO + CONV_SHIFTS
ATT_BLK = 128
ATT_Q = 512
HEAD_PAIR = 2
PAIR_W = HEAD_PAIR * HEAD_DIM
PAIRS_PER_CHECK = 4
DEAD_SUFFIX = 110.0
PAGES_PER_STEP = 16
VMEM_LIMIT = 56 * 1024 * 1024


def _dot(a, b):
    return jnp.dot(a, b, preferred_element_type=F32)


def _bf16_pieces(a, n):
    pieces, rem = [], a
    for _ in range(n):
        p = rem.astype(BF16)
        pieces.append(p)
        rem = rem - p.astype(F32)
    return pieces


def _dot_split_lhs(a, ones_bf, n):
    return _dot(jnp.concatenate(_bf16_pieces(a, n), axis=1), jnp.concatenate([ones_bf] * n, axis=0))


def _dot_split_rhs(ones_bf, b, n):
    return _dot(jnp.concatenate([ones_bf] * n, axis=1), jnp.concatenate(_bf16_pieces(b, n), axis=0))


def _dot_nt(a, b):
    return lax.dot_general(a, b, (((1,), (1,)), ((), ())), preferred_element_type=F32)


def _dot_tn(a, b):
    return lax.dot_general(a, b, (((0,), (0,)), ((), ())), preferred_element_type=F32)


def _sigmoid(x):
    return 0.5 * jnp.tanh(0.5 * x) + 0.5


def _silu(x):
    return x * _sigmoid(x)


def _softplus(x):
    return jnp.maximum(x, 0.0) + jnp.log1p(jnp.exp(-jnp.abs(x)))


def _rms_rows(x):
    return x * lax.rsqrt(jnp.mean(x * x, axis=-1, keepdims=True) + EPS)


def _layer_norm(x, g, b):
    mu = jnp.mean(x, axis=-1, keepdims=True)
    xc = x - mu
    var = jnp.mean(xc * xc, axis=-1, keepdims=True)
    return xc * lax.rsqrt(var + EPS) * g + b


def _head_rms(x, hmask, g):
    ms = _dot_split_lhs(x * x, hmask, 2) * (1.0 / HEAD_DIM)
    return x * lax.rsqrt(ms + EPS) * g


def _log_forget(df, lb):
    ls = -_softplus(-df)
    b = jnp.log1p(-lb) + ls
    pos = lb > 0.0
    a = jnp.log(jnp.where(pos, lb, 1.0))
    lae = jnp.maximum(a, b) + jnp.log1p(jnp.exp(-jnp.abs(a - b)))
    return jnp.where(pos, lae, b)


def _one_minus_forget(df, lb):
    return (1.0 - lb) * _sigmoid(-df)


def _const_spec(shape):
    nd = len(shape)
    return pl.BlockSpec(shape, lambda *_: (0,) * nd, pipeline_mode=pl.Buffered(1))


def _ada_kernel(c_ref, w_ref, b_ref, o_ref):
    s = _silu(c_ref[...]).astype(BF16)
    o_ref[...] = _dot(s, w_ref[...].astype(BF16)) + b_ref[...]


def _ada_call(c_all, ada_w, ada_b):
    depth = ada_w.shape[0]
    rows = c_all.shape[0]
    n_col = ada_w.shape[2] // D_MODEL
    return pl.pallas_call(
        _ada_kernel,
        out_shape=jax.ShapeDtypeStruct((depth, rows, ada_w.shape[2]), F32),
        grid=(depth, n_col),
        in_specs=[
            pl.BlockSpec((rows, D_MODEL), lambda l, j: (0, 0)),
            pl.BlockSpec((None, D_MODEL, D_MODEL), lambda l, j: (l, 0, j)),
            pl.BlockSpec((None, 1, D_MODEL), lambda l, j: (l, 0, j)),
        ],
        out_specs=pl.BlockSpec((None, rows, D_MODEL), lambda l, j: (l, 0, j)),
        compiler_params=pltpu.CompilerParams(vmem_limit_bytes=VMEM_LIMIT),
        name="ada_params",
    )(c_all, ada_w, ada_b.reshape(depth, 1, -1))


P_ALN_G, P_ALN_B, P_QN_G, P_KN_G, P_CONV_B, P_CLN_G, P_CLN_B, P_LB, P_DN_G = range(9)


def _gated_local_merge(h, wg_ref, out_a, out_c, out_d, wbr_ref):
    gate = lambda k: _sigmoid(_dot(h, wg_ref[:, k * D_MODEL:(k + 1) * D_MODEL]))
    macd = gate(0) * _dot(out_a.astype(BF16), wbr_ref[0])
    macd = macd + gate(2) * _dot(out_c.astype(BF16), wbr_ref[2])
    macd = macd + gate(3) * _dot(out_d.astype(BF16), wbr_ref[3])
    return macd, gate(1)


def _front_kernel(x_ref, g1_ref, sc_ref, sh_ref, wa_ref, wb_ref, wc_ref, wd_ref, wg_ref, wbr_ref,
                  p_ref, aws_ref, abias_ref, convw_ref, hmask_ref, tri_ref,
                  q_ref, k_ref, v_ref, kt_ref, vb_ref, macd_ref, gb_ref, convnew_ref, state_ref,
                  convbuf, st_ref):
    tm = x_ref.shape[0]
    step = pl.program_id(0)

    @pl.when(step == 0)
    def _():
        convbuf[...] = jnp.zeros_like(convbuf)
        st_ref[...] = jnp.zeros_like(st_ref)

    prm = p_ref[...]
    row = lambda i: prm[i:i + 1, :]
    hmask = hmask_ref[...]
    hmask_f32 = hmask.astype(F32)

    x = x_ref[...]
    h = (_rms_rows(x) * g1_ref[...] * (1.0 + sc_ref[...]) + sh_ref[...]).astype(BF16)

    za = _dot(h, wa_ref[...])
    zb = _dot(h, wb_ref[...])
    zc = _dot(h, wc_ref[...])
    zd = _dot(h, wd_ref[...])
    gate = lambda k: _sigmoid(_dot(h, wg_ref[:, k * D_MODEL:(k + 1) * D_MODEL]))

    n_c = tm // SUB
    in_chunks = lambda a: a.reshape(n_c, SUB, BRANCH_W)
    qd = _silu(zd[:, :BRANCH_W]) * QK_SCALE
    logf = _log_forget(zd[:, BRANCH_W:2 * BRANCH_W], row(P_LB))
    kd = _one_minus_forget(zd[:, BRANCH_W:2 * BRANCH_W], row(P_LB))
    di = zd[:, 2 * BRANCH_W:3 * BRANCH_W]
    b = _dot_split_rhs(tri_ref[...], logf, 3)
    b3, qd3, kd3, di3 = in_chunks(b), in_chunks(qd), in_chunks(kd), in_chunks(di)
    bl3 = b3[:, SUB - 1:SUB, :]
    qt = (qd * jnp.exp(b)).astype(BF16)
    kt = (kd3 * jnp.exp(bl3 - b3)).reshape(tm, BRANCH_W).astype(BF16)
    el = jnp.exp(bl3)
    di_bf = di.astype(BF16)
    increments = [_dot_tn(di_bf[c * SUB:(c + 1) * SUB], kt[c * SUB:(c + 1) * SUB]) * hmask_f32
                  for c in range(n_c)]

    ga = jax.nn.gelu(za)
    u = ga[:, :BRANCH_W]
    vn = _layer_norm(ga[:, BRANCH_W:], row(P_ALN_G), row(P_ALN_B))
    lane_grp = lax.broadcasted_iota(jnp.int32, (CHUNK, BRANCH_W), 1) // (BRANCH_W // HEADS)
    tril = (lax.broadcasted_iota(jnp.int32, (CHUNK, CHUNK), 1)
            <= lax.broadcasted_iota(jnp.int32, (CHUNK, CHUNK), 0))
    w_causal = [jnp.where(tril, aws_ref[g], 0.0).astype(BF16) for g in range(HEADS)]
    mixed_chunks = []
    for c in range(tm // CHUNK):
        vc = vn[c * CHUNK:(c + 1) * CHUNK, :].astype(BF16)
        mixed = None
        for g in range(HEADS):
            mg = _dot(w_causal[g], vc)
            mixed = mg if mixed is None else jnp.where(lane_grp == g, mg, mixed)
        mixed_chunks.append(mixed + abias_ref[...])
    out_a = u * jnp.concatenate(mixed_chunks, axis=0)

    gb_ref[...] = gate(1)

    qn = _head_rms(zb[:, :BRANCH_W], hmask, row(P_QN_G))
    kn = _head_rms(zb[:, BRANCH_W:2 * BRANCH_W], hmask, row(P_KN_G))
    vv = zb[:, 2 * BRANCH_W:]
    q_ref[...] = (qn * QK_SCALE).astype(BF16)
    k_ref[...] = kn
    v_ref[...] = vv
    vb_ref[...] = vv.astype(BF16)
    knt = kn.T.astype(BF16)
    for c in range(tm // ATT_BLK):
        kt_ref[c] = knt[:, c * ATT_BLK:(c + 1) * ATT_BLK]

    gate_a = gate(0)

    cin = zc[:, :BRANCH_W] * _sigmoid(zc[:, BRANCH_W:])
    for r in range(CONV_SHIFTS):
        convbuf[r, CONV_BASE - r:CONV_BASE - r + tm, :] = cin
    acc = jnp.broadcast_to(row(P_CONV_B), (tm, BRANCH_W))
    for j in range(CONV_W):
        lead = CONV_BASE - (CONV_W - 1) + j
        r = lead % CONV_SHIFTS
        acc = acc + convw_ref[j:j + 1, :] * convbuf[r, lead - r:lead - r + tm, :]
    out_c = _silu(_layer_norm(acc, row(P_CLN_G), row(P_CLN_B)))
    convnew_ref[...] = convbuf[0, tm + CONV_BASE - CONV_HALO:tm + CONV_BASE, :]
    for r in range(CONV_SHIFTS):
        convbuf[r, 0:CONV_BASE, :] = convbuf[r, tm:tm + CONV_BASE, :]

    gate_c = gate(2)

    t_idx = lax.broadcasted_iota(jnp.int32, (n_c, SUB, BRANCH_W), 1)
    od = jnp.zeros((tm, BRANCH_W), F32)
    for s in range(SUB):
        e = jnp.exp(jnp.where(t_idx >= s, b3 - b3[:, s:s + 1, :], MASKED_EXPONENT))
        wgt = e * qd3 * kd3[:, s:s + 1, :]
        coef = _dot(wgt.reshape(tm, BRANCH_W).astype(BF16), hmask)
        i_s = jnp.broadcast_to(di3[:, s:s + 1, :], (n_c, SUB, BRANCH_W)).reshape(tm, BRANCH_W)
        od = od + coef * i_s
    gate_d = gate(3)

    st = st_ref[...]
    o_inter = []
    for c in range(n_c):
        o_inter.append(_dot_nt(qt[c * SUB:(c + 1) * SUB], st.astype(BF16)))
        st = st * el[c] + increments[c]
    st_ref[...] = st
    state_ref[...] = st
    od = od + jnp.concatenate(o_inter, axis=0)
    out_d = _head_rms(od, hmask, row(P_DN_G)) * _silu(zd[:, 3 * BRANCH_W:])

    macd = gate_a * _dot(out_a.astype(BF16), wbr_ref[0])
    macd = macd + gate_c * _dot(out_c.astype(BF16), wbr_ref[2])
    macd_ref[...] = macd + gate_d * _dot(out_d.astype(BF16), wbr_ref[3])


def _front_call(x, g1, sc, sh, wts, consts):
    length = x.shape[0]
    tm = ROW_TILE
    n_steps = length // tm
    row_spec = lambda w: pl.BlockSpec((tm, w), lambda i: (i, 0))
    in_arrays = [x, g1, sc, sh, wts['wa'], wts['wb'], wts['wc'], wts['wd'], wts['wg'], wts['wbr'],
                 wts['ptab'], wts['a_ws'], wts['a_bias'], wts['conv_w'],
                 consts['hmask'], consts['tri']]
    in_specs = [row_spec(D_MODEL)] + [_const_spec(a.shape) for a in in_arrays[1:]]
    out_shape = [
        jax.ShapeDtypeStruct((length, BRANCH_W), BF16),
        jax.ShapeDtypeStruct((length, BRANCH_W), F32),
        jax.ShapeDtypeStruct((length, BRANCH_W), F32),
        jax.ShapeDtypeStruct((length // ATT_BLK, BRANCH_W, ATT_BLK), BF16),
        jax.ShapeDtypeStruct((length, BRANCH_W), BF16),
        jax.ShapeDtypeStruct((length, D_MODEL), F32),
        jax.ShapeDtypeStruct((length, D_MODEL), F32),
        jax.ShapeDtypeStruct((CONV_HALO, BRANCH_W), F32),
        jax.ShapeDtypeStruct((BRANCH_W, BRANCH_W), F32),
    ]
    out_specs = [
        row_spec(BRANCH_W), row_spec(BRANCH_W), row_spec(BRANCH_W),
        pl.BlockSpec((tm // ATT_BLK, BRANCH_W, ATT_BLK), lambda i: (i, 0, 0)),
        row_spec(BRANCH_W), row_spec(D_MODEL), row_spec(D_MODEL),
        _const_spec((CONV_HALO, BRANCH_W)), _const_spec((BRANCH_W, BRANCH_W)),
    ]
    scratch = [
        pltpu.VMEM((CONV_SHIFTS, tm + CONV_BASE, BRANCH_W), F32),
        pltpu.VMEM((BRANCH_W, BRANCH_W), F32),
    ]
    return pl.pallas_call(
        _front_kernel,
        out_shape=out_shape,
        grid=(n_steps,),
        in_specs=in_specs,
        out_specs=out_specs,
        scratch_shapes=scratch,
        compiler_params=pltpu.CompilerParams(
            dimension_semantics=("arbitrary",), vmem_limit_bytes=VMEM_LIMIT),
        name="prompt_front",
    )(*in_arrays)


def _sb_scores(z, uo, causal):
    sp = jnp.maximum(z, 0.0) + jnp.log(1.0 + jnp.exp2(jnp.abs(z) * (-LOG2E)))
    if causal is not None:
        sp = jnp.where(causal, sp, 0.0)
    hi = sp.astype(BF16)
    lo = (sp - hi.astype(F32)).astype(BF16)
    cs = _dot(jnp.concatenate([hi, lo], axis=1), uo)
    return z - cs[:, :ATT_BLK], cs[:, ATT_BLK:]


def _sb_weights(d, r, causal):
    w = jnp.exp(d - r)
    if causal is not None:
        w = jnp.where(causal, w, 0.0)
    return w


def _attn_kernel(brows_ref, uo_ref, q_ref, kt_ref, v_ref, o_ref, acc_ref, r_ref, z_s, d_s, t_s):
    i = pl.program_id(0)
    n_sub = ATT_Q // ATT_BLK
    n_old = i * n_sub
    q = q_ref[...]
    unit_cols = jnp.where(lax.broadcasted_iota(jnp.int32, (ATT_Q, HEAD_DIM), 1) < 2, 1.0, 0.0).astype(BF16)
    q_ext = [jnp.concatenate([q[:, HEAD_DIM * h:HEAD_DIM * (h + 1)], unit_cols], axis=1)
             for h in range(HEADS)]
    acc_ref[...] = jnp.zeros_like(acc_ref)
    r_ref[...] = jnp.zeros_like(r_ref)
    lane_half = lax.broadcasted_iota(jnp.int32, (ATT_BLK, PAIR_W), 1) // HEAD_DIM
    half_lanes = [jnp.where(lane_half == k, 1.0, 0.0).astype(BF16) for k in range(HEAD_PAIR)]
    causal = (lax.broadcasted_iota(jnp.int32, (ATT_BLK, ATT_BLK), 1)
              < lax.broadcasted_iota(jnp.int32, (ATT_BLK, ATT_BLK), 0))

    def logits(j, h, row0, n_rows):
        kt_h = jnp.concatenate([kt_ref[j, HEAD_DIM * h:HEAD_DIM * (h + 1), :], brows_ref[h]], axis=0)
        return _dot(q_ext[h][row0:row0 + n_rows, :], kt_h)

    def values(j, g):
        vb = v_ref[pl.ds(pl.multiple_of(j * ATT_BLK, ATT_BLK), ATT_BLK), PAIR_W * g:PAIR_W * (g + 1)]
        return jnp.concatenate([vb * half_lanes[k] for k in range(HEAD_PAIR)], axis=0)

    def attend(j, row0, n_rows, mask):
        for g in range(HEADS // HEAD_PAIR):
            ws = []
            for h in range(HEAD_PAIR * g, HEAD_PAIR * (g + 1)):
                d, tot = _sb_scores(logits(j, h, row0, n_rows), uo_ref[...], mask)
                r = r_ref[h, row0:row0 + n_rows, :]
                ws.append(_sb_weights(d, r, mask).astype(BF16))
                r_ref[h, row0:row0 + n_rows, :] = r + tot
            acc_ref[g, row0:row0 + n_rows, :] += _dot(jnp.concatenate(ws, axis=1), values(j, g))

    for kb in reversed(range(n_sub)):
        attend(n_old + kb, kb * ATT_BLK, ATT_BLK, causal)
        if kb + 1 < n_sub:
            attend(n_old + kb, (kb + 1) * ATT_BLK, (n_sub - kb - 1) * ATT_BLK, None)

    def block_of(t):
        return jnp.maximum(n_old - 1 - t, 0)

    def older_blocks(g):
        heads = range(HEAD_PAIR * g, HEAD_PAIR * (g + 1))

        def stage1(t, slot):
            for k, h in enumerate(heads):
                z_s[slot, k] = logits(block_of(t), h, 0, ATT_Q)

        def stage2(slot):
            for k in range(HEAD_PAIR):
                d, tot = _sb_scores(z_s[slot, k], uo_ref[...], None)
                d_s[slot, k] = d
                t_s[slot, k] = tot

        def stage3(t, slot):
            ws = []
            for k, h in enumerate(heads):
                r = r_ref[h]
                ws.append(_sb_weights(d_s[slot, k], r, None).astype(BF16))
                r_ref[h] = r + t_s[slot, k]
            acc_ref[g] += _dot(jnp.concatenate(ws, axis=1), values(block_of(t), g))

        stage1(0, 0)
        stage2(0)
        stage1(1, 1)

        def block_pair(u):
            for slot in range(2):
                t = 2 * u + slot
                stage1(t + 2, slot)
                stage3(t, slot)
                stage2(1 - slot)

        n_pairs = n_old // 2

        def more(c):
            return jnp.logical_and(c[0] < n_pairs, c[1])

        def run(c):
            u0 = c[0]
            n_run = jnp.minimum(PAIRS_PER_CHECK, n_pairs - u0)

            def one(k, carry):
                block_pair(u0 + k)
                return carry

            lax.fori_loop(0, n_run, one, 0)
            alive = jnp.min(r_ref[HEAD_PAIR * g:HEAD_PAIR * (g + 1)]) < DEAD_SUFFIX
            return u0 + n_run, alive

        lax.while_loop(more, run, (jnp.int32(0), jnp.bool_(True)))

    for g in range(HEADS // HEAD_PAIR):
        older_blocks(g)
        o_ref[:, PAIR_W * g:PAIR_W * (g + 1)] = acc_ref[g]


def _attn_call(brows, uo, q, kt, vb):
    length = q.shape[0]
    return pl.pallas_call(
        _attn_kernel,
        out_shape=jax.ShapeDtypeStruct((length, BRANCH_W), F32),
        grid=(length // ATT_Q,),
        in_specs=[
            _const_spec(brows.shape), _const_spec(uo.shape),
            pl.BlockSpec((ATT_Q, BRANCH_W), lambda i: (i, 0)),
            _const_spec(kt.shape), _const_spec(vb.shape),
        ],
        out_specs=pl.BlockSpec((ATT_Q, BRANCH_W), lambda i: (i, 0)),
        scratch_shapes=[
            pltpu.VMEM((HEADS // HEAD_PAIR, ATT_Q, PAIR_W), F32),
            pltpu.VMEM((HEADS, ATT_Q, ATT_BLK), F32),
            pltpu.VMEM((2, HEAD_PAIR, ATT_Q, ATT_BLK), F32),
            pltpu.VMEM((2, HEAD_PAIR, ATT_Q, ATT_BLK), F32),
            pltpu.VMEM((2, HEAD_PAIR, ATT_Q, ATT_BLK), F32),
        ],
        compiler_params=pltpu.CompilerParams(
            dimension_semantics=("arbitrary",), vmem_limit_bytes=VMEM_LIMIT),
        name="prompt_attn",
    )(brows, uo, q, kt, vb)


def _back_kernel(x_ref, ob_ref, macd_ref, gb_ref, g1_ref, n2_ref, sc2_ref, sh2_ref, g2_ref,
                 wb1_ref, wo_ref, w1_ref, w2_ref, y_ref):
    merged = macd_ref[...] + gb_ref[...] * _dot(ob_ref[...].astype(BF16), wb1_ref[...])
    x1 = x_ref[...] + g1_ref[...] * _dot(merged.astype(BF16), wo_ref[...])
    h2 = (_rms_rows(x1) * n2_ref[...] * (1.0 + sc2_ref[...]) + sh2_ref[...]).astype(BF16)
    hid = jnp.maximum(_dot(h2, w1_ref[...]), 0.0)
    y_ref[...] = x1 + g2_ref[...] * _dot((hid * hid).astype(BF16), w2_ref[...])


def _back_call(x, ob, macd, gb, g1, n2, sc2, sh2, g2, wts, tm):
    rows = x.shape[0]
    mod_rows = g1.shape[0]
    if mod_rows == 1:
        mod_spec = _const_spec((1, D_MODEL))
    else:
        mod_spec = pl.BlockSpec((tm, D_MODEL), lambda i: (i, 0))
    row_spec = lambda w: pl.BlockSpec((tm, w), lambda i: (i, 0))
    weights = [wts['wb1'], wts['wo'], wts['w1'], wts['w2']]
    return pl.pallas_call(
        _back_kernel,
        out_shape=jax.ShapeDtypeStruct((rows, D_MODEL), F32),
        grid=(rows // tm,),
        in_specs=[row_spec(D_MODEL), row_spec(BRANCH_W), row_spec(D_MODEL), row_spec(D_MODEL),
                  mod_spec, _const_spec((1, D_MODEL)), mod_spec, mod_spec, mod_spec]
                 + [_const_spec(w.shape) for w in weights],
        out_specs=row_spec(D_MODEL),
        compiler_params=pltpu.CompilerParams(
            dimension_semantics=("arbitrary",), vmem_limit_bytes=VMEM_LIMIT),
        name="layer_back",
    )(x, ob, macd, gb, g1, n2, sc2, sh2, g2, *weights)


def _sample_front_kernel(x_ref, g1_ref, sc_ref, sh_ref, wa_ref, wb_ref, wc_ref, wd_ref, wg_ref, wbr_ref,
                         p_ref, a0_ref, convw_ref, hmask_ref, tdiag_ref, conv_ref, s0_ref,
                         q_ref, k_ref, v_ref, macd_ref, gb_ref, convnew_ref, snew_ref, av_ref):
    nb = x_ref.shape[0]
    prm = p_ref[...]
    row = lambda i: prm[i:i + 1, :]
    hmask = hmask_ref[...]

    x = x_ref[...]
    h = (_rms_rows(x) * g1_ref[...] * (1.0 + sc_ref[...]) + sh_ref[...]).astype(BF16)

    ga = jax.nn.gelu(_dot(h, wa_ref[...]))
    vn = _layer_norm(ga[:, BRANCH_W:], row(P_ALN_G), row(P_ALN_B))
    av_ref[...] = vn
    out_a = ga[:, :BRANCH_W] * (a0_ref[0:1, :] * vn + a0_ref[1:2, :])

    zb = _dot(h, wb_ref[...])
    qn = _head_rms(zb[:, :BRANCH_W], hmask, row(P_QN_G))
    kn = _head_rms(zb[:, BRANCH_W:2 * BRANCH_W], hmask, row(P_KN_G))
    q_ref[...] = qn * QK_SCALE
    k_ref[...] = kn
    v_ref[...] = zb[:, 2 * BRANCH_W:]

    zc = _dot(h, wc_ref[...])
    cin = zc[:, :BRANCH_W] * _sigmoid(zc[:, BRANCH_W:])
    acc = row(P_CONV_B) + convw_ref[CONV_W - 1:CONV_W, :] * cin
    for j in range(CONV_W - 1):
        acc = acc + convw_ref[j:j + 1, :] * conv_ref[j]
        if j >= 1:
            convnew_ref[j - 1] = conv_ref[j]
    convnew_ref[CONV_W - 2] = cin
    out_c = _silu(_layer_norm(acc, row(P_CLN_G), row(P_CLN_B)))

    zd = _dot(h, wd_ref[...])
    qd = _silu(zd[:, :BRANCH_W]) * QK_SCALE
    logf = _log_forget(zd[:, BRANCH_W:2 * BRANCH_W], row(P_LB))
    ea = jnp.exp(logf)
    kd = _one_minus_forget(zd[:, BRANCH_W:2 * BRANCH_W], row(P_LB))
    di = zd[:, 2 * BRANCH_W:3 * BRANCH_W]
    tdiag = tdiag_ref[...]

    def spread(r):
        t3 = jnp.broadcast_to(r[:, None, :], (nb, HEAD_DIM, BRANCH_W)) * tdiag[None]
        flat = _dot_split_lhs(t3.reshape(nb * HEAD_DIM, BRANCH_W), hmask, 3)
        return flat.reshape(nb, HEAD_DIM, BRANCH_W)

    s0 = s0_ref[...]
    od = jnp.sum(spread(qd * ea) * s0, axis=1) + _dot_split_lhs(qd * kd, hmask, 3) * di
    snew_ref[...] = spread(ea) * s0 + spread(kd) * di[:, None, :]
    out_d = _head_rms(od, hmask, row(P_DN_G)) * _silu(zd[:, 3 * BRANCH_W:])

    macd, gb = _gated_local_merge(h, wg_ref, out_a, out_c, out_d, wbr_ref)
    macd_ref[...] = macd
    gb_ref[...] = gb


def _sample_front_call(x, g1, sc, sh, wts, consts, conv_t, s0_r):
    nb = x.shape[0]
    in_arrays = [x, g1, sc, sh, wts['wa'], wts['wb'], wts['wc'], wts['wd'], wts['wg'], wts['wbr'],
                 wts['ptab'], wts['a0'], wts['conv_w'], consts['hmask'], consts['tdiag'], conv_t, s0_r]
    out_shape = [
        jax.ShapeDtypeStruct((nb, BRANCH_W), F32),
        jax.ShapeDtypeStruct((nb, BRANCH_W), F32),
        jax.ShapeDtypeStruct((nb, BRANCH_W), F32),
        jax.ShapeDtypeStruct((nb, D_MODEL), F32),
        jax.ShapeDtypeStruct((nb, D_MODEL), F32),
        jax.ShapeDtypeStruct(conv_t.shape, F32),
        jax.ShapeDtypeStruct(s0_r.shape, F32),
        jax.ShapeDtypeStruct((nb, BRANCH_W), F32),
    ]
    return pl.pallas_call(
        _sample_front_kernel,
        out_shape=out_shape,
        grid=(1,),
        in_specs=[_const_spec(a.shape) for a in in_arrays],
        out_specs=[_const_spec(s.shape) for s in out_shape],
        compiler_params=pltpu.CompilerParams(vmem_limit_bytes=VMEM_LIMIT),
        name="sample_front",
    )(*in_arrays)


Q_ROWS = 16


def _paged_attn_kernel(pt_ref, q_ref, bias_ref, uo_ref, *refs):
    k_refs = refs[:PAGES_PER_STEP]
    v_refs = refs[PAGES_PER_STEP:2 * PAGES_PER_STEP]
    o_ref, acc_ref, r_ref = refs[2 * PAGES_PER_STEP:]
    jc = pl.program_id(1)

    @pl.when(jc == 0)
    def _():
        acc_ref[...] = jnp.zeros_like(acc_ref)
        r_ref[...] = jnp.zeros_like(r_ref)

    row_id = lax.broadcasted_iota(jnp.int32, (Q_ROWS, BRANCH_W), 0)
    own_head = row_id == lax.broadcasted_iota(jnp.int32, (Q_ROWS, BRANCH_W), 1) // HEAD_DIM
    qrows = jnp.where(own_head, jnp.broadcast_to(q_ref[...], (Q_ROWS, BRANCH_W)), 0.0).astype(BF16)
    as_matrix = lambda ref: ref[...].reshape(BRANCH_W, CHUNK).astype(BF16)
    pages = range(PAGES_PER_STEP)
    kt_all = jnp.concatenate([as_matrix(k_refs[p]) for p in pages], axis=1)
    z = _dot(qrows, kt_all)
    z_rows = jnp.concatenate(
        [z[:, p * CHUNK:(p + 1) * CHUNK] + bias_ref[...] for p in pages], axis=0)
    d, tot = _sb_scores(z_rows, uo_ref[...], None)
    r = r_ref[...]
    r_pages = [None] * PAGES_PER_STEP
    for p in reversed(pages):
        r_pages[p] = r
        r = r + tot[p * Q_ROWS:(p + 1) * Q_ROWS, :]
    r_ref[...] = r
    w = _sb_weights(d, jnp.concatenate(r_pages, axis=0), None).astype(BF16)
    w_all = jnp.concatenate([w[p * Q_ROWS:(p + 1) * Q_ROWS, :] for p in pages], axis=1)
    vt_all = jnp.concatenate([as_matrix(v_refs[p]) for p in pages], axis=1)
    acc = acc_ref[...] + _dot_nt(w_all, vt_all)
    acc_ref[...] = acc

    @pl.when(jc == pl.num_programs(1) - 1)
    def _():
        o_ref[...] = jnp.sum(jnp.where(own_head, acc, 0.0), axis=0, keepdims=True)


def _paged_attn_call(page_table, q, bias_rows, uo, cache_k, cache_v, layer):
    nb, n_pages = page_table.shape
    n_chunks = n_pages // PAGES_PER_STEP

    def page_spec(p):
        def index_map(b, jc, pt):
            return (pt[b, (n_chunks - 1 - jc) * PAGES_PER_STEP + p], layer, 0, 0, 0)
        return pl.BlockSpec((None, None, HEADS, HEAD_DIM, CHUNK), index_map)

    page_specs = [page_spec(p) for p in range(PAGES_PER_STEP)]
    grid_spec = pltpu.PrefetchScalarGridSpec(
        num_scalar_prefetch=1,
        grid=(nb, n_chunks),
        in_specs=[
            pl.BlockSpec((None, 1, BRANCH_W), lambda b, jc, pt: (b, 0, 0)),
            pl.BlockSpec(bias_rows.shape, lambda b, jc, pt: (0, 0)),
            pl.BlockSpec(uo.shape, lambda b, jc, pt: (0, 0)),
        ] + page_specs + page_specs,
        out_specs=pl.BlockSpec((None, 1, BRANCH_W), lambda b, jc, pt: (b, 0, 0)),
        scratch_shapes=[
            pltpu.VMEM((Q_ROWS, BRANCH_W), F32),
            pltpu.VMEM((Q_ROWS, CHUNK), F32),
        ],
    )
    return pl.pallas_call(
        _paged_attn_kernel,
        out_shape=jax.ShapeDtypeStruct((nb, 1, BRANCH_W), F32),
        grid_spec=grid_spec,
        compiler_params=pltpu.CompilerParams(
            dimension_semantics=("arbitrary", "arbitrary"), vmem_limit_bytes=VMEM_LIMIT),
        name="sample_paged_attn",
    )(page_table, q.reshape(nb, 1, BRANCH_W), bias_rows, uo,
      *([cache_k] * PAGES_PER_STEP), *([cache_v] * PAGES_PER_STEP))


def _constants():
    lane = jnp.arange(BRANCH_W)
    hmask = (lane[:, None] // HEAD_DIM == lane[None, :] // HEAD_DIM).astype(BF16)
    r = jnp.arange(ROW_TILE)
    tri = ((r[:, None] // SUB == r[None, :] // SUB) & (r[None, :] <= r[:, None])).astype(BF16)
    tdiag = (jnp.arange(HEAD_DIM)[:, None] == lane[None, :] % HEAD_DIM).astype(F32)
    j = jnp.arange(2 * ATT_BLK) % ATT_BLK
    c = jnp.arange(2 * ATT_BLK)
    uo = ((c[None, :] >= ATT_BLK) | (j[:, None] >= c[None, :])).astype(BF16)
    return {'hmask': hmask, 'tri': tri, 'tdiag': tdiag, 'uo': uo}


def _layer_weights(l, lb, w_in, a_ln_g, a_ln_b, a_ws, a_bs, b_qn_g, b_kn_g, c_conv_w, c_conv_b,
                   c_ln_g, c_ln_b, d_norm_g, w_branch, w_out, w_ff1, w_ff2):
    wi = w_in[l].astype(BF16)
    c0 = 2 * BRANCH_W
    c1 = c0 + 3 * BRANCH_W
    c2 = c1 + 2 * BRANCH_W
    c3 = c2 + 4 * BRANCH_W
    tile4 = lambda g: jnp.tile(g, HEADS)
    rows = [a_ln_g[l], a_ln_b[l], tile4(b_qn_g[l]), tile4(b_kn_g[l]), c_conv_b[l], c_ln_g[l], c_ln_b[l],
            lb, tile4(d_norm_g[l])]
    ptab = jnp.zeros((16, BRANCH_W), F32).at[:len(rows)].set(jnp.stack(rows))
    grp = BRANCH_W // HEADS
    wbr = w_branch[l].astype(BF16)
    return {
        'wa': wi[:, :c0], 'wb': wi[:, c0:c1], 'wc': wi[:, c1:c2], 'wd': wi[:, c2:c3], 'wg': wi[:, c3:],
        'wbr': wbr, 'wb1': wbr[1], 'ptab': ptab,
        'a_ws': a_ws[l], 'a_bias': jnp.repeat(a_bs[l].T, grp, axis=1),
        'a0': jnp.stack([jnp.repeat(a_ws[l][:, 0, 0], grp), jnp.repeat(a_bs[l][:, 0], grp)]),
        'conv_w': jnp.zeros((32, BRANCH_W), F32).at[:CONV_W].set(c_conv_w[l]),
        'wo': w_out[l].astype(BF16), 'w1': w_ff1[l].astype(BF16), 'w2': w_ff2[l].astype(BF16),
    }


def kernel(x_prompt, x_sample, c_prompt, c_sample, cache_k, cache_v, state_conv, state_hgrn, page_table,
           ada_w, ada_b, norm1_g, norm2_g, w_in, a_ln_g, a_ln_b, a_ws, a_bs, b_qn_g, b_kn_g, b_bias,
           c_conv_w, c_conv_b, c_ln_g, c_ln_b, d_lb_logits, d_norm_g, w_branch, w_out, w_ff1, w_ff2):
    depth = w_in.shape[0]
    seq = x_prompt.shape[1]
    nb = x_sample.shape[0]
    n_pool = cache_k.shape[0]
    consts = _constants()

    cp = jnp.cumsum(jax.nn.softmax(d_lb_logits.astype(F32), axis=0), axis=0)
    lbs = cp - cp[0:1]

    ada_rows = 8 * ((1 + nb + 7) // 8)
    c_all = jnp.zeros((ada_rows, D_MODEL), F32).at[0:1].set(c_prompt).at[1:1 + nb].set(c_sample)
    mods = _ada_call(c_all, ada_w, ada_b)

    ck = cache_k.transpose(0, 1, 3, 4, 2)
    cv = cache_v.transpose(0, 1, 3, 4, 2)

    yp = x_prompt[0]
    ys = x_sample[:, 0]
    outs = {n: [] for n in ('kp', 'vp', 'ks', 'vs', 'cp', 'cs', 'sp', 'ss', 'av')}
    for l in range(depth):
        wts = _layer_weights(l, lbs[l], w_in, a_ln_g, a_ln_b, a_ws, a_bs, b_qn_g, b_kn_g, c_conv_w,
                             c_conv_b, c_ln_g, c_ln_b, d_norm_g, w_branch, w_out, w_ff1, w_ff2)
        mp = [mods[l, 0:1, i * D_MODEL:(i + 1) * D_MODEL] for i in range(6)]
        ms = [mods[l, 1:1 + nb, i * D_MODEL:(i + 1) * D_MODEL] for i in range(6)]
        n1 = norm1_g[l][None]
        n2 = norm2_g[l][None]
        bias_rows = jnp.zeros((Q_ROWS, ATT_BLK), F32).at[:HEADS].set(
            jnp.broadcast_to(b_bias[l][:, None], (HEADS, ATT_BLK)))
        b_hi = b_bias[l].astype(BF16)
        b_lo = (b_bias[l] - b_hi.astype(F32)).astype(BF16)
        brows = jnp.zeros((HEADS, HEAD_DIM, ATT_BLK), BF16)
        brows = brows.at[:, 0, :].set(b_hi[:, None]).at[:, 1, :].set(b_lo[:, None])

        q, k, v, kt, vb, macd, gb, conv_tail, st = _front_call(yp, n1, mp[1], mp[0], wts, consts)
        ob = _attn_call(brows, consts['uo'], q, kt, vb)
        yp = _back_call(yp, ob, macd, gb, mp[2], n2, mp[4], mp[3], mp[5], wts, BACK_TILE)
        outs['kp'].append(k.reshape(1, seq, HEADS, HEAD_DIM))
        outs['vp'].append(v.reshape(1, seq, HEADS, HEAD_DIM))
        outs['cp'].append(conv_tail[CONV_HALO - (CONV_W - 1):][None])
        st4 = st.reshape(HEADS, HEAD_DIM, HEADS, HEAD_DIM)
        outs['sp'].append(jnp.stack([st4[h, :, h, :].T for h in range(HEADS)])[None])

        conv_t = state_conv[:, l].transpose(1, 0, 2)
        s0_r = state_hgrn[:, l].transpose(0, 2, 1, 3).reshape(nb, HEAD_DIM, BRANCH_W)
        qs, ksn, vsn, macd_s, gb_s, conv_new, s_new, avn = _sample_front_call(
            ys, n1, ms[1], ms[0], wts, consts, conv_t, s0_r)
        obs = _paged_attn_call(page_table, qs, bias_rows, consts['uo'], ck, cv, l)
        ys = _back_call(ys, obs[:, 0], macd_s, gb_s, ms[2], n2, ms[4], ms[3], ms[5], wts, nb)
        outs['ks'].append(ksn.reshape(nb, 1, HEADS, HEAD_DIM))
        outs['vs'].append(vsn.reshape(nb, 1, HEADS, HEAD_DIM))
        outs['cs'].append(conv_new.transpose(1, 0, 2))
        outs['ss'].append(s_new.reshape(nb, HEAD_DIM, HEADS, HEAD_DIM).transpose(0, 2, 1, 3))
        outs['av'].append(avn[:, None, :])

    stack = lambda n: jnp.stack(outs[n], axis=1)
    return (yp[None], ys[:, None, :],
            stack('kp'), stack('vp'), stack('ks'), stack('vs'),
            stack('cp'), stack('cs'), stack('sp'), stack('ss'), stack('av'))
```

```python
import functools

import jax
import jax.numpy as jnp
from jax import lax
from jax.experimental import pallas as pl
from jax.experimental.pallas import tpu as pltpu

F32 = jnp.float32
BF16 = jnp.bfloat16

D_MODEL = 1024
BRANCH_W = 256
HEADS = 4
HEAD_DIM = 64
CHUNK = 128
CONV_W = 31
D_FF = 4096
EPS = 1e-6
QK_SCALE = HEAD_DIM ** -0.5
LOG2E = 1.4426950408889634
MASKED_EXPONENT = -1e30

ROW_TILE = 256
BACK_TILE = 512
SUB = 16
CONV_HALO = 32
CONV_SHIFTS = 8
CONV_BASE = CONV_HALO + CONV_SHIFTS
ATT_BLK = 128
ATT_Q = 512
HEAD_PAIR = 2
PAIR_W = HEAD_PAIR * HEAD_DIM
PAGES_PER_STEP = 32
VMEM_LIMIT = 56 * 1024 * 1024


def _dot(a, b):
    return jnp.dot(a, b, preferred_element_type=F32)


def _bf16_pieces(a, n):
    pieces, rem = [], a
    for _ in range(n):
        p = rem.astype(BF16)
        pieces.append(p)
        rem = rem - p.astype(F32)
    return pieces


def _dot_split_lhs(a, ones_bf, n):
    return _dot(jnp.concatenate(_bf16_pieces(a, n), axis=1), jnp.concatenate([ones_bf] * n, axis=0))


def _dot_split_rhs(ones_bf, b, n):
    return _dot(jnp.concatenate([ones_bf] * n, axis=1), jnp.concatenate(_bf16_pieces(b, n), axis=0))


def _dot_nt(a, b):
    return lax.dot_general(a, b, (((1,), (1,)), ((), ())), preferred_element_type=F32)


def _dot_tn(a, b):
    return lax.dot_general(a, b, (((0,), (0,)), ((), ())), preferred_element_type=F32)


def _sigmoid(x):
    return 0.5 * jnp.tanh(0.5 * x) + 0.5


def _silu(x):
    return x * _sigmoid(x)


def _softplus(x):
    return jnp.maximum(x, 0.0) + jnp.log1p(jnp.exp(-jnp.abs(x)))


def _rms_rows(x):
    return x * lax.rsqrt(jnp.mean(x * x, axis=-1, keepdims=True) + EPS)


def _layer_norm(x, g, b):
    mu = jnp.mean(x, axis=-1, keepdims=True)
    xc = x - mu
    var = jnp.mean(xc * xc, axis=-1, keepdims=True)
    return xc * lax.rsqrt(var + EPS) * g + b


def _head_rms(x, hmask, g):
    ms = _dot_split_lhs(x * x, hmask, 2) * (1.0 / HEAD_DIM)
    return x * lax.rsqrt(ms + EPS) * g


def _log_forget(df, lb):
    ls = -_softplus(-df)
    b = jnp.log1p(-lb) + ls
    pos = lb > 0.0
    a = jnp.log(jnp.where(pos, lb, 1.0))
    lae = jnp.maximum(a, b) + jnp.log1p(jnp.exp(-jnp.abs(a - b)))
    return jnp.where(pos, lae, b)


def _one_minus_forget(df, lb):
    return (1.0 - lb) * _sigmoid(-df)


def _const_spec(shape):
    nd = len(shape)
    return pl.BlockSpec(shape, lambda *_: (0,) * nd, pipeline_mode=pl.Buffered(1))


def _ada_kernel(c_ref, w_ref, b_ref, o_ref):
    s = _silu(c_ref[...]).astype(BF16)
    o_ref[...] = _dot(s, w_ref[...].astype(BF16)) + b_ref[...]


def _ada_call(c_all, ada_w, ada_b):
    depth = ada_w.shape[0]
    rows = c_all.shape[0]
    n_col = ada_w.shape[2] // D_MODEL
    return pl.pallas_call(
        _ada_kernel,
        out_shape=jax.ShapeDtypeStruct((depth, rows, ada_w.shape[2]), F32),
        grid=(depth, n_col),
        in_specs=[
            pl.BlockSpec((rows, D_MODEL), lambda l, j: (0, 0)),
            pl.BlockSpec((None, D_MODEL, D_MODEL), lambda l, j: (l, 0, j)),
            pl.BlockSpec((None, 1, D_MODEL), lambda l, j: (l, 0, j)),
        ],
        out_specs=pl.BlockSpec((None, rows, D_MODEL), lambda l, j: (l, 0, j)),
        compiler_params=pltpu.CompilerParams(vmem_limit_bytes=VMEM_LIMIT),
        name="ada_params",
    )(c_all, ada_w, ada_b.reshape(depth, 1, -1))


P_ALN_G, P_ALN_B, P_QN_G, P_KN_G, P_CONV_B, P_CLN_G, P_CLN_B, P_LB, P_DN_G = range(9)


def _gated_local_merge(h, wg_ref, out_a, out_c, out_d, wbr_ref):
    gate = lambda k: _sigmoid(_dot(h, wg_ref[:, k * D_MODEL:(k + 1) * D_MODEL]))
    macd = gate(0) * _dot(out_a.astype(BF16), wbr_ref[0])
    macd = macd + gate(2) * _dot(out_c.astype(BF16), wbr_ref[2])
    macd = macd + gate(3) * _dot(out_d.astype(BF16), wbr_ref[3])
    return macd, gate(1)


def _in_proj_views(win_ref):
    bounds, start = [], 0
    for width in (2 * BRANCH_W, 3 * BRANCH_W, 2 * BRANCH_W, 4 * BRANCH_W, HEADS * D_MODEL):
        bounds.append((start, start + width))
        start += width
    return [win_ref.at[:, lo:hi] for lo, hi in bounds]


def _front_kernel(x_ref, g1_ref, sc_ref, sh_ref, win_ref, wbr_ref,
                  p_ref, aws_ref, abias_ref, convw_ref, hmask_ref, tri_ref,
                  q_ref, k_ref, v_ref, kt_ref, vb_ref, macd_ref, gb_ref, convnew_ref, state_ref,
                  convbuf, st_ref):
    tm = x_ref.shape[0]
    step = pl.program_id(0)
    wa_ref, wb_ref, wc_ref, wd_ref, wg_ref = _in_proj_views(win_ref)

    @pl.when(step == 0)
    def _():
        convbuf[...] = jnp.zeros_like(convbuf)
        st_ref[...] = jnp.zeros_like(st_ref)

    prm = p_ref[...]
    row = lambda i: prm[i:i + 1, :]
    hmask = hmask_ref[...]
    hmask_f32 = hmask.astype(F32)

    x = x_ref[...]
    h = (_rms_rows(x) * g1_ref[...] * (1.0 + sc_ref[...]) + sh_ref[...]).astype(BF16)

    za = _dot(h, wa_ref[...])
    zb = _dot(h, wb_ref[...])
    zc = _dot(h, wc_ref[...])
    zd = _dot(h, wd_ref[...])
    gate = lambda k: _sigmoid(_dot(h, wg_ref[:, k * D_MODEL:(k + 1) * D_MODEL]))

    n_c = tm // SUB
    in_chunks = lambda a: a.reshape(n_c, SUB, BRANCH_W)
    qd = _silu(zd[:, :BRANCH_W]) * QK_SCALE
    logf = _log_forget(zd[:, BRANCH_W:2 * BRANCH_W], row(P_LB))
    kd = _one_minus_forget(zd[:, BRANCH_W:2 * BRANCH_W], row(P_LB))
    di = zd[:, 2 * BRANCH_W:3 * BRANCH_W]
    b = _dot_split_rhs(tri_ref[...], logf, 3)
    b3, qd3, kd3, di3 = in_chunks(b), in_chunks(qd), in_chunks(kd), in_chunks(di)
    bl3 = b3[:, SUB - 1:SUB, :]
    qt = (qd * jnp.exp(b)).astype(BF16)
    kt = (kd3 * jnp.exp(bl3 - b3)).reshape(tm, BRANCH_W).astype(BF16)
    el = jnp.exp(bl3)
    di_bf = di.astype(BF16)
    increments = [_dot_tn(di_bf[c * SUB:(c + 1) * SUB], kt[c * SUB:(c + 1) * SUB]) * hmask_f32
                  for c in range(n_c)]

    ga = jax.nn.gelu(za)
    u = ga[:, :BRANCH_W]
    vn = _layer_norm(ga[:, BRANCH_W:], row(P_ALN_G), row(P_ALN_B))
    lane_grp = lax.broadcasted_iota(jnp.int32, (CHUNK, BRANCH_W), 1) // (BRANCH_W // HEADS)
    tril = (lax.broadcasted_iota(jnp.int32, (CHUNK, CHUNK), 1)
            <= lax.broadcasted_iota(jnp.int32, (CHUNK, CHUNK), 0))
    w_causal = [jnp.where(tril, aws_ref[g], 0.0).astype(BF16) for g in range(HEADS)]
    mixed_chunks = []
    for c in range(tm // CHUNK):
        vc = vn[c * CHUNK:(c + 1) * CHUNK, :].astype(BF16)
        mixed = None
        for g in range(HEADS):
            mg = _dot(w_causal[g], vc)
            mixed = mg if mixed is None else jnp.where(lane_grp == g, mg, mixed)
        mixed_chunks.append(mixed + abias_ref[...])
    out_a = u * jnp.concatenate(mixed_chunks, axis=0)

    gb_ref[...] = gate(1)

    qn = _head_rms(zb[:, :BRANCH_W], hmask, row(P_QN_G))
    kn = _head_rms(zb[:, BRANCH_W:2 * BRANCH_W], hmask, row(P_KN_G))
    vv = zb[:, 2 * BRANCH_W:]
    q_ref[...] = (qn * QK_SCALE).astype(BF16)
    k_ref[...] = kn
    v_ref[...] = vv
    vb_ref[...] = vv.astype(BF16)
    knt = kn.T.astype(BF16)
    for c in range(tm // ATT_BLK):
        kt_ref[c] = knt[:, c * ATT_BLK:(c + 1) * ATT_BLK]

    gate_a = gate(0)

    cin = zc[:, :BRANCH_W] * _sigmoid(zc[:, BRANCH_W:])
    for r in range(CONV_SHIFTS):
        convbuf[r, CONV_BASE - r:CONV_BASE - r + tm, :] = cin
    acc = jnp.broadcast_to(row(P_CONV_B), (tm, BRANCH_W))
    for j in range(CONV_W):
        lead = CONV_BASE - (CONV_W - 1) + j
        r = lead % CONV_SHIFTS
        acc = acc + convw_ref[j:j + 1, :] * convbuf[r, lead - r:lead - r + tm, :]
    out_c = _silu(_layer_norm(acc, row(P_CLN_G), row(P_CLN_B)))
    convnew_ref[...] = convbuf[0, tm + CONV_BASE - CONV_HALO:tm + CONV_BASE, :]
    for r in range(CONV_SHIFTS):
        convbuf[r, 0:CONV_BASE, :] = convbuf[r, tm:tm + CONV_BASE, :]

    gate_c = gate(2)

    t_idx = lax.broadcasted_iota(jnp.int32, (n_c, SUB, BRANCH_W), 1)
    od = jnp.zeros((tm, BRANCH_W), F32)
    for s in range(SUB):
        e = jnp.exp(jnp.where(t_idx >= s, b3 - b3[:, s:s + 1, :], MASKED_EXPONENT))
        wgt = e * qd3 * kd3[:, s:s + 1, :]
        coef = _dot(wgt.reshape(tm, BRANCH_W).astype(BF16), hmask)
        i_s = jnp.broadcast_to(di3[:, s:s + 1, :], (n_c, SUB, BRANCH_W)).reshape(tm, BRANCH_W)
        od = od + coef * i_s
    gate_d = gate(3)

    st = st_ref[...]
    o_inter = []
    for c in range(n_c):
        o_inter.append(_dot_nt(qt[c * SUB:(c + 1) * SUB], st.astype(BF16)))
        st = st * el[c] + increments[c]
    st_ref[...] = st
    state_ref[...] = st
    od = od + jnp.concatenate(o_inter, axis=0)
    out_d = _head_rms(od, hmask, row(P_DN_G)) * _silu(zd[:, 3 * BRANCH_W:])

    macd = gate_a * _dot(out_a.astype(BF16), wbr_ref[0])
    macd = macd + gate_c * _dot(out_c.astype(BF16), wbr_ref[2])
    macd_ref[...] = macd + gate_d * _dot(out_d.astype(BF16), wbr_ref[3])


def _front_call(x, g1, sc, sh, wts, consts):
    length = x.shape[0]
    tm = ROW_TILE
    n_steps = length // tm
    row_spec = lambda w: pl.BlockSpec((tm, w), lambda i: (i, 0))
    in_arrays = [x, g1, sc, sh, wts['win'], wts['wbr'],
                 wts['ptab'], wts['a_ws'], wts['a_bias'], wts['conv_w'],
                 consts['hmask'], consts['tri']]
    in_specs = [row_spec(D_MODEL)] + [_const_spec(a.shape) for a in in_arrays[1:]]
    out_shape = [
        jax.ShapeDtypeStruct((length, BRANCH_W), BF16),
        jax.ShapeDtypeStruct((length, BRANCH_W), F32),
        jax.ShapeDtypeStruct((length, BRANCH_W), F32),
        jax.ShapeDtypeStruct((length // ATT_BLK, BRANCH_W, ATT_BLK), BF16),
        jax.ShapeDtypeStruct((length, BRANCH_W), BF16),
        jax.ShapeDtypeStruct((length, D_MODEL), F32),
        jax.ShapeDtypeStruct((length, D_MODEL), F32),
        jax.ShapeDtypeStruct((CONV_HALO, BRANCH_W), F32),
        jax.ShapeDtypeStruct((BRANCH_W, BRANCH_W), F32),
    ]
    out_specs = [
        row_spec(BRANCH_W), row_spec(BRANCH_W), row_spec(BRANCH_W),
        pl.BlockSpec((tm // ATT_BLK, BRANCH_W, ATT_BLK), lambda i: (i, 0, 0)),
        row_spec(BRANCH_W), row_spec(D_MODEL), row_spec(D_MODEL),
        _const_spec((CONV_HALO, BRANCH_W)), _const_spec((BRANCH_W, BRANCH_W)),
    ]
    scratch = [
        pltpu.VMEM((CONV_SHIFTS, tm + CONV_BASE, BRANCH_W), F32),
        pltpu.VMEM((BRANCH_W, BRANCH_W), F32),
    ]
    return pl.pallas_call(
        _front_kernel,
        out_shape=out_shape,
        grid=(n_steps,),
        in_specs=in_specs,
        out_specs=out_specs,
        scratch_shapes=scratch,
        compiler_params=pltpu.CompilerParams(
            dimension_semantics=("arbitrary",), vmem_limit_bytes=VMEM_LIMIT),
        name="prompt_front",
    )(*in_arrays)


def _sb_scores(z, uo, causal):
    sp = jnp.maximum(z, 0.0) + jnp.log(1.0 + jnp.exp2(jnp.abs(z) * (-LOG2E)))
    if causal is not None:
        sp = jnp.where(causal, sp, 0.0)
    hi = sp.astype(BF16)
    lo = (sp - hi.astype(F32)).astype(BF16)
    cs = _dot(jnp.concatenate([hi, lo], axis=1), uo)
    return z - cs[:, :ATT_BLK], cs[:, ATT_BLK:]


def _sb_weights(d, r, causal):
    w = jnp.exp(d - r)
    if causal is not None:
        w = jnp.where(causal, w, 0.0)
    return w


def _attn_kernel(brows_ref, uo_ref, q_ref, kt_ref, v_ref, o_ref, acc_ref, r_ref, z_s, d_s, t_s):
    i = pl.program_id(0)
    n_sub = ATT_Q // ATT_BLK
    n_old = i * n_sub
    q = q_ref[...]
    unit_cols = jnp.where(lax.broadcasted_iota(jnp.int32, (ATT_Q, HEAD_DIM), 1) < 2, 1.0, 0.0).astype(BF16)
    q_ext = [jnp.concatenate([q[:, HEAD_DIM * h:HEAD_DIM * (h + 1)], unit_cols], axis=1)
             for h in range(HEADS)]
    acc_ref[...] = jnp.zeros_like(acc_ref)
    r_ref[...] = jnp.zeros_like(r_ref)
    lane_half = lax.broadcasted_iota(jnp.int32, (ATT_BLK, PAIR_W), 1) // HEAD_DIM
    half_lanes = [jnp.where(lane_half == k, 1.0, 0.0).astype(BF16) for k in range(HEAD_PAIR)]
    causal = (lax.broadcasted_iota(jnp.int32, (ATT_BLK, ATT_BLK), 1)
              < lax.broadcasted_iota(jnp.int32, (ATT_BLK, ATT_BLK), 0))

    def logits(j, h, row0, n_rows):
        kt_h = jnp.concatenate([kt_ref[j, HEAD_DIM * h:HEAD_DIM * (h + 1), :], brows_ref[h]], axis=0)
        return _dot(q_ext[h][row0:row0 + n_rows, :], kt_h)

    def values(j, g):
        vb = v_ref[pl.ds(pl.multiple_of(j * ATT_BLK, ATT_BLK), ATT_BLK), PAIR_W * g:PAIR_W * (g + 1)]
        return jnp.concatenate([vb * half_lanes[k] for k in range(HEAD_PAIR)], axis=0)

    def attend(j, row0, n_rows, mask):
        for g in range(HEADS // HEAD_PAIR):
            ws = []
            for h in range(HEAD_PAIR * g, HEAD_PAIR * (g + 1)):
                d, tot = _sb_scores(logits(j, h, row0, n_rows), uo_ref[...], mask)
                r = r_ref[h, row0:row0 + n_rows, :]
                ws.append(_sb_weights(d, r, mask).astype(BF16))
                r_ref[h, row0:row0 + n_rows, :] = r + tot
            acc_ref[g, row0:row0 + n_rows, :] += _dot(jnp.concatenate(ws, axis=1), values(j, g))

    for kb in reversed(range(n_sub)):
        attend(n_old + kb, kb * ATT_BLK, ATT_BLK, causal)
        if kb + 1 < n_sub:
            attend(n_old + kb, (kb + 1) * ATT_BLK, (n_sub - kb - 1) * ATT_BLK, None)

    def block_of(t):
        return jnp.maximum(n_old - 1 - t, 0)

    def older_blocks(g):
        heads = range(HEAD_PAIR * g, HEAD_PAIR * (g + 1))

        def stage1(t, slot):
            for k, h in enumerate(heads):
                z_s[slot, k] = logits(block_of(t), h, 0, ATT_Q)

        def stage2(slot):
            for k in range(HEAD_PAIR):
                d, tot = _sb_scores(z_s[slot, k], uo_ref[...], None)
                d_s[slot, k] = d
                t_s[slot, k] = tot

        def stage3(t, slot):
            ws = []
            for k, h in enumerate(heads):
                r = r_ref[h]
                ws.append(_sb_weights(d_s[slot, k], r, None).astype(BF16))
                r_ref[h] = r + t_s[slot, k]
            acc_ref[g] += _dot(jnp.concatenate(ws, axis=1), values(block_of(t), g))

        stage1(0, 0)
        stage2(0)
        stage1(1, 1)

        def block_pair(u):
            for slot in range(2):
                t = 2 * u + slot
                stage1(t + 2, slot)
                stage2(1 - slot)
                stage3(t, slot)

        def one(u, carry):
            block_pair(u)
            return carry

        lax.fori_loop(0, n_old // 2, one, 0)

    for g in range(HEADS // HEAD_PAIR):
        older_blocks(g)
        o_ref[:, PAIR_W * g:PAIR_W * (g + 1)] = acc_ref[g]


def _attn_call(brows, uo, q, kt, vb):
    length = q.shape[0]
    return pl.pallas_call(
        _attn_kernel,
        out_shape=jax.ShapeDtypeStruct((length, BRANCH_W), F32),
        grid=(length // ATT_Q,),
        in_specs=[
            _const_spec(brows.shape), _const_spec(uo.shape),
            pl.BlockSpec((ATT_Q, BRANCH_W), lambda i: (i, 0)),
            _const_spec(kt.shape), _const_spec(vb.shape),
        ],
        out_specs=pl.BlockSpec((ATT_Q, BRANCH_W), lambda i: (i, 0)),
        scratch_shapes=[
            pltpu.VMEM((HEADS // HEAD_PAIR, ATT_Q, PAIR_W), F32),
            pltpu.VMEM((HEADS, ATT_Q, ATT_BLK), F32),
            pltpu.VMEM((2, HEAD_PAIR, ATT_Q, ATT_BLK), F32),
            pltpu.VMEM((2, HEAD_PAIR, ATT_Q, ATT_BLK), F32),
            pltpu.VMEM((2, HEAD_PAIR, ATT_Q, ATT_BLK), F32),
        ],
        compiler_params=pltpu.CompilerParams(
            dimension_semantics=("arbitrary",), vmem_limit_bytes=VMEM_LIMIT),
        name="prompt_attn",
    )(brows, uo, q, kt, vb)


def _back_kernel(x_ref, ob_ref, macd_ref, gb_ref, g1_ref, n2_ref, sc2_ref, sh2_ref, g2_ref,
                 wb1_ref, wo_ref, w1_ref, w2_ref, y_ref):
    merged = macd_ref[...] + gb_ref[...] * _dot(ob_ref[...].astype(BF16), wb1_ref[...])
    x1 = x_ref[...] + g1_ref[...] * _dot(merged.astype(BF16), wo_ref[...])
    h2 = (_rms_rows(x1) * n2_ref[...] * (1.0 + sc2_ref[...]) + sh2_ref[...]).astype(BF16)
    hid = jnp.maximum(_dot(h2, w1_ref[...]), 0.0)
    y_ref[...] = x1 + g2_ref[...] * _dot((hid * hid).astype(BF16), w2_ref[...])


def _back_call(x, ob, macd, gb, g1, n2, sc2, sh2, g2, wts, tm):
    rows = x.shape[0]
    mod_rows = g1.shape[0]
    if mod_rows == 1:
        mod_spec = _const_spec((1, D_MODEL))
    else:
        mod_spec = pl.BlockSpec((tm, D_MODEL), lambda i: (i, 0))
    row_spec = lambda w: pl.BlockSpec((tm, w), lambda i: (i, 0))
    weights = [wts['wb1'], wts['wo'], wts['w1'], wts['w2']]
    return pl.pallas_call(
        _back_kernel,
        out_shape=jax.ShapeDtypeStruct((rows, D_MODEL), F32),
        grid=(rows // tm,),
        in_specs=[row_spec(D_MODEL), row_spec(BRANCH_W), row_spec(D_MODEL), row_spec(D_MODEL),
                  mod_spec, _const_spec((1, D_MODEL)), mod_spec, mod_spec, mod_spec]
                 + [_const_spec(w.shape) for w in weights],
        out_specs=row_spec(D_MODEL),
        compiler_params=pltpu.CompilerParams(
            dimension_semantics=("arbitrary",), vmem_limit_bytes=VMEM_LIMIT),
        name="layer_back",
    )(x, ob, macd, gb, g1, n2, sc2, sh2, g2, *weights)


def _sample_front_kernel(x_ref, g1_ref, sc_ref, sh_ref, win_ref, wbr_ref,
                         p_ref, a0_ref, convw_ref, hmask_ref, tdiag_ref, conv_ref, s0_ref,
                         q_ref, k_ref, v_ref, macd_ref, gb_ref, convnew_ref, snew_ref, av_ref):
    nb = x_ref.shape[0]
    wa_ref, wb_ref, wc_ref, wd_ref, wg_ref = _in_proj_views(win_ref)
    prm = p_ref[...]
    row = lambda i: prm[i:i + 1, :]
    hmask = hmask_ref[...]

    x = x_ref[...]
    h = (_rms_rows(x) * g1_ref[...] * (1.0 + sc_ref[...]) + sh_ref[...]).astype(BF16)

    ga = jax.nn.gelu(_dot(h, wa_ref[...]))
    vn = _layer_norm(ga[:, BRANCH_W:], row(P_ALN_G), row(P_ALN_B))
    av_ref[...] = vn
    out_a = ga[:, :BRANCH_W] * (a0_ref[0:1, :] * vn + a0_ref[1:2, :])

    zb = _dot(h, wb_ref[...])
    qn = _head_rms(zb[:, :BRANCH_W], hmask, row(P_QN_G))
    kn = _head_rms(zb[:, BRANCH_W:2 * BRANCH_W], hmask, row(P_KN_G))
    q_ref[...] = qn * QK_SCALE
    k_ref[...] = kn
    v_ref[...] = zb[:, 2 * BRANCH_W:]

    zc = _dot(h, wc_ref[...])
    cin = zc[:, :BRANCH_W] * _sigmoid(zc[:, BRANCH_W:])
    acc = row(P_CONV_B) + convw_ref[CONV_W - 1:CONV_W, :] * cin
    for j in range(CONV_W - 1):
        acc = acc + convw_ref[j:j + 1, :] * conv_ref[j]
        if j >= 1:
            convnew_ref[j - 1] = conv_ref[j]
    convnew_ref[CONV_W - 2] = cin
    out_c = _silu(_layer_norm(acc, row(P_CLN_G), row(P_CLN_B)))

    zd = _dot(h, wd_ref[...])
    qd = _silu(zd[:, :BRANCH_W]) * QK_SCALE
    logf = _log_forget(zd[:, BRANCH_W:2 * BRANCH_W], row(P_LB))
    ea = jnp.exp(logf)
    kd = _one_minus_forget(zd[:, BRANCH_W:2 * BRANCH_W], row(P_LB))
    di = zd[:, 2 * BRANCH_W:3 * BRANCH_W]
    tdiag = tdiag_ref[...]

    def spread(r):
        t3 = jnp.broadcast_to(r[:, None, :], (nb, HEAD_DIM, BRANCH_W)) * tdiag[None]
        flat = _dot_split_lhs(t3.reshape(nb * HEAD_DIM, BRANCH_W), hmask, 3)
        return flat.reshape(nb, HEAD_DIM, BRANCH_W)

    s0 = s0_ref[...]
    od = jnp.sum(spread(qd * ea) * s0, axis=1) + _dot_split_lhs(qd * kd, hmask, 3) * di
    snew_ref[...] = spread(ea) * s0 + spread(kd) * di[:, None, :]
    out_d = _head_rms(od, hmask, row(P_DN_G)) * _silu(zd[:, 3 * BRANCH_W:])

    macd, gb = _gated_local_merge(h, wg_ref, out_a, out_c, out_d, wbr_ref)
    macd_ref[...] = macd
    gb_ref[...] = gb


def _sample_front_call(x, g1, sc, sh, wts, consts, conv_t, s0_r):
    nb = x.shape[0]
    in_arrays = [x, g1, sc, sh, wts['win'], wts['wbr'],
                 wts['ptab'], wts['a0'], wts['conv_w'], consts['hmask'], consts['tdiag'], conv_t, s0_r]
    out_shape = [
        jax.ShapeDtypeStruct((nb, BRANCH_W), F32),
        jax.ShapeDtypeStruct((nb, BRANCH_W), F32),
        jax.ShapeDtypeStruct((nb, BRANCH_W), F32),
        jax.ShapeDtypeStruct((nb, D_MODEL), F32),
        jax.ShapeDtypeStruct((nb, D_MODEL), F32),
        jax.ShapeDtypeStruct(conv_t.shape, F32),
        jax.ShapeDtypeStruct(s0_r.shape, F32),
        jax.ShapeDtypeStruct((nb, BRANCH_W), F32),
    ]
    return pl.pallas_call(
        _sample_front_kernel,
        out_shape=out_shape,
        grid=(1,),
        in_specs=[_const_spec(a.shape) for a in in_arrays],
        out_specs=[_const_spec(s.shape) for s in out_shape],
        compiler_params=pltpu.CompilerParams(vmem_limit_bytes=VMEM_LIMIT),
        name="sample_front",
    )(*in_arrays)


Q_ROWS = 16


def _paged_attn_kernel(pt_ref, q_ref, bias_ref, uo_ref, *refs):
    k_refs = refs[:PAGES_PER_STEP]
    v_refs = refs[PAGES_PER_STEP:2 * PAGES_PER_STEP]
    o_ref, acc_ref, r_ref = refs[2 * PAGES_PER_STEP:]
    jc = pl.program_id(1)

    @pl.when(jc == 0)
    def _():
        acc_ref[...] = jnp.zeros_like(acc_ref)
        r_ref[...] = jnp.zeros_like(r_ref)

    row_id = lax.broadcasted_iota(jnp.int32, (Q_ROWS, BRANCH_W), 0)
    own_head = row_id == lax.broadcasted_iota(jnp.int32, (Q_ROWS, BRANCH_W), 1) // HEAD_DIM
    qrows = jnp.where(own_head, jnp.broadcast_to(q_ref[...], (Q_ROWS, BRANCH_W)), 0.0).astype(BF16)
    as_matrix = lambda ref: ref[...].reshape(BRANCH_W, CHUNK).astype(BF16)
    pages = range(PAGES_PER_STEP)
    kt_all = jnp.concatenate([as_matrix(k_refs[p]) for p in pages], axis=1)
    z = _dot(qrows, kt_all)
    z_rows = jnp.concatenate(
        [z[:, p * CHUNK:(p + 1) * CHUNK] + bias_ref[...] for p in pages], axis=0)
    d, tot = _sb_scores(z_rows, uo_ref[...], None)
    r = r_ref[...]
    r_pages = [None] * PAGES_PER_STEP
    for p in reversed(pages):
        r_pages[p] = r
        r = r + tot[p * Q_ROWS:(p + 1) * Q_ROWS, :]
    r_ref[...] = r
    w = _sb_weights(d, jnp.concatenate(r_pages, axis=0), None).astype(BF16)
    w_all = jnp.concatenate([w[p * Q_ROWS:(p + 1) * Q_ROWS, :] for p in pages], axis=1)
    vt_all = jnp.concatenate([as_matrix(v_refs[p]) for p in pages], axis=1)
    acc = acc_ref[...] + _dot_nt(w_all, vt_all)
    acc_ref[...] = acc

    @pl.when(jc == pl.num_programs(1) - 1)
    def _():
        o_ref[...] = jnp.sum(jnp.where(own_head, acc, 0.0), axis=0, keepdims=True)


def _paged_attn_call(page_table, q, bias_rows, uo, cache_k, cache_v, layer):
    nb, n_pages = page_table.shape
    n_chunks = n_pages // PAGES_PER_STEP

    def page_spec(p):
        def index_map(b, jc, pt):
            return (pt[b, (n_chunks - 1 - jc) * PAGES_PER_STEP + p], layer, 0, 0, 0)
        return pl.BlockSpec((None, None, HEADS, HEAD_DIM, CHUNK), index_map)

    page_specs = [page_spec(p) for p in range(PAGES_PER_STEP)]
    grid_spec = pltpu.PrefetchScalarGridSpec(
        num_scalar_prefetch=1,
        grid=(nb, n_chunks),
        in_specs=[
            pl.BlockSpec((None, 1, BRANCH_W), lambda b, jc, pt: (b, 0, 0)),
            pl.BlockSpec(bias_rows.shape, lambda b, jc, pt: (0, 0)),
            pl.BlockSpec(uo.shape, lambda b, jc, pt: (0, 0)),
        ] + page_specs + page_specs,
        out_specs=pl.BlockSpec((None, 1, BRANCH_W), lambda b, jc, pt: (b, 0, 0)),
        scratch_shapes=[
            pltpu.VMEM((Q_ROWS, BRANCH_W), F32),
            pltpu.VMEM((Q_ROWS, CHUNK), F32),
        ],
    )
    return pl.pallas_call(
        _paged_attn_kernel,
        out_shape=jax.ShapeDtypeStruct((nb, 1, BRANCH_W), F32),
        grid_spec=grid_spec,
        compiler_params=pltpu.CompilerParams(
            dimension_semantics=("arbitrary", "arbitrary"), vmem_limit_bytes=VMEM_LIMIT),
        name="sample_paged_attn",
    )(page_table, q.reshape(nb, 1, BRANCH_W), bias_rows, uo,
      *([cache_k] * PAGES_PER_STEP), *([cache_v] * PAGES_PER_STEP))


def _constants():
    lane = jnp.arange(BRANCH_W)
    hmask = (lane[:, None] // HEAD_DIM == lane[None, :] // HEAD_DIM).astype(BF16)
    r = jnp.arange(ROW_TILE)
    tri = ((r[:, None] // SUB == r[None, :] // SUB) & (r[None, :] <= r[:, None])).astype(BF16)
    tdiag = (jnp.arange(HEAD_DIM)[:, None] == lane[None, :] % HEAD_DIM).astype(F32)
    j = jnp.arange(2 * ATT_BLK) % ATT_BLK
    c = jnp.arange(2 * ATT_BLK)
    uo = ((c[None, :] >= ATT_BLK) | (j[:, None] >= c[None, :])).astype(BF16)
    return {'hmask': hmask, 'tri': tri, 'tdiag': tdiag, 'uo': uo}


def _layer_weights(l, lb, w_in, a_ln_g, a_ln_b, a_ws, a_bs, b_qn_g, b_kn_g, c_conv_w, c_conv_b,
                   c_ln_g, c_ln_b, d_norm_g, w_branch, w_out, w_ff1, w_ff2):
    tile4 = lambda g: jnp.tile(g, HEADS)
    rows = [a_ln_g[l], a_ln_b[l], tile4(b_qn_g[l]), tile4(b_kn_g[l]), c_conv_b[l], c_ln_g[l], c_ln_b[l],
            lb, tile4(d_norm_g[l])]
    ptab = jnp.zeros((16, BRANCH_W), F32).at[:len(rows)].set(jnp.stack(rows))
    grp = BRANCH_W // HEADS
    wbr = w_branch[l].astype(BF16)
    return {
        'win': w_in[l].astype(BF16),
        'wbr': wbr, 'wb1': wbr[1], 'ptab': ptab,
        'a_ws': a_ws[l], 'a_bias': jnp.repeat(a_bs[l].T, grp, axis=1),
        'a0': jnp.stack([jnp.repeat(a_ws[l][:, 0, 0], grp), jnp.repeat(a_bs[l][:, 0], grp)]),
        'conv_w': jnp.zeros((32, BRANCH_W), F32).at[:CONV_W].set(c_conv_w[l]),
        'wo': w_out[l].astype(BF16), 'w1': w_ff1[l].astype(BF16), 'w2': w_ff2[l].astype(BF16),
    }


def kernel(x_prompt, x_sample, c_prompt, c_sample, cache_k, cache_v, state_conv, state_hgrn, page_table,
           ada_w, ada_b, norm1_g, norm2_g, w_in, a_ln_g, a_ln_b, a_ws, a_bs, b_qn_g, b_kn_g, b_bias,
           c_conv_w, c_conv_b, c_ln_g, c_ln_b, d_lb_logits, d_norm_g, w_branch, w_out, w_ff1, w_ff2):
    depth = w_in.shape[0]
    seq = x_prompt.shape[1]
    nb = x_sample.shape[0]
    n_pool = cache_k.shape[0]
    consts = _constants()

    cp = jnp.cumsum(jax.nn.softmax(d_lb_logits.astype(F32), axis=0), axis=0)
    lbs = cp - cp[0:1]

    ada_rows = 8 * ((1 + nb + 7) // 8)
    c_all = jnp.zeros((ada_rows, D_MODEL), F32).at[0:1].set(c_prompt).at[1:1 + nb].set(c_sample)
    mods = _ada_call(c_all, ada_w, ada_b)

    ck = cache_k.transpose(0, 1, 3, 4, 2)
    cv = cache_v.transpose(0, 1, 3, 4, 2)

    yp = x_prompt[0]
    ys = x_sample[:, 0]
    outs = {n: [] for n in ('kp', 'vp', 'ks', 'vs', 'cp', 'cs', 'sp', 'ss', 'av')}
    for l in range(depth):
        wts = _layer_weights(l, lbs[l], w_in, a_ln_g, a_ln_b, a_ws, a_bs, b_qn_g, b_kn_g, c_conv_w,
                             c_conv_b, c_ln_g, c_ln_b, d_norm_g, w_branch, w_out, w_ff1, w_ff2)
        mp = [mods[l, 0:1, i * D_MODEL:(i + 1) * D_MODEL] for i in range(6)]
        ms = [mods[l, 1:1 + nb, i * D_MODEL:(i + 1) * D_MODEL] for i in range(6)]
        n1 = norm1_g[l][None]
        n2 = norm2_g[l][None]
        bias_rows = jnp.zeros((Q_ROWS, ATT_BLK), F32).at[:HEADS].set(
            jnp.broadcast_to(b_bias[l][:, None], (HEADS, ATT_BLK)))
        b_hi = b_bias[l].astype(BF16)
        b_lo = (b_bias[l] - b_hi.astype(F32)).astype(BF16)
        brows = jnp.zeros((HEADS, HEAD_DIM, ATT_BLK), BF16)
        brows = brows.at[:, 0, :].set(b_hi[:, None]).at[:, 1, :].set(b_lo[:, None])

        q, k, v, kt, vb, macd, gb, conv_tail, st = _front_call(yp, n1, mp[1], mp[0], wts, consts)
        ob = _attn_call(brows, consts['uo'], q, kt, vb)
        yp = _back_call(yp, ob, macd, gb, mp[2], n2, mp[4], mp[3], mp[5], wts, BACK_TILE)
        outs['kp'].append(k.reshape(1, seq, HEADS, HEAD_DIM))
        outs['vp'].append(v.reshape(1, seq, HEADS, HEAD_DIM))
        outs['cp'].append(conv_tail[CONV_HALO - (CONV_W - 1):][None])
        st4 = st.reshape(HEADS, HEAD_DIM, HEADS, HEAD_DIM)
        outs['sp'].append(jnp.stack([st4[h, :, h, :].T for h in range(HEADS)])[None])

        conv_t = state_conv[:, l].transpose(1, 0, 2)
        s0_r = state_hgrn[:, l].transpose(0, 2, 1, 3).reshape(nb, HEAD_DIM, BRANCH_W)
        qs, ksn, vsn, macd_s, gb_s, conv_new, s_new, avn = _sample_front_call(
            ys, n1, ms[1], ms[0], wts, consts, conv_t, s0_r)
        obs = _paged_attn_call(page_table, qs, bias_rows, consts['uo'], ck, cv, l)
        ys = _back_call(ys, obs[:, 0], macd_s, gb_s, ms[2], n2, ms[4], ms[3], ms[5], wts, nb)
        outs['ks'].append(ksn.reshape(nb, 1, HEADS, HEAD_DIM))
        outs['vs'].append(vsn.reshape(nb, 1, HEADS, HEAD_DIM))
        outs['cs'].append(conv_new.transpose(1, 0, 2))
        outs['ss'].append(s_new.reshape(nb, HEAD_DIM, HEADS, HEAD_DIM).transpose(0, 2, 1, 3))
        outs['av'].append(avn[:, None, :])

    stack = lambda n: jnp.stack(outs[n], axis=1)
    return (yp[None], ys[:, None, :],
            stack('kp'), stack('vp'), stack('ks'), stack('vs'),
            stack('cp'), stack('cs'), stack('sp'), stack('ss'), stack('av'))
```

```python
import functools

import jax
import jax.numpy as jnp
from jax import lax
from jax.experimental import pallas as pl
from jax.experimental.pallas import tpu as pltpu

F32 = jnp.float32
BF16 = jnp.bfloat16

D_MODEL = 1024
BRANCH_W = 256
HEADS = 4
HEAD_DIM = 64
CHUNK = 128
CONV_W = 31
D_FF = 4096
EPS = 1e-6
QK_SCALE = HEAD_DIM ** -0.5
LOG2E = 1.4426950408889634
MASKED_EXPONENT = -1e30

ROW_TILE = 256
BACK_TILE = 512
SUB = 16
CONV_HALO = 32
CONV_SHIFTS = 8
CONV_BASE = CONV_HALO + CONV_SHIFTS
ATT_BLK = 128
ATT_Q = 512
HEAD_PAIR = 2
PAIR_W = HEAD_PAIR * HEAD_DIM
PAGES_PER_STEP = 32
VMEM_LIMIT = 56 * 1024 * 1024


def _dot(a, b):
    return jnp.dot(a, b, preferred_element_type=F32)


def _bf16_pieces(a, n):
    pieces, rem = [], a
    for _ in range(n):
        p = rem.astype(BF16)
        pieces.append(p)
        rem = rem - p.astype(F32)
    return pieces


def _dot_split_lhs(a, ones_bf, n):
    return _dot(jnp.concatenate(_bf16_pieces(a, n), axis=1), jnp.concatenate([ones_bf] * n, axis=0))


def _dot_split_rhs(ones_bf, b, n):
    return _dot(jnp.concatenate([ones_bf] * n, axis=1), jnp.concatenate(_bf16_pieces(b, n), axis=0))


def _dot_nt(a, b):
    return lax.dot_general(a, b, (((1,), (1,)), ((), ())), preferred_element_type=F32)


def _dot_tn(a, b):
    return lax.dot_general(a, b, (((0,), (0,)), ((), ())), preferred_element_type=F32)


def _sigmoid(x):
    return 0.5 * jnp.tanh(0.5 * x) + 0.5


def _silu(x):
    return x * _sigmoid(x)


def _softplus(x):
    return jnp.maximum(x, 0.0) + jnp.log1p(jnp.exp(-jnp.abs(x)))


def _rms_rows(x):
    return x * lax.rsqrt(jnp.mean(x * x, axis=-1, keepdims=True) + EPS)


def _layer_norm(x, g, b):
    mu = jnp.mean(x, axis=-1, keepdims=True)
    xc = x - mu
    var = jnp.mean(xc * xc, axis=-1, keepdims=True)
    return xc * lax.rsqrt(var + EPS) * g + b


def _head_rms(x, hmask, g):
    ms = _dot_split_lhs(x * x, hmask, 2) * (1.0 / HEAD_DIM)
    return x * lax.rsqrt(ms + EPS) * g


def _log_forget(df, lb):
    ls = -_softplus(-df)
    b = jnp.log1p(-lb) + ls
    pos = lb > 0.0
    a = jnp.log(jnp.where(pos, lb, 1.0))
    lae = jnp.maximum(a, b) + jnp.log1p(jnp.exp(-jnp.abs(a - b)))
    return jnp.where(pos, lae, b)


def _one_minus_forget(df, lb):
    return (1.0 - lb) * _sigmoid(-df)


def _const_spec(shape):
    nd = len(shape)
    return pl.BlockSpec(shape, lambda *_: (0,) * nd, pipeline_mode=pl.Buffered(1))


def _ada_kernel(c_ref, w_ref, b_ref, o_ref):
    s = _silu(c_ref[...]).astype(BF16)
    o_ref[...] = _dot(s, w_ref[...].astype(BF16)) + b_ref[...]


def _ada_call(c_all, ada_w, ada_b):
    depth = ada_w.shape[0]
    rows = c_all.shape[0]
    n_col = ada_w.shape[2] // D_MODEL
    return pl.pallas_call(
        _ada_kernel,
        out_shape=jax.ShapeDtypeStruct((depth, rows, ada_w.shape[2]), F32),
        grid=(depth, n_col),
        in_specs=[
            pl.BlockSpec((rows, D_MODEL), lambda l, j: (0, 0)),
            pl.BlockSpec((None, D_MODEL, D_MODEL), lambda l, j: (l, 0, j)),
            pl.BlockSpec((None, 1, D_MODEL), lambda l, j: (l, 0, j)),
        ],
        out_specs=pl.BlockSpec((None, rows, D_MODEL), lambda l, j: (l, 0, j)),
        compiler_params=pltpu.CompilerParams(vmem_limit_bytes=VMEM_LIMIT),
        name="ada_params",
    )(c_all, ada_w, ada_b.reshape(depth, 1, -1))


P_ALN_G, P_ALN_B, P_QN_G, P_KN_G, P_CONV_B, P_CLN_G, P_CLN_B, P_LB, P_DN_G = range(9)


def _gated_local_merge(h, wg_ref, out_a, out_c, out_d, wbr_ref):
    gate = lambda k: _sigmoid(_dot(h, wg_ref[:, k * D_MODEL:(k + 1) * D_MODEL]))
    macd = gate(0) * _dot(out_a.astype(BF16), wbr_ref[0])
    macd = macd + gate(2) * _dot(out_c.astype(BF16), wbr_ref[2])
    macd = macd + gate(3) * _dot(out_d.astype(BF16), wbr_ref[3])
    return macd, gate(1)


def _in_proj_views(win_ref):
    bounds, start = [], 0
    for width in (2 * BRANCH_W, 3 * BRANCH_W, 2 * BRANCH_W, 4 * BRANCH_W, HEADS * D_MODEL):
        bounds.append((start, start + width))
        start += width
    return [win_ref.at[:, lo:hi] for lo, hi in bounds]


def _front_kernel(n_aliased, *refs):
    (x_ref, g1_ref, sc_ref, sh_ref, win_ref, wbr_ref,
     p_ref, aws_ref, abias_ref, convw_ref, hmask_ref, tri_ref,
     q_ref, ktf_ref, vtf_ref, kt_ref, vb_ref, macd_ref, gb_ref, convnew_ref, state_ref,
     convbuf, st_ref) = refs[n_aliased:]
    tm = x_ref.shape[0]
    step = pl.program_id(0)
    wa_ref, wb_ref, wc_ref, wd_ref, wg_ref = _in_proj_views(win_ref)

    @pl.when(step == 0)
    def _():
        convbuf[...] = jnp.zeros_like(convbuf)
        st_ref[...] = jnp.zeros_like(st_ref)

    prm = p_ref[...]
    row = lambda i: prm[i:i + 1, :]
    hmask = hmask_ref[...]
    hmask_f32 = hmask.astype(F32)

    x = x_ref[...]
    h = (_rms_rows(x) * g1_ref[...] * (1.0 + sc_ref[...]) + sh_ref[...]).astype(BF16)

    za = _dot(h, wa_ref[...])
    zb = _dot(h, wb_ref[...])
    zc = _dot(h, wc_ref[...])
    zd = _dot(h, wd_ref[...])

    n_c = tm // SUB
    in_chunks = lambda a: a.reshape(n_c, SUB, BRANCH_W)
    qd = _silu(zd[:, :BRANCH_W]) * QK_SCALE
    logf = _log_forget(zd[:, BRANCH_W:2 * BRANCH_W], row(P_LB))
    kd = _one_minus_forget(zd[:, BRANCH_W:2 * BRANCH_W], row(P_LB))
    di = zd[:, 2 * BRANCH_W:3 * BRANCH_W]
    b = _dot_split_rhs(tri_ref[...], logf, 3)
    b3, qd3, kd3, di3 = in_chunks(b), in_chunks(qd), in_chunks(kd), in_chunks(di)
    bl3 = b3[:, SUB - 1:SUB, :]
    qt = (qd * jnp.exp(b)).astype(BF16)
    kt = (kd3 * jnp.exp(bl3 - b3)).reshape(tm, BRANCH_W).astype(BF16)
    el = jnp.exp(bl3)
    di_bf = di.astype(BF16)
    increments = [_dot_tn(di_bf[c * SUB:(c + 1) * SUB], kt[c * SUB:(c + 1) * SUB]) * hmask_f32
                  for c in range(n_c)]

    ga = jax.nn.gelu(za)
    u = ga[:, :BRANCH_W]
    vn = _layer_norm(ga[:, BRANCH_W:], row(P_ALN_G), row(P_ALN_B))
    lane_grp = lax.broadcasted_iota(jnp.int32, (CHUNK, BRANCH_W), 1) // (BRANCH_W // HEADS)
    tril = (lax.broadcasted_iota(jnp.int32, (CHUNK, CHUNK), 1)
            <= lax.broadcasted_iota(jnp.int32, (CHUNK, CHUNK), 0))
    w_causal = [jnp.where(tril, aws_ref[g], 0.0).astype(BF16) for g in range(HEADS)]
    mixed_chunks = []
    for c in range(tm // CHUNK):
        vc = vn[c * CHUNK:(c + 1) * CHUNK, :].astype(BF16)
        mixed = None
        for g in range(HEADS):
            mg = _dot(w_causal[g], vc)
            mixed = mg if mixed is None else jnp.where(lane_grp == g, mg, mixed)
        mixed_chunks.append(mixed + abias_ref[...])
    out_a = u * jnp.concatenate(mixed_chunks, axis=0)

    qn = _head_rms(zb[:, :BRANCH_W], hmask, row(P_QN_G))
    kn = _head_rms(zb[:, BRANCH_W:2 * BRANCH_W], hmask, row(P_KN_G))
    vv = zb[:, 2 * BRANCH_W:]
    q_ref[...] = (qn * QK_SCALE).astype(BF16)
    knt_f32 = kn.T
    ktf_ref[...] = knt_f32
    vtf_ref[...] = vv.T
    vb_ref[...] = vv.astype(BF16)
    knt = knt_f32.astype(BF16)
    for c in range(tm // ATT_BLK):
        kt_ref[c] = knt[:, c * ATT_BLK:(c + 1) * ATT_BLK]

    cin = zc[:, :BRANCH_W] * _sigmoid(zc[:, BRANCH_W:])
    for r in range(CONV_SHIFTS):
        convbuf[r, CONV_BASE - r:CONV_BASE - r + tm, :] = cin
    acc = jnp.broadcast_to(row(P_CONV_B), (tm, BRANCH_W))
    for j in range(CONV_W):
        lead = CONV_BASE - (CONV_W - 1) + j
        r = lead % CONV_SHIFTS
        acc = acc + convw_ref[j:j + 1, :] * convbuf[r, lead - r:lead - r + tm, :]
    out_c = _silu(_layer_norm(acc, row(P_CLN_G), row(P_CLN_B)))
    convnew_ref[...] = convbuf[0, tm + CONV_BASE - CONV_HALO:tm + CONV_BASE, :]
    for r in range(CONV_SHIFTS):
        convbuf[r, 0:CONV_BASE, :] = convbuf[r, tm:tm + CONV_BASE, :]

    t_idx = lax.broadcasted_iota(jnp.int32, (n_c, SUB, BRANCH_W), 1)
    od = jnp.zeros((tm, BRANCH_W), F32)
    gate_cols = []
    assert SUB * BRANCH_W == HEADS * D_MODEL
    for s in range(SUB):
        e = jnp.exp(jnp.where(t_idx >= s, b3 - b3[:, s:s + 1, :], MASKED_EXPONENT))
        wgt = e * qd3 * kd3[:, s:s + 1, :]
        coef = _dot(wgt.reshape(tm, BRANCH_W).astype(BF16), hmask)
        i_s = jnp.broadcast_to(di3[:, s:s + 1, :], (n_c, SUB, BRANCH_W)).reshape(tm, BRANCH_W)
        od = od + coef * i_s
        gate_cols.append(_sigmoid(_dot(h, wg_ref[:, s * BRANCH_W:(s + 1) * BRANCH_W])))
    per_gate = D_MODEL // BRANCH_W
    gate = lambda k: jnp.concatenate(gate_cols[k * per_gate:(k + 1) * per_gate], axis=1)
    gate_a, gate_c, gate_d = gate(0), gate(2), gate(3)
    gb_ref[...] = gate(1)

    st = st_ref[...]
    o_inter = []
    for c in range(n_c):
        o_inter.append(_dot_nt(qt[c * SUB:(c + 1) * SUB], st.astype(BF16)))
        st = st * el[c] + increments[c]
    st_ref[...] = st
    state_ref[...] = st
    od = od + jnp.concatenate(o_inter, axis=0)
    out_d = _head_rms(od, hmask, row(P_DN_G)) * _silu(zd[:, 3 * BRANCH_W:])

    macd = gate_a * _dot(out_a.astype(BF16), wbr_ref[0])
    macd = macd + gate_c * _dot(out_c.astype(BF16), wbr_ref[2])
    macd_ref[...] = macd + gate_d * _dot(out_d.astype(BF16), wbr_ref[3])


def _front_call(x, g1, sc, sh, wts, consts, layer, depth, kv_prev):
    length = x.shape[0]
    tm = ROW_TILE
    n_steps = length // tm
    row_spec = lambda w: pl.BlockSpec((tm, w), lambda i: (i, 0))
    in_arrays = [x, g1, sc, sh, wts['win'], wts['wbr'],
                 wts['ptab'], wts['a_ws'], wts['a_bias'], wts['conv_w'],
                 consts['hmask'], consts['tri']]
    in_specs = [row_spec(D_MODEL)] + [_const_spec(a.shape) for a in in_arrays[1:]]
    aliased = [] if kv_prev is None else list(kv_prev)
    in_arrays = aliased + in_arrays
    in_specs = [pl.BlockSpec(memory_space=pl.ANY)] * len(aliased) + in_specs
    kv_spec = pl.BlockSpec((None, BRANCH_W, tm), lambda i: (layer, 0, i))
    out_shape = [
        jax.ShapeDtypeStruct((length, BRANCH_W), BF16),
        jax.ShapeDtypeStruct((depth, BRANCH_W, length), F32),
        jax.ShapeDtypeStruct((depth, BRANCH_W, length), F32),
        jax.ShapeDtypeStruct((length // ATT_BLK, BRANCH_W, ATT_BLK), BF16),
        jax.ShapeDtypeStruct((length, BRANCH_W), BF16),
        jax.ShapeDtypeStruct((length, D_MODEL), F32),
        jax.ShapeDtypeStruct((length, D_MODEL), F32),
        jax.ShapeDtypeStruct((CONV_HALO, BRANCH_W), F32),
        jax.ShapeDtypeStruct((BRANCH_W, BRANCH_W), F32),
    ]
    out_specs = [
        row_spec(BRANCH_W), kv_spec, kv_spec,
        pl.BlockSpec((tm // ATT_BLK, BRANCH_W, ATT_BLK), lambda i: (i, 0, 0)),
        row_spec(BRANCH_W), row_spec(D_MODEL), row_spec(D_MODEL),
        _const_spec((CONV_HALO, BRANCH_W)), _const_spec((BRANCH_W, BRANCH_W)),
    ]
    scratch = [
        pltpu.VMEM((CONV_SHIFTS, tm + CONV_BASE, BRANCH_W), F32),
        pltpu.VMEM((BRANCH_W, BRANCH_W), F32),
    ]
    return pl.pallas_call(
        functools.partial(_front_kernel, len(aliased)),
        out_shape=out_shape,
        grid=(n_steps,),
        in_specs=in_specs,
        out_specs=out_specs,
        scratch_shapes=scratch,
        input_output_aliases={n: 1 + n for n in range(len(aliased))},
        compiler_params=pltpu.CompilerParams(
            dimension_semantics=("arbitrary",), vmem_limit_bytes=VMEM_LIMIT),
        name="prompt_front",
    )(*in_arrays)


def _sb_scores(z, uo, causal):
    sp = jnp.maximum(z, 0.0) + jnp.log(1.0 + jnp.exp2(jnp.abs(z) * (-LOG2E)))
    if causal is not None:
        sp = jnp.where(causal, sp, 0.0)
    hi = sp.astype(BF16)
    lo = (sp - hi.astype(F32)).astype(BF16)
    cs = _dot(jnp.concatenate([hi, lo], axis=1), uo)
    return z - cs[:, :ATT_BLK], cs[:, ATT_BLK:]


def _sb_weights(d, r, causal):
    w = jnp.exp(d - r)
    if causal is not None:
        w = jnp.where(causal, w, 0.0)
    return w


def _attn_kernel(brows_ref, uo_ref, q_ref, kt_ref, v_ref, o_ref, acc_ref, r_ref, z_s, d_s, t_s):
    i = pl.program_id(0)
    n_sub = ATT_Q // ATT_BLK
    n_old = i * n_sub
    q = q_ref[...]
    unit_cols = jnp.where(lax.broadcasted_iota(jnp.int32, (ATT_Q, HEAD_DIM), 1) < 2, 1.0, 0.0).astype(BF16)
    q_ext = [jnp.concatenate([q[:, HEAD_DIM * h:HEAD_DIM * (h + 1)], unit_cols], axis=1)
             for h in range(HEADS)]
    acc_ref[...] = jnp.zeros_like(acc_ref)
    r_ref[...] = jnp.zeros_like(r_ref)
    lane_half = lax.broadcasted_iota(jnp.int32, (ATT_BLK, PAIR_W), 1) // HEAD_DIM
    half_lanes = [jnp.where(lane_half == k, 1.0, 0.0).astype(BF16) for k in range(HEAD_PAIR)]
    causal = (lax.broadcasted_iota(jnp.int32, (ATT_BLK, ATT_BLK), 1)
              < lax.broadcasted_iota(jnp.int32, (ATT_BLK, ATT_BLK), 0))

    brows_pair = [jnp.concatenate([brows_ref[h], brows_ref[h]], axis=1) for h in range(HEADS)]

    def logits(j, h, row0, n_rows):
        kt_h = jnp.concatenate([kt_ref[j, HEAD_DIM * h:HEAD_DIM * (h + 1), :], brows_ref[h]], axis=0)
        return _dot(q_ext[h][row0:row0 + n_rows, :], kt_h)

    def values(j, g):
        vb = v_ref[pl.ds(pl.multiple_of(j * ATT_BLK, ATT_BLK), ATT_BLK), PAIR_W * g:PAIR_W * (g + 1)]
        return jnp.concatenate([vb * half_lanes[k] for k in range(HEAD_PAIR)], axis=0)

    def attend(j, row0, n_rows, mask):
        for g in range(HEADS // HEAD_PAIR):
            ws = []
            for h in range(HEAD_PAIR * g, HEAD_PAIR * (g + 1)):
                d, tot = _sb_scores(logits(j, h, row0, n_rows), uo_ref[...], mask)
                r = r_ref[h, row0:row0 + n_rows, :]
                ws.append(_sb_weights(d, r, mask).astype(BF16))
                r_ref[h, row0:row0 + n_rows, :] = r + tot
            acc_ref[g, row0:row0 + n_rows, :] += _dot(jnp.concatenate(ws, axis=1), values(j, g))

    for kb in reversed(range(n_sub)):
        attend(n_old + kb, kb * ATT_BLK, ATT_BLK, causal)
        if kb + 1 < n_sub:
            attend(n_old + kb, (kb + 1) * ATT_BLK, (n_sub - kb - 1) * ATT_BLK, None)

    assert n_sub % 4 == 0
    def block_of(t):
        return jnp.maximum(n_old - 1 - t, 0)

    def older_blocks(g):
        heads = range(HEAD_PAIR * g, HEAD_PAIR * (g + 1))

        def stage1(p, z_slot):
            j0, j1 = block_of(2 * p), block_of(2 * p + 1)
            for k, h in enumerate(heads):
                rows_h = slice(HEAD_DIM * h, HEAD_DIM * (h + 1))
                kt_pair = jnp.concatenate([
                    jnp.concatenate([kt_ref[j0, rows_h, :], kt_ref[j1, rows_h, :]], axis=1),
                    brows_pair[h]], axis=0)
                z = _dot(q_ext[h], kt_pair)
                z_s[z_slot, 0, k] = z[:, :ATT_BLK]
                z_s[z_slot, 1, k] = z[:, ATT_BLK:]

        def stage2(z_slot, blk, d_slot):
            for k in range(HEAD_PAIR):
                d, tot = _sb_scores(z_s[z_slot, blk, k], uo_ref[...], None)
                d_s[d_slot, k] = d
                t_s[d_slot, k] = tot

        def stage3(t, d_slot):
            ws = []
            for k, h in enumerate(heads):
                r = r_ref[h]
                ws.append(_sb_weights(d_s[d_slot, k], r, None).astype(BF16))
                r_ref[h] = r + t_s[d_slot, k]
            acc_ref[g] += _dot(jnp.concatenate(ws, axis=1), values(block_of(t), g))

        stage1(0, 0)
        stage2(0, 0, 0)

        def four_blocks(v, carry):
            for half in range(2):
                p = 2 * v + half
                stage1(p + 1, 1 - half)
                stage2(half, 1, 1)
                stage3(2 * p, 0)
                stage2(1 - half, 0, 0)
                stage3(2 * p + 1, 1)
            return carry

        lax.fori_loop(0, n_old // 4, four_blocks, 0)

    for g in range(HEADS // HEAD_PAIR):
        older_blocks(g)
        o_ref[:, PAIR_W * g:PAIR_W * (g + 1)] = acc_ref[g]


def _attn_call(brows, uo, q, kt, vb):
    length = q.shape[0]
    return pl.pallas_call(
        _attn_kernel,
        out_shape=jax.ShapeDtypeStruct((length, BRANCH_W), F32),
        grid=(length // ATT_Q,),
        in_specs=[
            _const_spec(brows.shape), _const_spec(uo.shape),
            pl.BlockSpec((ATT_Q, BRANCH_W), lambda i: (i, 0)),
            _const_spec(kt.shape), _const_spec(vb.shape),
        ],
        out_specs=pl.BlockSpec((ATT_Q, BRANCH_W), lambda i: (i, 0)),
        scratch_shapes=[
            pltpu.VMEM((HEADS // HEAD_PAIR, ATT_Q, PAIR_W), F32),
            pltpu.VMEM((HEADS, ATT_Q, ATT_BLK), F32),
            pltpu.VMEM((2, 2, HEAD_PAIR, ATT_Q, ATT_BLK), F32),
            pltpu.VMEM((2, HEAD_PAIR, ATT_Q, ATT_BLK), F32),
            pltpu.VMEM((2, HEAD_PAIR, ATT_Q, ATT_BLK), F32),
        ],
        compiler_params=pltpu.CompilerParams(
            dimension_semantics=("arbitrary",), vmem_limit_bytes=VMEM_LIMIT),
        name="prompt_attn",
    )(brows, uo, q, kt, vb)


def _back_kernel(x_ref, ob_ref, macd_ref, gb_ref, g1_ref, n2_ref, sc2_ref, sh2_ref, g2_ref,
                 wb1_ref, wo_ref, w1_ref, w2_ref, y_ref):
    merged = macd_ref[...] + gb_ref[...] * _dot(ob_ref[...].astype(BF16), wb1_ref[...])
    x1 = x_ref[...] + g1_ref[...] * _dot(merged.astype(BF16), wo_ref[...])
    h2 = (_rms_rows(x1) * n2_ref[...] * (1.0 + sc2_ref[...]) + sh2_ref[...]).astype(BF16)
    hid = jnp.maximum(_dot(h2, w1_ref[...]), 0.0)
    y_ref[...] = x1 + g2_ref[...] * _dot((hid * hid).astype(BF16), w2_ref[...])


def _back_call(x, ob, macd, gb, g1, n2, sc2, sh2, g2, wts, tm):
    rows = x.shape[0]
    mod_rows = g1.shape[0]
    if mod_rows == 1:
        mod_spec = _const_spec((1, D_MODEL))
    else:
        mod_spec = pl.BlockSpec((tm, D_MODEL), lambda i: (i, 0))
    row_spec = lambda w: pl.BlockSpec((tm, w), lambda i: (i, 0))
    weights = [wts['wb1'], wts['wo'], wts['w1'], wts['w2']]
    return pl.pallas_call(
        _back_kernel,
        out_shape=jax.ShapeDtypeStruct((rows, D_MODEL), F32),
        grid=(rows // tm,),
        in_specs=[row_spec(D_MODEL), row_spec(BRANCH_W), row_spec(D_MODEL), row_spec(D_MODEL),
                  mod_spec, _const_spec((1, D_MODEL)), mod_spec, mod_spec, mod_spec]
                 + [_const_spec(w.shape) for w in weights],
        out_specs=row_spec(D_MODEL),
        compiler_params=pltpu.CompilerParams(
            dimension_semantics=("arbitrary",), vmem_limit_bytes=VMEM_LIMIT),
        name="layer_back",
    )(x, ob, macd, gb, g1, n2, sc2, sh2, g2, *weights)


def _sample_front_kernel(x_ref, g1_ref, sc_ref, sh_ref, win_ref, wbr_ref,
                         p_ref, a0_ref, convw_ref, hmask_ref, tdiag_ref, conv_ref, s0_ref,
                         q_ref, k_ref, v_ref, macd_ref, gb_ref, convnew_ref, snew_ref, av_ref):
    nb = x_ref.shape[0]
    wa_ref, wb_ref, wc_ref, wd_ref, wg_ref = _in_proj_views(win_ref)
    prm = p_ref[...]
    row = lambda i: prm[i:i + 1, :]
    hmask = hmask_ref[...]

    x = x_ref[...]
    h = (_rms_rows(x) * g1_ref[...] * (1.0 + sc_ref[...]) + sh_ref[...]).astype(BF16)

    ga = jax.nn.gelu(_dot(h, wa_ref[...]))
    vn = _layer_norm(ga[:, BRANCH_W:], row(P_ALN_G), row(P_ALN_B))
    av_ref[...] = vn
    out_a = ga[:, :BRANCH_W] * (a0_ref[0:1, :] * vn + a0_ref[1:2, :])

    zb = _dot(h, wb_ref[...])
    qn = _head_rms(zb[:, :BRANCH_W], hmask, row(P_QN_G))
    kn = _head_rms(zb[:, BRANCH_W:2 * BRANCH_W], hmask, row(P_KN_G))
    q_ref[...] = qn * QK_SCALE
    k_ref[...] = kn
    v_ref[...] = zb[:, 2 * BRANCH_W:]

    zc = _dot(h, wc_ref[...])
    cin = zc[:, :BRANCH_W] * _sigmoid(zc[:, BRANCH_W:])
    acc = row(P_CONV_B) + convw_ref[CONV_W - 1:CONV_W, :] * cin
    for j in range(CONV_W - 1):
        acc = acc + convw_ref[j:j + 1, :] * conv_ref[j]
        if j >= 1:
            convnew_ref[j - 1] = conv_ref[j]
    convnew_ref[CONV_W - 2] = cin
    out_c = _silu(_layer_norm(acc, row(P_CLN_G), row(P_CLN_B)))

    zd = _dot(h, wd_ref[...])
    qd = _silu(zd[:, :BRANCH_W]) * QK_SCALE
    logf = _log_forget(zd[:, BRANCH_W:2 * BRANCH_W], row(P_LB))
    ea = jnp.exp(logf)
    kd = _one_minus_forget(zd[:, BRANCH_W:2 * BRANCH_W], row(P_LB))
    di = zd[:, 2 * BRANCH_W:3 * BRANCH_W]
    tdiag = tdiag_ref[...]

    def spread(r):
        t3 = jnp.broadcast_to(r[:, None, :], (nb, HEAD_DIM, BRANCH_W)) * tdiag[None]
        flat = _dot_split_lhs(t3.reshape(nb * HEAD_DIM, BRANCH_W), hmask, 3)
        return flat.reshape(nb, HEAD_DIM, BRANCH_W)

    s0 = s0_ref[...]
    od = jnp.sum(spread(qd * ea) * s0, axis=1) + _dot_split_lhs(qd * kd, hmask, 3) * di
    snew_ref[...] = spread(ea) * s0 + spread(kd) * di[:, None, :]
    out_d = _head_rms(od, hmask, row(P_DN_G)) * _silu(zd[:, 3 * BRANCH_W:])

    macd, gb = _gated_local_merge(h, wg_ref, out_a, out_c, out_d, wbr_ref)
    macd_ref[...] = macd
    gb_ref[...] = gb


def _sample_front_call(x, g1, sc, sh, wts, consts, conv_t, s0_r):
    nb = x.shape[0]
    in_arrays = [x, g1, sc, sh, wts['win'], wts['wbr'],
                 wts['ptab'], wts['a0'], wts['conv_w'], consts['hmask'], consts['tdiag'], conv_t, s0_r]
    out_shape = [
        jax.ShapeDtypeStruct((nb, BRANCH_W), F32),
        jax.ShapeDtypeStruct((nb, BRANCH_W), F32),
        jax.ShapeDtypeStruct((nb, BRANCH_W), F32),
        jax.ShapeDtypeStruct((nb, D_MODEL), F32),
        jax.ShapeDtypeStruct((nb, D_MODEL), F32),
        jax.ShapeDtypeStruct(conv_t.shape, F32),
        jax.ShapeDtypeStruct(s0_r.shape, F32),
        jax.ShapeDtypeStruct((nb, BRANCH_W), F32),
    ]
    return pl.pallas_call(
        _sample_front_kernel,
        out_shape=out_shape,
        grid=(1,),
        in_specs=[_const_spec(a.shape) for a in in_arrays],
        out_specs=[_const_spec(s.shape) for s in out_shape],
        compiler_params=pltpu.CompilerParams(vmem_limit_bytes=VMEM_LIMIT),
        name="sample_front",
    )(*in_arrays)


Q_ROWS = 16


def _paged_attn_kernel(pt_ref, q_ref, bias_ref, uo_ref, *refs):
    k_refs = refs[:PAGES_PER_STEP]
    v_refs = refs[PAGES_PER_STEP:2 * PAGES_PER_STEP]
    o_ref, acc_ref, r_ref = refs[2 * PAGES_PER_STEP:]
    jc = pl.program_id(1)

    @pl.when(jc == 0)
    def _():
        acc_ref[...] = jnp.zeros_like(acc_ref)
        r_ref[...] = jnp.zeros_like(r_ref)

    row_id = lax.broadcasted_iota(jnp.int32, (Q_ROWS, BRANCH_W), 0)
    own_head = row_id == lax.broadcasted_iota(jnp.int32, (Q_ROWS, BRANCH_W), 1) // HEAD_DIM
    qrows = jnp.where(own_head, jnp.broadcast_to(q_ref[...], (Q_ROWS, BRANCH_W)), 0.0).astype(BF16)
    as_matrix = lambda ref: ref[...].reshape(BRANCH_W, CHUNK).astype(BF16)
    pages = range(PAGES_PER_STEP)
    kt_all = jnp.concatenate([as_matrix(k_refs[p]) for p in pages], axis=1)
    z = _dot(qrows, kt_all)
    z_rows = jnp.concatenate(
        [z[:, p * CHUNK:(p + 1) * CHUNK] + bias_ref[...] for p in pages], axis=0)
    d, tot = _sb_scores(z_rows, uo_ref[...], None)
    r = r_ref[...]
    r_pages = [None] * PAGES_PER_STEP
    for p in reversed(pages):
        r_pages[p] = r
        r = r + tot[p * Q_ROWS:(p + 1) * Q_ROWS, :]
    r_ref[...] = r
    w = _sb_weights(d, jnp.concatenate(r_pages, axis=0), None).astype(BF16)
    w_all = jnp.concatenate([w[p * Q_ROWS:(p + 1) * Q_ROWS, :] for p in pages], axis=1)
    vt_all = jnp.concatenate([as_matrix(v_refs[p]) for p in pages], axis=1)
    acc = acc_ref[...] + _dot_nt(w_all, vt_all)
    acc_ref[...] = acc

    @pl.when(jc == pl.num_programs(1) - 1)
    def _():
        o_ref[...] = jnp.sum(jnp.where(own_head, acc, 0.0), axis=0, keepdims=True)


def _paged_attn_call(page_table, q, bias_rows, uo, cache_k, cache_v, layer):
    nb, n_pages = page_table.shape
    n_chunks = n_pages // PAGES_PER_STEP

    def page_spec(p):
        def index_map(b, jc, pt):
            return (pt[b, (n_chunks - 1 - jc) * PAGES_PER_STEP + p], layer, 0, 0, 0)
        return pl.BlockSpec((None, None, HEADS, HEAD_DIM, CHUNK), index_map)

    page_specs = [page_spec(p) for p in range(PAGES_PER_STEP)]
    grid_spec = pltpu.PrefetchScalarGridSpec(
        num_scalar_prefetch=1,
        grid=(nb, n_chunks),
        in_specs=[
            pl.BlockSpec((None, 1, BRANCH_W), lambda b, jc, pt: (b, 0, 0)),
            pl.BlockSpec(bias_rows.shape, lambda b, jc, pt: (0, 0)),
            pl.BlockSpec(uo.shape, lambda b, jc, pt: (0, 0)),
        ] + page_specs + page_specs,
        out_specs=pl.BlockSpec((None, 1, BRANCH_W), lambda b, jc, pt: (b, 0, 0)),
        scratch_shapes=[
            pltpu.VMEM((Q_ROWS, BRANCH_W), F32),
            pltpu.VMEM((Q_ROWS, CHUNK), F32),
        ],
    )
    return pl.pallas_call(
        _paged_attn_kernel,
        out_shape=jax.ShapeDtypeStruct((nb, 1, BRANCH_W), F32),
        grid_spec=grid_spec,
        compiler_params=pltpu.CompilerParams(
            dimension_semantics=("arbitrary", "arbitrary"), vmem_limit_bytes=VMEM_LIMIT),
        name="sample_paged_attn",
    )(page_table, q.reshape(nb, 1, BRANCH_W), bias_rows, uo,
      *([cache_k] * PAGES_PER_STEP), *([cache_v] * PAGES_PER_STEP))


def _constants():
    lane = jnp.arange(BRANCH_W)
    hmask = (lane[:, None] // HEAD_DIM == lane[None, :] // HEAD_DIM).astype(BF16)
    r = jnp.arange(ROW_TILE)
    tri = ((r[:, None] // SUB == r[None, :] // SUB) & (r[None, :] <= r[:, None])).astype(BF16)
    tdiag = (jnp.arange(HEAD_DIM)[:, None] == lane[None, :] % HEAD_DIM).astype(F32)
    j = jnp.arange(2 * ATT_BLK) % ATT_BLK
    c = jnp.arange(2 * ATT_BLK)
    uo = ((c[None, :] >= ATT_BLK) | (j[:, None] >= c[None, :])).astype(BF16)
    return {'hmask': hmask, 'tri': tri, 'tdiag': tdiag, 'uo': uo}


def _layer_weights(l, lb, w_in, a_ln_g, a_ln_b, a_ws, a_bs, b_qn_g, b_kn_g, c_conv_w, c_conv_b,
                   c_ln_g, c_ln_b, d_norm_g, w_branch, w_out, w_ff1, w_ff2):
    tile4 = lambda g: jnp.tile(g, HEADS)
    rows = [a_ln_g[l], a_ln_b[l], tile4(b_qn_g[l]), tile4(b_kn_g[l]), c_conv_b[l], c_ln_g[l], c_ln_b[l],
            lb, tile4(d_norm_g[l])]
    ptab = jnp.zeros((16, BRANCH_W), F32).at[:len(rows)].set(jnp.stack(rows))
    grp = BRANCH_W // HEADS
    wbr = w_branch[l].astype(BF16)
    return {
        'win': w_in[l].astype(BF16),
        'wbr': wbr, 'wb1': wbr[1], 'ptab': ptab,
        'a_ws': a_ws[l], 'a_bias': jnp.repeat(a_bs[l].T, grp, axis=1),
        'a0': jnp.stack([jnp.repeat(a_ws[l][:, 0, 0], grp), jnp.repeat(a_bs[l][:, 0], grp)]),
        'conv_w': jnp.zeros((32, BRANCH_W), F32).at[:CONV_W].set(c_conv_w[l]),
        'wo': w_out[l].astype(BF16), 'w1': w_ff1[l].astype(BF16), 'w2': w_ff2[l].astype(BF16),
    }


def kernel(x_prompt, x_sample, c_prompt, c_sample, cache_k, cache_v, state_conv, state_hgrn, page_table,
           ada_w, ada_b, norm1_g, norm2_g, w_in, a_ln_g, a_ln_b, a_ws, a_bs, b_qn_g, b_kn_g, b_bias,
           c_conv_w, c_conv_b, c_ln_g, c_ln_b, d_lb_logits, d_norm_g, w_branch, w_out, w_ff1, w_ff2):
    depth = w_in.shape[0]
    seq = x_prompt.shape[1]
    nb = x_sample.shape[0]
    n_pool = cache_k.shape[0]
    consts = _constants()

    cp = jnp.cumsum(jax.nn.softmax(d_lb_logits.astype(F32), axis=0), axis=0)
    lbs = cp - cp[0:1]

    ada_rows = 8 * ((1 + nb + 7) // 8)
    c_all = jnp.zeros((ada_rows, D_MODEL), F32).at[0:1].set(c_prompt).at[1:1 + nb].set(c_sample)
    mods = _ada_call(c_all, ada_w, ada_b)

    ck = cache_k.transpose(0, 1, 3, 4, 2)
    cv = cache_v.transpose(0, 1, 3, 4, 2)

    yp = x_prompt[0]
    ys = x_sample[:, 0]
    outs = {n: [] for n in ('ks', 'vs', 'cp', 'cs', 'sp', 'ss', 'av')}
    kv_all = None
    for l in range(depth):
        wts = _layer_weights(l, lbs[l], w_in, a_ln_g, a_ln_b, a_ws, a_bs, b_qn_g, b_kn_g, c_conv_w,
                             c_conv_b, c_ln_g, c_ln_b, d_norm_g, w_branch, w_out, w_ff1, w_ff2)
        mp = [mods[l, 0:1, i * D_MODEL:(i + 1) * D_MODEL] for i in range(6)]
        ms = [mods[l, 1:1 + nb, i * D_MODEL:(i + 1) * D_MODEL] for i in range(6)]
        n1 = norm1_g[l][None]
        n2 = norm2_g[l][None]
        bias_rows = jnp.zeros((Q_ROWS, ATT_BLK), F32).at[:HEADS].set(
            jnp.broadcast_to(b_bias[l][:, None], (HEADS, ATT_BLK)))
        b_hi = b_bias[l].astype(BF16)
        b_lo = (b_bias[l] - b_hi.astype(F32)).astype(BF16)
        brows = jnp.zeros((HEADS, HEAD_DIM, ATT_BLK), BF16)
        brows = brows.at[:, 0, :].set(b_hi[:, None]).at[:, 1, :].set(b_lo[:, None])

        q, k_all, v_all, kt, vb, macd, gb, conv_tail, st = _front_call(
            yp, n1, mp[1], mp[0], wts, consts, l, depth, kv_all)
        kv_all = (k_all, v_all)
        ob = _attn_call(brows, consts['uo'], q, kt, vb)
        yp = _back_call(yp, ob, macd, gb, mp[2], n2, mp[4], mp[3], mp[5], wts, BACK_TILE)
        outs['cp'].append(conv_tail[CONV_HALO - (CONV_W - 1):][None])
        st4 = st.reshape(HEADS, HEAD_DIM, HEADS, HEAD_DIM)
        outs['sp'].append(jnp.stack([st4[h, :, h, :].T for h in range(HEADS)])[None])

        conv_t = state_conv[:, l].transpose(1, 0, 2)
        s0_r = state_hgrn[:, l].transpose(0, 2, 1, 3).reshape(nb, HEAD_DIM, BRANCH_W)
        qs, ksn, vsn, macd_s, gb_s, conv_new, s_new, avn = _sample_front_call(
            ys, n1, ms[1], ms[0], wts, consts, conv_t, s0_r)
        obs = _paged_attn_call(page_table, qs, bias_rows, consts['uo'], ck, cv, l)
        ys = _back_call(ys, obs[:, 0], macd_s, gb_s, ms[2], n2, ms[4], ms[3], ms[5], wts, nb)
        outs['ks'].append(ksn.reshape(nb, 1, HEADS, HEAD_DIM))
        outs['vs'].append(vsn.reshape(nb, 1, HEADS, HEAD_DIM))
        outs['cs'].append(conv_new.transpose(1, 0, 2))
        outs['ss'].append(s_new.reshape(nb, HEAD_DIM, HEADS, HEAD_DIM).transpose(0, 2, 1, 3))
        outs['av'].append(avn[:, None, :])

    stack = lambda n: jnp.stack(outs[n], axis=1)
    rows_out = lambda a: a.reshape(1, depth, HEADS, HEAD_DIM, seq).transpose(0, 1, 4, 2, 3)
    return (yp[None], ys[:, None, :],
            rows_out(kv_all[0]), rows_out(kv_all[1]), stack('ks'), stack('vs'),
            stack('cp'), stack('cs'), stack('sp'), stack('ss'), stack('av'))
```

```python
import functools

import jax
import jax.numpy as jnp
from jax import lax
from jax.experimental import pallas as pl
from jax.experimental.pallas import tpu as pltpu

F32 = jnp.float32
BF16 = jnp.bfloat16

D_MODEL = 1024
BRANCH_W = 256
HEADS = 4
HEAD_DIM = 64
CHUNK = 128
CONV_W = 31
D_FF = 4096
EPS = 1e-6
QK_SCALE = HEAD_DIM ** -0.5
LOG2E = 1.4426950408889634
MASKED_EXPONENT = -1e30

ROW_TILE = 256
BACK_TILE = 512
SUB = 16
CONV_HALO = 32
CONV_SHIFTS = 8
CONV_BASE = CONV_HALO + CONV_SHIFTS
ATT_BLK = 128
ATT_Q = 512
HEAD_PAIR = 2
PAIR_W = HEAD_PAIR * HEAD_DIM
PAGES_PER_STEP = 32
VMEM_LIMIT = 56 * 1024 * 1024


def _dot(a, b):
    return jnp.dot(a, b, preferred_element_type=F32)


def _bf16_pieces(a, n):
    pieces, rem = [], a
    for _ in range(n):
        p = rem.astype(BF16)
        pieces.append(p)
        rem = rem - p.astype(F32)
    return pieces


def _dot_split_lhs(a, ones_bf, n):
    return _dot(jnp.concatenate(_bf16_pieces(a, n), axis=1), jnp.concatenate([ones_bf] * n, axis=0))


def _dot_split_rhs(ones_bf, b, n):
    return _dot(jnp.concatenate([ones_bf] * n, axis=1), jnp.concatenate(_bf16_pieces(b, n), axis=0))


def _dot_nt(a, b):
    return lax.dot_general(a, b, (((1,), (1,)), ((), ())), preferred_element_type=F32)


def _dot_tn(a, b):
    return lax.dot_general(a, b, (((0,), (0,)), ((), ())), preferred_element_type=F32)


def _sigmoid(x):
    return 0.5 * jnp.tanh(0.5 * x) + 0.5


def _silu(x):
    return x * _sigmoid(x)


def _softplus(x):
    return jnp.maximum(x, 0.0) + jnp.log1p(jnp.exp(-jnp.abs(x)))


def _rms_rows(x):
    return x * lax.rsqrt(jnp.mean(x * x, axis=-1, keepdims=True) + EPS)


def _layer_norm(x, g, b):
    mu = jnp.mean(x, axis=-1, keepdims=True)
    xc = x - mu
    var = jnp.mean(xc * xc, axis=-1, keepdims=True)
    return xc * lax.rsqrt(var + EPS) * g + b


def _head_rms(x, hmask, g):
    ms = _dot_split_lhs(x * x, hmask, 2) * (1.0 / HEAD_DIM)
    return x * lax.rsqrt(ms + EPS) * g


def _log_forget(df, lb):
    ls = -_softplus(-df)
    b = jnp.log1p(-lb) + ls
    pos = lb > 0.0
    a = jnp.log(jnp.where(pos, lb, 1.0))
    lae = jnp.maximum(a, b) + jnp.log1p(jnp.exp(-jnp.abs(a - b)))
    return jnp.where(pos, lae, b)


def _one_minus_forget(df, lb):
    return (1.0 - lb) * _sigmoid(-df)


def _const_spec(shape):
    nd = len(shape)
    return pl.BlockSpec(shape, lambda *_: (0,) * nd, pipeline_mode=pl.Buffered(1))


class _Stacked:
    def __init__(self, arr, *prefix):
        self.arr, self.prefix = arr, prefix


def _operands(items):
    arrays, specs = [], []
    for a in items:
        if isinstance(a, _Stacked):
            tail = a.arr.shape[len(a.prefix):]
            index = a.prefix + (0,) * len(tail)
            arrays.append(a.arr)
            specs.append(pl.BlockSpec((None,) * len(a.prefix) + tail, lambda *_, index=index: index,
                                      pipeline_mode=pl.Buffered(1)))
        else:
            arrays.append(a)
            specs.append(_const_spec(a.shape))
    return arrays, specs


def _ada_kernel(c_ref, w_ref, b_ref, o_ref):
    s = _silu(c_ref[...]).astype(BF16)
    o_ref[...] = _dot(s, w_ref[...].astype(BF16)) + b_ref[...]


def _ada_call(c_all, ada_w, ada_b):
    depth = ada_w.shape[0]
    rows = c_all.shape[0]
    n_col = ada_w.shape[2] // D_MODEL
    return pl.pallas_call(
        _ada_kernel,
        out_shape=jax.ShapeDtypeStruct((depth, rows, ada_w.shape[2]), F32),
        grid=(depth, n_col),
        in_specs=[
            pl.BlockSpec((rows, D_MODEL), lambda l, j: (0, 0)),
            pl.BlockSpec((None, D_MODEL, D_MODEL), lambda l, j: (l, 0, j)),
            pl.BlockSpec((None, 1, D_MODEL), lambda l, j: (l, 0, j)),
        ],
        out_specs=pl.BlockSpec((None, rows, D_MODEL), lambda l, j: (l, 0, j)),
        compiler_params=pltpu.CompilerParams(vmem_limit_bytes=VMEM_LIMIT),
        name="ada_params",
    )(c_all, ada_w, ada_b.reshape(depth, 1, -1))


P_ALN_G, P_ALN_B, P_QN_G, P_KN_G, P_CONV_B, P_CLN_G, P_CLN_B, P_LB, P_DN_G = range(9)


def _gated_local_merge(h, wg_ref, out_a, out_c, out_d, wbr_ref):
    gate = lambda k: _sigmoid(_dot(h, wg_ref[:, k * D_MODEL:(k + 1) * D_MODEL]))
    macd = gate(0) * _dot(out_a.astype(BF16), wbr_ref[0])
    macd = macd + gate(2) * _dot(out_c.astype(BF16), wbr_ref[2])
    macd = macd + gate(3) * _dot(out_d.astype(BF16), wbr_ref[3])
    return macd, gate(1)


def _in_proj_views(win_ref):
    bounds, start = [], 0
    for width in (2 * BRANCH_W, 3 * BRANCH_W, 2 * BRANCH_W, 4 * BRANCH_W, HEADS * D_MODEL):
        bounds.append((start, start + width))
        start += width
    return [win_ref.at[:, lo:hi] for lo, hi in bounds]


def _front_kernel(n_aliased, *refs):
    (x_ref, g1_ref, sc_ref, sh_ref, win_ref, wbr_ref,
     p_ref, aws_ref, abias_ref, convw_ref, hmask_ref, tri_ref,
     q_ref, ktf_ref, vtf_ref, kt_ref, vb_ref, macd_ref, gb_ref, convnew_ref, state_ref,
     convbuf, st_ref) = refs[n_aliased:]
    tm = x_ref.shape[0]
    step = pl.program_id(0)
    wa_ref, wb_ref, wc_ref, wd_ref, wg_ref = _in_proj_views(win_ref)

    @pl.when(step == 0)
    def _():
        convbuf[...] = jnp.zeros_like(convbuf)
        st_ref[...] = jnp.zeros_like(st_ref)

    prm = p_ref[...]
    row = lambda i: prm[i:i + 1, :]
    hmask = hmask_ref[...]
    hmask_f32 = hmask.astype(F32)

    x = x_ref[...]
    h = (_rms_rows(x) * g1_ref[...] * (1.0 + sc_ref[...]) + sh_ref[...]).astype(BF16)

    za = _dot(h, wa_ref[...])
    zb = _dot(h, wb_ref[...])
    zc = _dot(h, wc_ref[...])
    zd = _dot(h, wd_ref[...])

    n_c = tm // SUB
    in_chunks = lambda a: a.reshape(n_c, SUB, BRANCH_W)
    qd = _silu(zd[:, :BRANCH_W]) * QK_SCALE
    logf = _log_forget(zd[:, BRANCH_W:2 * BRANCH_W], row(P_LB))
    kd = _one_minus_forget(zd[:, BRANCH_W:2 * BRANCH_W], row(P_LB))
    di = zd[:, 2 * BRANCH_W:3 * BRANCH_W]
    b = _dot_split_rhs(tri_ref[...], logf, 3)
    b3, qd3, kd3, di3 = in_chunks(b), in_chunks(qd), in_chunks(kd), in_chunks(di)
    bl3 = b3[:, SUB - 1:SUB, :]
    qt = (qd * jnp.exp(b)).astype(BF16)
    kt = (kd3 * jnp.exp(bl3 - b3)).reshape(tm, BRANCH_W).astype(BF16)
    el = jnp.exp(bl3)
    di_bf = di.astype(BF16)
    increments = [_dot_tn(di_bf[c * SUB:(c + 1) * SUB], kt[c * SUB:(c + 1) * SUB]) * hmask_f32
                  for c in range(n_c)]

    ga = jax.nn.gelu(za)
    u = ga[:, :BRANCH_W]
    vn = _layer_norm(ga[:, BRANCH_W:], row(P_ALN_G), row(P_ALN_B))
    lane_grp = lax.broadcasted_iota(jnp.int32, (CHUNK, BRANCH_W), 1) // (BRANCH_W // HEADS)
    tril = (lax.broadcasted_iota(jnp.int32, (CHUNK, CHUNK), 1)
            <= lax.broadcasted_iota(jnp.int32, (CHUNK, CHUNK), 0))
    w_causal = [jnp.where(tril, aws_ref[g], 0.0).astype(BF16) for g in range(HEADS)]
    mixed_chunks = []
    for c in range(tm // CHUNK):
        vc = vn[c * CHUNK:(c + 1) * CHUNK, :].astype(BF16)
        mixed = None
        for g in range(HEADS):
            mg = _dot(w_causal[g], vc)
            mixed = mg if mixed is None else jnp.where(lane_grp == g, mg, mixed)
        mixed_chunks.append(mixed + abias_ref[...])
    out_a = u * jnp.concatenate(mixed_chunks, axis=0)

    qn = _head_rms(zb[:, :BRANCH_W], hmask, row(P_QN_G))
    kn = _head_rms(zb[:, BRANCH_W:2 * BRANCH_W], hmask, row(P_KN_G))
    vv = zb[:, 2 * BRANCH_W:]
    q_ref[...] = (qn * QK_SCALE).astype(BF16)
    knt_f32 = kn.T
    ktf_ref[...] = knt_f32
    vtf_ref[...] = vv.T
    vb_ref[...] = vv.astype(BF16)
    knt = knt_f32.astype(BF16)
    for c in range(tm // ATT_BLK):
        kt_ref[c] = knt[:, c * ATT_BLK:(c + 1) * ATT_BLK]

    cin = zc[:, :BRANCH_W] * _sigmoid(zc[:, BRANCH_W:])
    for r in range(CONV_SHIFTS):
        convbuf[r, CONV_BASE - r:CONV_BASE - r + tm, :] = cin
    acc = jnp.broadcast_to(row(P_CONV_B), (tm, BRANCH_W))
    for j in range(CONV_W):
        lead = CONV_BASE - (CONV_W - 1) + j
        r = lead % CONV_SHIFTS
        acc = acc + convw_ref[j:j + 1, :] * convbuf[r, lead - r:lead - r + tm, :]
    out_c = _silu(_layer_norm(acc, row(P_CLN_G), row(P_CLN_B)))
    convnew_ref[...] = convbuf[0, tm + CONV_BASE - CONV_HALO:tm + CONV_BASE, :]
    for r in range(CONV_SHIFTS):
        convbuf[r, 0:CONV_BASE, :] = convbuf[r, tm:tm + CONV_BASE, :]

    t_idx = lax.broadcasted_iota(jnp.int32, (n_c, SUB, BRANCH_W), 1)
    od = jnp.zeros((tm, BRANCH_W), F32)
    gate_cols = []
    assert SUB * BRANCH_W == HEADS * D_MODEL
    for s in range(SUB):
        e = jnp.exp(jnp.where(t_idx >= s, b3 - b3[:, s:s + 1, :], MASKED_EXPONENT))
        wgt = e * qd3 * kd3[:, s:s + 1, :]
        coef = _dot(wgt.reshape(tm, BRANCH_W).astype(BF16), hmask)
        i_s = jnp.broadcast_to(di3[:, s:s + 1, :], (n_c, SUB, BRANCH_W)).reshape(tm, BRANCH_W)
        od = od + coef * i_s
        gate_cols.append(_sigmoid(_dot(h, wg_ref[:, s * BRANCH_W:(s + 1) * BRANCH_W])))
    per_gate = D_MODEL // BRANCH_W
    gate = lambda k: jnp.concatenate(gate_cols[k * per_gate:(k + 1) * per_gate], axis=1)
    gate_a, gate_c, gate_d = gate(0), gate(2), gate(3)
    gb_ref[...] = gate(1)

    st = st_ref[...]
    o_inter = []
    for c in range(n_c):
        o_inter.append(_dot_nt(qt[c * SUB:(c + 1) * SUB], st.astype(BF16)))
        st = st * el[c] + increments[c]
    st_ref[...] = st
    state_ref[...] = st
    od = od + jnp.concatenate(o_inter, axis=0)
    out_d = _head_rms(od, hmask, row(P_DN_G)) * _silu(zd[:, 3 * BRANCH_W:])

    macd = gate_a * _dot(out_a.astype(BF16), wbr_ref[0])
    macd = macd + gate_c * _dot(out_c.astype(BF16), wbr_ref[2])
    macd_ref[...] = macd + gate_d * _dot(out_d.astype(BF16), wbr_ref[3])


def _front_call(x, g1, sc, sh, wts, consts, layer, depth, kv_prev):
    length = x.shape[0]
    tm = ROW_TILE
    n_steps = length // tm
    row_spec = lambda w: pl.BlockSpec((tm, w), lambda i: (i, 0))
    const_arrays, const_specs = _operands([g1, sc, sh, wts['win'], wts['wbr'],
                                           wts['ptab'], wts['a_ws'], wts['a_bias'], wts['conv_w'],
                                           consts['hmask'], consts['tri']])
    aliased = [] if kv_prev is None else list(kv_prev)
    in_arrays = aliased + [x] + const_arrays
    in_specs = [pl.BlockSpec(memory_space=pl.ANY)] * len(aliased) + [row_spec(D_MODEL)] + const_specs
    kv_spec = pl.BlockSpec((None, BRANCH_W, tm), lambda i: (layer, 0, i))
    out_shape = [
        jax.ShapeDtypeStruct((length, BRANCH_W), BF16),
        jax.ShapeDtypeStruct((depth, BRANCH_W, length), F32),
        jax.ShapeDtypeStruct((depth, BRANCH_W, length), F32),
        jax.ShapeDtypeStruct((length // ATT_BLK, BRANCH_W, ATT_BLK), BF16),
        jax.ShapeDtypeStruct((length, BRANCH_W), BF16),
        jax.ShapeDtypeStruct((length, D_MODEL), F32),
        jax.ShapeDtypeStruct((length, D_MODEL), F32),
        jax.ShapeDtypeStruct((CONV_HALO, BRANCH_W), F32),
        jax.ShapeDtypeStruct((BRANCH_W, BRANCH_W), F32),
    ]
    out_specs = [
        row_spec(BRANCH_W), kv_spec, kv_spec,
        pl.BlockSpec((tm // ATT_BLK, BRANCH_W, ATT_BLK), lambda i: (i, 0, 0)),
        row_spec(BRANCH_W), row_spec(D_MODEL), row_spec(D_MODEL),
        _const_spec((CONV_HALO, BRANCH_W)), _const_spec((BRANCH_W, BRANCH_W)),
    ]
    scratch = [
        pltpu.VMEM((CONV_SHIFTS, tm + CONV_BASE, BRANCH_W), F32),
        pltpu.VMEM((BRANCH_W, BRANCH_W), F32),
    ]
    return pl.pallas_call(
        functools.partial(_front_kernel, len(aliased)),
        out_shape=out_shape,
        grid=(n_steps,),
        in_specs=in_specs,
        out_specs=out_specs,
        scratch_shapes=scratch,
        input_output_aliases={n: 1 + n for n in range(len(aliased))},
        compiler_params=pltpu.CompilerParams(
            dimension_semantics=("arbitrary",), vmem_limit_bytes=VMEM_LIMIT),
        name="prompt_front",
    )(*in_arrays)


def _sb_scores(z, uo, causal):
    sp = jnp.maximum(z, 0.0) + jnp.log(1.0 + jnp.exp2(jnp.abs(z) * (-LOG2E)))
    if causal is not None:
        sp = jnp.where(causal, sp, 0.0)
    hi = sp.astype(BF16)
    lo = (sp - hi.astype(F32)).astype(BF16)
    cs = _dot(jnp.concatenate([hi, lo], axis=1), uo)
    return z - cs[:, :ATT_BLK], cs[:, ATT_BLK:]


def _sb_weights(d, r, causal):
    w = jnp.exp(d - r)
    if causal is not None:
        w = jnp.where(causal, w, 0.0)
    return w


def _attn_kernel(brows_ref, uo_ref, q_ref, kt_ref, v_ref, o_ref, acc_ref, r_ref, z_s, d_s, t_s):
    i = pl.program_id(0)
    n_sub = ATT_Q // ATT_BLK
    n_old = i * n_sub
    q = q_ref[...]
    unit_cols = jnp.where(lax.broadcasted_iota(jnp.int32, (ATT_Q, HEAD_DIM), 1) < 2, 1.0, 0.0).astype(BF16)
    q_ext = [jnp.concatenate([q[:, HEAD_DIM * h:HEAD_DIM * (h + 1)], unit_cols], axis=1)
             for h in range(HEADS)]
    acc_ref[...] = jnp.zeros_like(acc_ref)
    r_ref[...] = jnp.zeros_like(r_ref)
    lane_half = lax.broadcasted_iota(jnp.int32, (ATT_BLK, PAIR_W), 1) // HEAD_DIM
    half_lanes = [jnp.where(lane_half == k, 1.0, 0.0).astype(BF16) for k in range(HEAD_PAIR)]
    causal = (lax.broadcasted_iota(jnp.int32, (ATT_BLK, ATT_BLK), 1)
              < lax.broadcasted_iota(jnp.int32, (ATT_BLK, ATT_BLK), 0))

    brows_pair = [jnp.concatenate([brows_ref[h], brows_ref[h]], axis=1) for h in range(HEADS)]

    def logits(j, h, row0, n_rows):
        kt_h = jnp.concatenate([kt_ref[j, HEAD_DIM * h:HEAD_DIM * (h + 1), :], brows_ref[h]], axis=0)
        return _dot(q_ext[h][row0:row0 + n_rows, :], kt_h)

    def values(j, g):
        vb = v_ref[pl.ds(pl.multiple_of(j * ATT_BLK, ATT_BLK), ATT_BLK), PAIR_W * g:PAIR_W * (g + 1)]
        return jnp.concatenate([vb * half_lanes[k] for k in range(HEAD_PAIR)], axis=0)

    pieces = []
    for kb in reversed(range(n_sub)):
        pieces.append((kb, kb * ATT_BLK, ATT_BLK, causal))
        if kb + 1 < n_sub:
            pieces.append((kb, (kb + 1) * ATT_BLK, (n_sub - kb - 1) * ATT_BLK, None))
    scores = [[_sb_scores(logits(n_old + kb, h, row0, n_rows), uo_ref[...], mask) for h in range(HEADS)]
              for kb, row0, n_rows, mask in pieces]
    for (kb, row0, n_rows, mask), piece_scores in zip(pieces, scores):
        for g in range(HEADS // HEAD_PAIR):
            ws = []
            for h in range(HEAD_PAIR * g, HEAD_PAIR * (g + 1)):
                d, tot = piece_scores[h]
                r = r_ref[h, row0:row0 + n_rows, :]
                ws.append(_sb_weights(d, r, mask).astype(BF16))
                r_ref[h, row0:row0 + n_rows, :] = r + tot
            acc_ref[g, row0:row0 + n_rows, :] += _dot(jnp.concatenate(ws, axis=1), values(n_old + kb, g))

    assert n_sub % 4 == 0
    def block_of(t):
        return jnp.maximum(n_old - 1 - t, 0)

    def older_blocks(g):
        heads = range(HEAD_PAIR * g, HEAD_PAIR * (g + 1))

        def stage1(p, z_slot):
            j0, j1 = block_of(2 * p), block_of(2 * p + 1)
            for k, h in enumerate(heads):
                rows_h = slice(HEAD_DIM * h, HEAD_DIM * (h + 1))
                kt_pair = jnp.concatenate([
                    jnp.concatenate([kt_ref[j0, rows_h, :], kt_ref[j1, rows_h, :]], axis=1),
                    brows_pair[h]], axis=0)
                z = _dot(q_ext[h], kt_pair)
                z_s[z_slot, 0, k] = z[:, :ATT_BLK]
                z_s[z_slot, 1, k] = z[:, ATT_BLK:]

        def stage2(z_slot, blk, d_slot):
            for k in range(HEAD_PAIR):
                d, tot = _sb_scores(z_s[z_slot, blk, k], uo_ref[...], None)
                d_s[d_slot, k] = d
                t_s[d_slot, k] = tot

        def stage3(t, d_slot):
            ws = []
            for k, h in enumerate(heads):
                r = r_ref[h]
                ws.append(_sb_weights(d_s[d_slot, k], r, None).astype(BF16))
                r_ref[h] = r + t_s[d_slot, k]
            acc_ref[g] += _dot(jnp.concatenate(ws, axis=1), values(block_of(t), g))

        stage1(0, 0)
        stage2(0, 0, 0)

        def four_blocks(v, carry):
            for half in range(2):
                p = 2 * v + half
                stage1(p + 1, 1 - half)
                stage2(half, 1, 1)
                stage3(2 * p, 0)
                stage2(1 - half, 0, 0)
                stage3(2 * p + 1, 1)
            return carry

        lax.fori_loop(0, n_old // 4, four_blocks, 0)

    for g in range(HEADS // HEAD_PAIR):
        older_blocks(g)
        o_ref[:, PAIR_W * g:PAIR_W * (g + 1)] = acc_ref[g]


def _attn_call(brows, uo, q, kt, vb):
    length = q.shape[0]
    return pl.pallas_call(
        _attn_kernel,
        out_shape=jax.ShapeDtypeStruct((length, BRANCH_W), F32),
        grid=(length // ATT_Q,),
        in_specs=[
            _const_spec(brows.shape), _const_spec(uo.shape),
            pl.BlockSpec((ATT_Q, BRANCH_W), lambda i: (i, 0)),
            _const_spec(kt.shape), _const_spec(vb.shape),
        ],
        out_specs=pl.BlockSpec((ATT_Q, BRANCH_W), lambda i: (i, 0)),
        scratch_shapes=[
            pltpu.VMEM((HEADS // HEAD_PAIR, ATT_Q, PAIR_W), F32),
            pltpu.VMEM((HEADS, ATT_Q, ATT_BLK), F32),
            pltpu.VMEM((2, 2, HEAD_PAIR, ATT_Q, ATT_BLK), F32),
            pltpu.VMEM((2, HEAD_PAIR, ATT_Q, ATT_BLK), F32),
            pltpu.VMEM((2, HEAD_PAIR, ATT_Q, ATT_BLK), F32),
        ],
        compiler_params=pltpu.CompilerParams(
            dimension_semantics=("arbitrary",), vmem_limit_bytes=VMEM_LIMIT),
        name="prompt_attn",
    )(brows, uo, q, kt, vb)


def _back_kernel(x_ref, ob_ref, macd_ref, gb_ref, g1_ref, n2_ref, sc2_ref, sh2_ref, g2_ref,
                 wb1_ref, wo_ref, w1_ref, w2_ref, y_ref):
    merged = macd_ref[...] + gb_ref[...] * _dot(ob_ref[...].astype(BF16), wb1_ref[...])
    x1 = x_ref[...] + g1_ref[...] * _dot(merged.astype(BF16), wo_ref[...])
    h2 = (_rms_rows(x1) * n2_ref[...] * (1.0 + sc2_ref[...]) + sh2_ref[...]).astype(BF16)
    hid = jnp.maximum(_dot(h2, w1_ref[...]), 0.0)
    y_ref[...] = x1 + g2_ref[...] * _dot((hid * hid).astype(BF16), w2_ref[...])


def _back_call(x, ob, macd, gb, g1, n2, sc2, sh2, g2, wts, tm):
    rows = x.shape[0]
    mod_rows = g1.shape[0]
    if mod_rows == 1:
        mod_spec = _const_spec((1, D_MODEL))
    else:
        mod_spec = pl.BlockSpec((tm, D_MODEL), lambda i: (i, 0))
    row_spec = lambda w: pl.BlockSpec((tm, w), lambda i: (i, 0))
    weights, weight_specs = _operands([wts['wb1'], wts['wo'], wts['w1'], wts['w2']])
    return pl.pallas_call(
        _back_kernel,
        out_shape=jax.ShapeDtypeStruct((rows, D_MODEL), F32),
        grid=(rows // tm,),
        in_specs=[row_spec(D_MODEL), row_spec(BRANCH_W), row_spec(D_MODEL), row_spec(D_MODEL),
                  mod_spec, _const_spec((1, D_MODEL)), mod_spec, mod_spec, mod_spec] + weight_specs,
        out_specs=row_spec(D_MODEL),
        compiler_params=pltpu.CompilerParams(
            dimension_semantics=("arbitrary",), vmem_limit_bytes=VMEM_LIMIT),
        name="layer_back",
    )(x, ob, macd, gb, g1, n2, sc2, sh2, g2, *weights)


def _sample_front_kernel(x_ref, g1_ref, sc_ref, sh_ref, win_ref, wbr_ref,
                         p_ref, a0_ref, convw_ref, hmask_ref, tdiag_ref, conv_ref, s0_ref,
                         q_ref, k_ref, v_ref, macd_ref, gb_ref, convnew_ref, snew_ref, av_ref):
    nb = x_ref.shape[0]
    wa_ref, wb_ref, wc_ref, wd_ref, wg_ref = _in_proj_views(win_ref)
    prm = p_ref[...]
    row = lambda i: prm[i:i + 1, :]
    hmask = hmask_ref[...]

    x = x_ref[...]
    h = (_rms_rows(x) * g1_ref[...] * (1.0 + sc_ref[...]) + sh_ref[...]).astype(BF16)

    ga = jax.nn.gelu(_dot(h, wa_ref[...]))
    vn = _layer_norm(ga[:, BRANCH_W:], row(P_ALN_G), row(P_ALN_B))
    av_ref[...] = vn
    out_a = ga[:, :BRANCH_W] * (a0_ref[0:1, :] * vn + a0_ref[1:2, :])

    zb = _dot(h, wb_ref[...])
    qn = _head_rms(zb[:, :BRANCH_W], hmask, row(P_QN_G))
    kn = _head_rms(zb[:, BRANCH_W:2 * BRANCH_W], hmask, row(P_KN_G))
    q_ref[...] = qn * QK_SCALE
    k_ref[...] = kn
    v_ref[...] = zb[:, 2 * BRANCH_W:]

    zc = _dot(h, wc_ref[...])
    cin = zc[:, :BRANCH_W] * _sigmoid(zc[:, BRANCH_W:])
    acc = row(P_CONV_B) + convw_ref[CONV_W - 1:CONV_W, :] * cin
    for j in range(CONV_W - 1):
        acc = acc + convw_ref[j:j + 1, :] * conv_ref[j]
        if j >= 1:
            convnew_ref[j - 1] = conv_ref[j]
    convnew_ref[CONV_W - 2] = cin
    out_c = _silu(_layer_norm(acc, row(P_CLN_G), row(P_CLN_B)))

    zd = _dot(h, wd_ref[...])
    qd = _silu(zd[:, :BRANCH_W]) * QK_SCALE
    logf = _log_forget(zd[:, BRANCH_W:2 * BRANCH_W], row(P_LB))
    ea = jnp.exp(logf)
    kd = _one_minus_forget(zd[:, BRANCH_W:2 * BRANCH_W], row(P_LB))
    di = zd[:, 2 * BRANCH_W:3 * BRANCH_W]
    tdiag = tdiag_ref[...]

    def spread(r):
        t3 = jnp.broadcast_to(r[:, None, :], (nb, HEAD_DIM, BRANCH_W)) * tdiag[None]
        flat = _dot_split_lhs(t3.reshape(nb * HEAD_DIM, BRANCH_W), hmask, 3)
        return flat.reshape(nb, HEAD_DIM, BRANCH_W)

    s0 = s0_ref[...]
    od = jnp.sum(spread(qd * ea) * s0, axis=1) + _dot_split_lhs(qd * kd, hmask, 3) * di
    snew_ref[...] = spread(ea) * s0 + spread(kd) * di[:, None, :]
    out_d = _head_rms(od, hmask, row(P_DN_G)) * _silu(zd[:, 3 * BRANCH_W:])

    macd, gb = _gated_local_merge(h, wg_ref, out_a, out_c, out_d, wbr_ref)
    macd_ref[...] = macd
    gb_ref[...] = gb


def _sample_front_call(x, g1, sc, sh, wts, consts, conv_t, s0_r):
    nb = x.shape[0]
    in_arrays, in_specs = _operands([x, g1, sc, sh, wts['win'], wts['wbr'],
                                     wts['ptab'], wts['a0'], wts['conv_w'], consts['hmask'], consts['tdiag'],
                                     conv_t, s0_r])
    out_shape = [
        jax.ShapeDtypeStruct((nb, BRANCH_W), F32),
        jax.ShapeDtypeStruct((nb, BRANCH_W), F32),
        jax.ShapeDtypeStruct((nb, BRANCH_W), F32),
        jax.ShapeDtypeStruct((nb, D_MODEL), F32),
        jax.ShapeDtypeStruct((nb, D_MODEL), F32),
        jax.ShapeDtypeStruct(conv_t.shape, F32),
        jax.ShapeDtypeStruct(s0_r.shape, F32),
        jax.ShapeDtypeStruct((nb, BRANCH_W), F32),
    ]
    return pl.pallas_call(
        _sample_front_kernel,
        out_shape=out_shape,
        grid=(1,),
        in_specs=in_specs,
        out_specs=[_const_spec(s.shape) for s in out_shape],
        compiler_params=pltpu.CompilerParams(vmem_limit_bytes=VMEM_LIMIT),
        name="sample_front",
    )(*in_arrays)


Q_ROWS = 16


def _paged_attn_kernel(pt_ref, q_ref, bias_ref, uo_ref, *refs):
    k_refs = refs[:PAGES_PER_STEP]
    v_refs = refs[PAGES_PER_STEP:2 * PAGES_PER_STEP]
    o_ref, acc_ref, r_ref = refs[2 * PAGES_PER_STEP:]
    jc = pl.program_id(1)

    @pl.when(jc == 0)
    def _():
        acc_ref[...] = jnp.zeros_like(acc_ref)
        r_ref[...] = jnp.zeros_like(r_ref)

    row_id = lax.broadcasted_iota(jnp.int32, (Q_ROWS, BRANCH_W), 0)
    own_head = row_id == lax.broadcasted_iota(jnp.int32, (Q_ROWS, BRANCH_W), 1) // HEAD_DIM
    qrows = jnp.where(own_head, jnp.broadcast_to(q_ref[...], (Q_ROWS, BRANCH_W)), 0.0).astype(BF16)
    as_matrix = lambda ref: ref[...].reshape(BRANCH_W, CHUNK).astype(BF16)
    pages = range(PAGES_PER_STEP)
    kt_all = jnp.concatenate([as_matrix(k_refs[p]) for p in pages], axis=1)
    z = _dot(qrows, kt_all)
    z_rows = jnp.concatenate(
        [z[:, p * CHUNK:(p + 1) * CHUNK] + bias_ref[...] for p in pages], axis=0)
    d, tot = _sb_scores(z_rows, uo_ref[...], None)
    r = r_ref[...]
    r_pages = [None] * PAGES_PER_STEP
    for p in reversed(pages):
        r_pages[p] = r
        r = r + tot[p * Q_ROWS:(p + 1) * Q_ROWS, :]
    r_ref[...] = r
    w = _sb_weights(d, jnp.concatenate(r_pages, axis=0), None).astype(BF16)
    w_all = jnp.concatenate([w[p * Q_ROWS:(p + 1) * Q_ROWS, :] for p in pages], axis=1)
    vt_all = jnp.concatenate([as_matrix(v_refs[p]) for p in pages], axis=1)
    acc = acc_ref[...] + _dot_nt(w_all, vt_all)
    acc_ref[...] = acc

    @pl.when(jc == pl.num_programs(1) - 1)
    def _():
        o_ref[...] = jnp.sum(jnp.where(own_head, acc, 0.0), axis=0, keepdims=True)


def _paged_attn_call(page_table, q, bias_rows, uo, cache_k, cache_v, layer):
    nb, n_pages = page_table.shape
    n_chunks = n_pages // PAGES_PER_STEP

    def page_spec(p):
        def index_map(b, jc, pt):
            return (pt[b, (n_chunks - 1 - jc) * PAGES_PER_STEP + p], layer, 0, 0, 0)
        return pl.BlockSpec((None, None, HEADS, HEAD_DIM, CHUNK), index_map)

    page_specs = [page_spec(p) for p in range(PAGES_PER_STEP)]
    grid_spec = pltpu.PrefetchScalarGridSpec(
        num_scalar_prefetch=1,
        grid=(nb, n_chunks),
        in_specs=[
            pl.BlockSpec((None, 1, BRANCH_W), lambda b, jc, pt: (b, 0, 0)),
            pl.BlockSpec(bias_rows.shape, lambda b, jc, pt: (0, 0)),
            pl.BlockSpec(uo.shape, lambda b, jc, pt: (0, 0)),
        ] + page_specs + page_specs,
        out_specs=pl.BlockSpec((None, 1, BRANCH_W), lambda b, jc, pt: (b, 0, 0)),
        scratch_shapes=[
            pltpu.VMEM((Q_ROWS, BRANCH_W), F32),
            pltpu.VMEM((Q_ROWS, CHUNK), F32),
        ],
    )
    return pl.pallas_call(
        _paged_attn_kernel,
        out_shape=jax.ShapeDtypeStruct((nb, 1, BRANCH_W), F32),
        grid_spec=grid_spec,
        compiler_params=pltpu.CompilerParams(
            dimension_semantics=("arbitrary", "arbitrary"), vmem_limit_bytes=VMEM_LIMIT),
        name="sample_paged_attn",
    )(page_table, q.reshape(nb, 1, BRANCH_W), bias_rows, uo,
      *([cache_k] * PAGES_PER_STEP), *([cache_v] * PAGES_PER_STEP))


def _constants():
    lane = jnp.arange(BRANCH_W)
    hmask = (lane[:, None] // HEAD_DIM == lane[None, :] // HEAD_DIM).astype(BF16)
    r = jnp.arange(ROW_TILE)
    tri = ((r[:, None] // SUB == r[None, :] // SUB) & (r[None, :] <= r[:, None])).astype(BF16)
    tdiag = (jnp.arange(HEAD_DIM)[:, None] == lane[None, :] % HEAD_DIM).astype(F32)
    j = jnp.arange(2 * ATT_BLK) % ATT_BLK
    c = jnp.arange(2 * ATT_BLK)
    uo = ((c[None, :] >= ATT_BLK) | (j[:, None] >= c[None, :])).astype(BF16)
    return {'hmask': hmask, 'tri': tri, 'tdiag': tdiag, 'uo': uo}


def _layer_weights(l, lb, big, a_ln_g, a_ln_b, a_ws, a_bs, b_qn_g, b_kn_g, c_conv_w, c_conv_b,
                   c_ln_g, c_ln_b, d_norm_g):
    tile4 = lambda g: jnp.tile(g, HEADS)
    rows = [a_ln_g[l], a_ln_b[l], tile4(b_qn_g[l]), tile4(b_kn_g[l]), c_conv_b[l], c_ln_g[l], c_ln_b[l],
            lb, tile4(d_norm_g[l])]
    ptab = jnp.zeros((16, BRANCH_W), F32).at[:len(rows)].set(jnp.stack(rows))
    grp = BRANCH_W // HEADS
    return {
        'win': _Stacked(big['w_in'], l),
        'wbr': _Stacked(big['w_branch'], l), 'wb1': _Stacked(big['w_branch'], l, 1), 'ptab': ptab,
        'a_ws': a_ws[l], 'a_bias': jnp.repeat(a_bs[l].T, grp, axis=1),
        'a0': jnp.stack([jnp.repeat(a_ws[l][:, 0, 0], grp), jnp.repeat(a_bs[l][:, 0], grp)]),
        'conv_w': jnp.zeros((32, BRANCH_W), F32).at[:CONV_W].set(c_conv_w[l]),
        'wo': _Stacked(big['w_out'], l), 'w1': _Stacked(big['w_ff1'], l), 'w2': _Stacked(big['w_ff2'], l),
    }


def kernel(x_prompt, x_sample, c_prompt, c_sample, cache_k, cache_v, state_conv, state_hgrn, page_table,
           ada_w, ada_b, norm1_g, norm2_g, w_in, a_ln_g, a_ln_b, a_ws, a_bs, b_qn_g, b_kn_g, b_bias,
           c_conv_w, c_conv_b, c_ln_g, c_ln_b, d_lb_logits, d_norm_g, w_branch, w_out, w_ff1, w_ff2):
    depth = w_in.shape[0]
    seq = x_prompt.shape[1]
    nb = x_sample.shape[0]
    n_pool = cache_k.shape[0]
    consts = _constants()

    cp = jnp.cumsum(jax.nn.softmax(d_lb_logits.astype(F32), axis=0), axis=0)
    lbs = cp - cp[0:1]

    ada_rows = 8 * ((1 + nb + 7) // 8)
    c_all = jnp.zeros((ada_rows, D_MODEL), F32).at[0:1].set(c_prompt).at[1:1 + nb].set(c_sample)
    mods = _ada_call(c_all, ada_w, ada_b)

    ck = cache_k.transpose(0, 1, 3, 4, 2)
    cv = cache_v.transpose(0, 1, 3, 4, 2)

    yp = x_prompt[0]
    ys = x_sample[:, 0]
    outs = {n: [] for n in ('ks', 'vs', 'cp', 'cs', 'sp', 'ss', 'av')}
    kv_all = None
    big = {'w_in': w_in.astype(BF16), 'w_branch': w_branch.astype(BF16), 'w_out': w_out.astype(BF16),
           'w_ff1': w_ff1.astype(BF16), 'w_ff2': w_ff2.astype(BF16)}
    for l in range(depth):
        wts = _layer_weights(l, lbs[l], big, a_ln_g, a_ln_b, a_ws, a_bs, b_qn_g, b_kn_g, c_conv_w,
                             c_conv_b, c_ln_g, c_ln_b, d_norm_g)
        mp = [mods[l, 0:1, i * D_MODEL:(i + 1) * D_MODEL] for i in range(6)]
        ms = [mods[l, 1:1 + nb, i * D_MODEL:(i + 1) * D_MODEL] for i in range(6)]
        n1 = norm1_g[l][None]
        n2 = norm2_g[l][None]
        bias_rows = jnp.zeros((Q_ROWS, ATT_BLK), F32).at[:HEADS].set(
            jnp.broadcast_to(b_bias[l][:, None], (HEADS, ATT_BLK)))
        b_hi = b_bias[l].astype(BF16)
        b_lo = (b_bias[l] - b_hi.astype(F32)).astype(BF16)
        brows = jnp.zeros((HEADS, HEAD_DIM, ATT_BLK), BF16)
        brows = brows.at[:, 0, :].set(b_hi[:, None]).at[:, 1, :].set(b_lo[:, None])

        q, k_all, v_all, kt, vb, macd, gb, conv_tail, st = _front_call(
            yp, n1, mp[1], mp[0], wts, consts, l, depth, kv_all)
        kv_all = (k_all, v_all)
        ob = _attn_call(brows, consts['uo'], q, kt, vb)
        yp = _back_call(yp, ob, macd, gb, mp[2], n2, mp[4], mp[3], mp[5], wts, BACK_TILE)
        outs['cp'].append(conv_tail[CONV_HALO - (CONV_W - 1):][None])
        st4 = st.reshape(HEADS, HEAD_DIM, HEADS, HEAD_DIM)
        outs['sp'].append(jnp.stack([st4[h, :, h, :].T for h in range(HEADS)])[None])

        conv_t = state_conv[:, l].transpose(1, 0, 2)
        s0_r = state_hgrn[:, l].transpose(0, 2, 1, 3).reshape(nb, HEAD_DIM, BRANCH_W)
        qs, ksn, vsn, macd_s, gb_s, conv_new, s_new, avn = _sample_front_call(
            ys, n1, ms[1], ms[0], wts, consts, conv_t, s0_r)
        obs = _paged_attn_call(page_table, qs, bias_rows, consts['uo'], ck, cv, l)
        ys = _back_call(ys, obs[:, 0], macd_s, gb_s, ms[2], n2, ms[4], ms[3], ms[5], wts, nb)
        outs['ks'].append(ksn.reshape(nb, 1, HEADS, HEAD_DIM))
        outs['vs'].append(vsn.reshape(nb, 1, HEADS, HEAD_DIM))
        outs['cs'].append(conv_new.transpose(1, 0, 2))
        outs['ss'].append(s_new.reshape(nb, HEAD_DIM, HEADS, HEAD_DIM).transpose(0, 2, 1, 3))
        outs['av'].append(avn[:, None, :])

    stack = lambda n: jnp.stack(outs[n], axis=1)
    rows_out = lambda a: a.reshape(1, depth, HEADS, HEAD_DIM, seq).transpose(0, 1, 4, 2, 3)
    return (yp[None], ys[:, None, :],
            rows_out(kv_all[0]), rows_out(kv_all[1]), stack('ks'), stack('vs'),
            stack('cp'), stack('cs'), stack('sp'), stack('ss'), stack('av'))
```

```python
import functools

import jax
import jax.numpy as jnp
from jax import lax
from jax.experimental import pallas as pl
from jax.experimental.pallas import tpu as pltpu

F32 = jnp.float32
BF16 = jnp.bfloat16

D_MODEL = 1024
BRANCH_W = 256
HEADS = 4
HEAD_DIM = 64
CHUNK = 128
CONV_W = 31
D_FF = 4096
EPS = 1e-6
QK_SCALE = HEAD_DIM ** -0.5
LOG2E = 1.4426950408889634
MASKED_EXPONENT = -1e30

ROW_TILE = 256
BACK_TILE = 512
SUB = 16
CONV_HALO = 32
CONV_SHIFTS = 8
CONV_BASE = CONV_HALO + CONV_SHIFTS
ATT_BLK = 128
ATT_Q = 512
HEAD_PAIR = 2
PAIR_W = HEAD_PAIR * HEAD_DIM
PAGES_PER_STEP = 32
VMEM_LIMIT = 56 * 1024 * 1024


def _dot(a, b):
    return jnp.dot(a, b, preferred_element_type=F32)


def _bf16_pieces(a, n):
    pieces, rem = [], a
    for _ in range(n):
        p = rem.astype(BF16)
        pieces.append(p)
        rem = rem - p.astype(F32)
    return pieces


def _dot_split_lhs(a, ones_bf, n):
    return _dot(jnp.concatenate(_bf16_pieces(a, n), axis=1), jnp.concatenate([ones_bf] * n, axis=0))


def _dot_split_rhs(ones_bf, b, n):
    return _dot(jnp.concatenate([ones_bf] * n, axis=1), jnp.concatenate(_bf16_pieces(b, n), axis=0))


def _dot_nt(a, b):
    return lax.dot_general(a, b, (((1,), (1,)), ((), ())), preferred_element_type=F32)


def _dot_tn(a, b):
    return lax.dot_general(a, b, (((0,), (0,)), ((), ())), preferred_element_type=F32)


def _sigmoid(x):
    return 0.5 * jnp.tanh(0.5 * x) + 0.5


def _silu(x):
    return x * _sigmoid(x)


def _softplus(x):
    return jnp.maximum(x, 0.0) + jnp.log1p(jnp.exp(-jnp.abs(x)))


def _rms_rows(x):
    return x * lax.rsqrt(jnp.mean(x * x, axis=-1, keepdims=True) + EPS)


def _layer_norm(x, g, b):
    mu = jnp.mean(x, axis=-1, keepdims=True)
    xc = x - mu
    var = jnp.mean(xc * xc, axis=-1, keepdims=True)
    return xc * lax.rsqrt(var + EPS) * g + b


def _head_rms(x, hmask, g):
    ms = _dot_split_lhs(x * x, hmask, 2) * (1.0 / HEAD_DIM)
    return x * lax.rsqrt(ms + EPS) * g


def _log_forget(df, lb):
    ls = -_softplus(-df)
    b = jnp.log1p(-lb) + ls
    pos = lb > 0.0
    a = jnp.log(jnp.where(pos, lb, 1.0))
    lae = jnp.maximum(a, b) + jnp.log1p(jnp.exp(-jnp.abs(a - b)))
    return jnp.where(pos, lae, b)


def _one_minus_forget(df, lb):
    return (1.0 - lb) * _sigmoid(-df)


def _const_spec(shape):
    nd = len(shape)
    return pl.BlockSpec(shape, lambda *_: (0,) * nd, pipeline_mode=pl.Buffered(1))


class _Stacked:
    def __init__(self, arr, *prefix):
        self.arr, self.prefix = arr, prefix


def _operands(items):
    arrays, specs = [], []
    for a in items:
        if isinstance(a, _Stacked):
            tail = a.arr.shape[len(a.prefix):]
            index = a.prefix + (0,) * len(tail)
            arrays.append(a.arr)
            specs.append(pl.BlockSpec((None,) * len(a.prefix) + tail, lambda *_, index=index: index,
                                      pipeline_mode=pl.Buffered(1)))
        else:
            arrays.append(a)
            specs.append(_const_spec(a.shape))
    return arrays, specs


def _ada_kernel(c_ref, w_ref, b_ref, o_ref):
    s = _silu(c_ref[...]).astype(BF16)
    o_ref[...] = _dot(s, w_ref[...].astype(BF16)) + b_ref[...]


def _ada_call(c_all, ada_w, ada_b):
    depth = ada_w.shape[0]
    rows = c_all.shape[0]
    n_col = ada_w.shape[2] // D_MODEL
    return pl.pallas_call(
        _ada_kernel,
        out_shape=jax.ShapeDtypeStruct((depth, rows, ada_w.shape[2]), F32),
        grid=(depth, n_col),
        in_specs=[
            pl.BlockSpec((rows, D_MODEL), lambda l, j: (0, 0)),
            pl.BlockSpec((None, D_MODEL, D_MODEL), lambda l, j: (l, 0, j)),
            pl.BlockSpec((None, 1, D_MODEL), lambda l, j: (l, 0, j)),
        ],
        out_specs=pl.BlockSpec((None, rows, D_MODEL), lambda l, j: (l, 0, j)),
        compiler_params=pltpu.CompilerParams(vmem_limit_bytes=VMEM_LIMIT),
        name="ada_params",
    )(c_all, ada_w, ada_b.reshape(depth, 1, -1))


P_ALN_G, P_ALN_B, P_QN_G, P_KN_G, P_CONV_B, P_CLN_G, P_CLN_B, P_LB, P_DN_G = range(9)


def _gated_local_merge(h, wg_ref, out_a, out_c, out_d, wbr_ref):
    gate = lambda k: _sigmoid(_dot(h, wg_ref[:, k * D_MODEL:(k + 1) * D_MODEL]))
    macd = gate(0) * _dot(out_a.astype(BF16), wbr_ref[0])
    macd = macd + gate(2) * _dot(out_c.astype(BF16), wbr_ref[2])
    macd = macd + gate(3) * _dot(out_d.astype(BF16), wbr_ref[3])
    return macd, gate(1)


def _in_proj_views(win_ref):
    bounds, start = [], 0
    for width in (2 * BRANCH_W, 3 * BRANCH_W, 2 * BRANCH_W, 4 * BRANCH_W, HEADS * D_MODEL):
        bounds.append((start, start + width))
        start += width
    return [win_ref.at[:, lo:hi] for lo, hi in bounds]


def _front_kernel(n_aliased, *refs):
    (x_ref, g1_ref, sc_ref, sh_ref, win_ref, wbr_ref,
     p_ref, aws_ref, abias_ref, convw_ref, hmask_ref, tri_ref,
     q_ref, ktf_ref, vtf_ref, kt_ref, vb_ref, macd_ref, gb_ref, convnew_ref, state_ref,
     convbuf, st_ref) = refs[n_aliased:]
    tm = x_ref.shape[0]
    step = pl.program_id(0)
    wa_ref, wb_ref, wc_ref, wd_ref, wg_ref = _in_proj_views(win_ref)

    @pl.when(step == 0)
    def _():
        convbuf[...] = jnp.zeros_like(convbuf)
        st_ref[...] = jnp.zeros_like(st_ref)

    prm = p_ref[...]
    row = lambda i: prm[i:i + 1, :]
    hmask = hmask_ref[...]
    hmask_f32 = hmask.astype(F32)

    x = x_ref[...]
    h = (_rms_rows(x) * g1_ref[...] * (1.0 + sc_ref[...]) + sh_ref[...]).astype(BF16)

    gate_cols = {}

    def gate_slices(first, count):
        for s in range(first, first + count):
            gate_cols[s] = _sigmoid(_dot(h, wg_ref[:, s * BRANCH_W:(s + 1) * BRANCH_W]))

    za = _dot(h, wa_ref[...])
    zb = _dot(h, wb_ref[...])

    ga = jax.nn.gelu(za)
    u = ga[:, :BRANCH_W]
    vn = _layer_norm(ga[:, BRANCH_W:], row(P_ALN_G), row(P_ALN_B))
    zc = _dot(h, wc_ref[...])
    zd = _dot(h, wd_ref[...])
    lane_grp = lax.broadcasted_iota(jnp.int32, (CHUNK, BRANCH_W), 1) // (BRANCH_W // HEADS)
    tril = (lax.broadcasted_iota(jnp.int32, (CHUNK, CHUNK), 1)
            <= lax.broadcasted_iota(jnp.int32, (CHUNK, CHUNK), 0))
    w_causal = [jnp.where(tril, aws_ref[g], 0.0).astype(BF16) for g in range(HEADS)]
    mixed_chunks = []
    for c in range(tm // CHUNK):
        vc = vn[c * CHUNK:(c + 1) * CHUNK, :].astype(BF16)
        mixed = None
        for g in range(HEADS):
            mg = _dot(w_causal[g], vc)
            mixed = mg if mixed is None else jnp.where(lane_grp == g, mg, mixed)
        mixed_chunks.append(mixed + abias_ref[...])
    out_a = u * jnp.concatenate(mixed_chunks, axis=0)

    gate_slices(0, 2)

    qn = _head_rms(zb[:, :BRANCH_W], hmask, row(P_QN_G))
    kn = _head_rms(zb[:, BRANCH_W:2 * BRANCH_W], hmask, row(P_KN_G))
    vv = zb[:, 2 * BRANCH_W:]
    q_ref[...] = (qn * QK_SCALE).astype(BF16)
    knt_f32 = kn.T
    ktf_ref[...] = knt_f32
    vtf_ref[...] = vv.T
    vb_ref[...] = vv.astype(BF16)
    knt = knt_f32.astype(BF16)
    for c in range(tm // ATT_BLK):
        kt_ref[c] = knt[:, c * ATT_BLK:(c + 1) * ATT_BLK]

    gate_slices(2, 3)

    cin = zc[:, :BRANCH_W] * _sigmoid(zc[:, BRANCH_W:])
    for r in range(CONV_SHIFTS):
        convbuf[r, CONV_BASE - r:CONV_BASE - r + tm, :] = cin
    acc = jnp.broadcast_to(row(P_CONV_B), (tm, BRANCH_W))
    for j in range(CONV_W):
        lead = CONV_BASE - (CONV_W - 1) + j
        r = lead % CONV_SHIFTS
        acc = acc + convw_ref[j:j + 1, :] * convbuf[r, lead - r:lead - r + tm, :]
    out_c = _silu(_layer_norm(acc, row(P_CLN_G), row(P_CLN_B)))
    convnew_ref[...] = convbuf[0, tm + CONV_BASE - CONV_HALO:tm + CONV_BASE, :]
    for r in range(CONV_SHIFTS):
        convbuf[r, 0:CONV_BASE, :] = convbuf[r, tm:tm + CONV_BASE, :]

    gate_slices(5, 3)

    n_c = tm // SUB
    in_chunks = lambda a: a.reshape(n_c, SUB, BRANCH_W)
    qd = _silu(zd[:, :BRANCH_W]) * QK_SCALE
    logf = _log_forget(zd[:, BRANCH_W:2 * BRANCH_W], row(P_LB))
    kd = _one_minus_forget(zd[:, BRANCH_W:2 * BRANCH_W], row(P_LB))
    di = zd[:, 2 * BRANCH_W:3 * BRANCH_W]
    b = _dot_split_rhs(tri_ref[...], logf, 3)
    b3, qd3, kd3, di3 = in_chunks(b), in_chunks(qd), in_chunks(kd), in_chunks(di)
    bl3 = b3[:, SUB - 1:SUB, :]
    qt = (qd * jnp.exp(b)).astype(BF16)
    kt = (kd3 * jnp.exp(bl3 - b3)).reshape(tm, BRANCH_W).astype(BF16)
    el = jnp.exp(bl3)
    di_bf = di.astype(BF16)
    increments = [_dot_tn(di_bf[c * SUB:(c + 1) * SUB], kt[c * SUB:(c + 1) * SUB]) * hmask_f32
                  for c in range(n_c)]

    t_idx = lax.broadcasted_iota(jnp.int32, (n_c, SUB, BRANCH_W), 1)
    od = jnp.zeros((tm, BRANCH_W), F32)
    n_gate_cols = HEADS * D_MODEL // BRANCH_W
    for s in range(SUB):
        e = jnp.exp(jnp.where(t_idx >= s, b3 - b3[:, s:s + 1, :], MASKED_EXPONENT))
        wgt = e * qd3 * kd3[:, s:s + 1, :]
        coef = _dot(wgt.reshape(tm, BRANCH_W).astype(BF16), hmask)
        i_s = jnp.broadcast_to(di3[:, s:s + 1, :], (n_c, SUB, BRANCH_W)).reshape(tm, BRANCH_W)
        od = od + coef * i_s
        if s % 2 == 0 and 8 + s // 2 < n_gate_cols:
            gate_slices(8 + s // 2, 1)
    assert len(gate_cols) == n_gate_cols
    per_gate = D_MODEL // BRANCH_W
    gate = lambda k: jnp.concatenate([gate_cols[c] for c in range(k * per_gate, (k + 1) * per_gate)], axis=1)
    gate_a, gate_c, gate_d = gate(0), gate(2), gate(3)
    gb_ref[...] = gate(1)

    st = st_ref[...]
    o_inter = []
    for c in range(n_c):
        o_inter.append(_dot_nt(qt[c * SUB:(c + 1) * SUB], st.astype(BF16)))
        st = st * el[c] + increments[c]
    st_ref[...] = st
    state_ref[...] = st
    od = od + jnp.concatenate(o_inter, axis=0)
    out_d = _head_rms(od, hmask, row(P_DN_G)) * _silu(zd[:, 3 * BRANCH_W:])

    macd = gate_a * _dot(out_a.astype(BF16), wbr_ref[0])
    macd = macd + gate_c * _dot(out_c.astype(BF16), wbr_ref[2])
    macd_ref[...] = macd + gate_d * _dot(out_d.astype(BF16), wbr_ref[3])


def _front_call(x, g1, sc, sh, wts, consts, layer, depth, kv_prev):
    length = x.shape[0]
    tm = ROW_TILE
    n_steps = length // tm
    row_spec = lambda w: pl.BlockSpec((tm, w), lambda i: (i, 0))
    const_arrays, const_specs = _operands([g1, sc, sh, wts['win'], wts['wbr'],
                                           wts['ptab'], wts['a_ws'], wts['a_bias'], wts['conv_w'],
                                           consts['hmask'], consts['tri']])
    aliased = [] if kv_prev is None else list(kv_prev)
    in_arrays = aliased + [x] + const_arrays
    in_specs = [pl.BlockSpec(memory_space=pl.ANY)] * len(aliased) + [row_spec(D_MODEL)] + const_specs
    kv_spec = pl.BlockSpec((None, BRANCH_W, tm), lambda i: (layer, 0, i))
    out_shape = [
        jax.ShapeDtypeStruct((length, BRANCH_W), BF16),
        jax.ShapeDtypeStruct((depth, BRANCH_W, length), F32),
        jax.ShapeDtypeStruct((depth, BRANCH_W, length), F32),
        jax.ShapeDtypeStruct((length // ATT_BLK, BRANCH_W, ATT_BLK), BF16),
        jax.ShapeDtypeStruct((length, BRANCH_W), BF16),
        jax.ShapeDtypeStruct((length, D_MODEL), F32),
        jax.ShapeDtypeStruct((length, D_MODEL), F32),
        jax.ShapeDtypeStruct((CONV_HALO, BRANCH_W), F32),
        jax.ShapeDtypeStruct((BRANCH_W, BRANCH_W), F32),
    ]
    out_specs = [
        row_spec(BRANCH_W), kv_spec, kv_spec,
        pl.BlockSpec((tm // ATT_BLK, BRANCH_W, ATT_BLK), lambda i: (i, 0, 0)),
        row_spec(BRANCH_W), row_spec(D_MODEL), row_spec(D_MODEL),
        _const_spec((CONV_HALO, BRANCH_W)), _const_spec((BRANCH_W, BRANCH_W)),
    ]
    scratch = [
        pltpu.VMEM((CONV_SHIFTS, tm + CONV_BASE, BRANCH_W), F32),
        pltpu.VMEM((BRANCH_W, BRANCH_W), F32),
    ]
    return pl.pallas_call(
        functools.partial(_front_kernel, len(aliased)),
        out_shape=out_shape,
        grid=(n_steps,),
        in_specs=in_specs,
        out_specs=out_specs,
        scratch_shapes=scratch,
        input_output_aliases={n: 1 + n for n in range(len(aliased))},
        compiler_params=pltpu.CompilerParams(
            dimension_semantics=("arbitrary",), vmem_limit_bytes=VMEM_LIMIT),
        name="prompt_front",
    )(*in_arrays)


def _sb_scores(z, uo, causal):
    sp = jnp.maximum(z, 0.0) + jnp.log(1.0 + jnp.exp2(jnp.abs(z) * (-LOG2E)))
    if causal is not None:
        sp = jnp.where(causal, sp, 0.0)
    hi = sp.astype(BF16)
    lo = (sp - hi.astype(F32)).astype(BF16)
    cs = _dot(jnp.concatenate([hi, lo], axis=1), uo)
    return z - cs[:, :ATT_BLK], cs[:, ATT_BLK:]


def _sb_weights(d, r, causal):
    w = jnp.exp(d - r)
    if causal is not None:
        w = jnp.where(causal, w, 0.0)
    return w


def _attn_kernel(brows_ref, uo_ref, q_ref, kt_ref, v_ref, o_ref, acc_ref, r_ref, z_s, d_s, t_s):
    i = pl.program_id(0)
    n_sub = ATT_Q // ATT_BLK
    n_old = i * n_sub
    q = q_ref[...]
    unit_cols = jnp.where(lax.broadcasted_iota(jnp.int32, (ATT_Q, HEAD_DIM), 1) < 2, 1.0, 0.0).astype(BF16)
    q_ext = [jnp.concatenate([q[:, HEAD_DIM * h:HEAD_DIM * (h + 1)], unit_cols], axis=1)
             for h in range(HEADS)]
    acc_ref[...] = jnp.zeros_like(acc_ref)
    r_ref[...] = jnp.zeros_like(r_ref)
    lane_half = lax.broadcasted_iota(jnp.int32, (ATT_BLK, PAIR_W), 1) // HEAD_DIM
    half_lanes = [jnp.where(lane_half == k, 1.0, 0.0).astype(BF16) for k in range(HEAD_PAIR)]
    causal = (lax.broadcasted_iota(jnp.int32, (ATT_BLK, ATT_BLK), 1)
              < lax.broadcasted_iota(jnp.int32, (ATT_BLK, ATT_BLK), 0))

    brows_pair = [jnp.concatenate([brows_ref[h], brows_ref[h]], axis=1) for h in range(HEADS)]

    def logits(j, h, row0, n_rows):
        kt_h = jnp.concatenate([kt_ref[j, HEAD_DIM * h:HEAD_DIM * (h + 1), :], brows_ref[h]], axis=0)
        return _dot(q_ext[h][row0:row0 + n_rows, :], kt_h)

    def values(j, g):
        vb = v_ref[pl.ds(pl.multiple_of(j * ATT_BLK, ATT_BLK), ATT_BLK), PAIR_W * g:PAIR_W * (g + 1)]
        return jnp.concatenate([vb * half_lanes[k] for k in range(HEAD_PAIR)], axis=0)

    pieces = []
    for kb in reversed(range(n_sub)):
        pieces.append((kb, kb * ATT_BLK, ATT_BLK, causal))
        if kb + 1 < n_sub:
            pieces.append((kb, (kb + 1) * ATT_BLK, (n_sub - kb - 1) * ATT_BLK, None))
    scores = [[_sb_scores(logits(n_old + kb, h, row0, n_rows), uo_ref[...], mask) for h in range(HEADS)]
              for kb, row0, n_rows, mask in pieces]
    for (kb, row0, n_rows, mask), piece_scores in zip(pieces, scores):
        for g in range(HEADS // HEAD_PAIR):
            ws = []
            for h in range(HEAD_PAIR * g, HEAD_PAIR * (g + 1)):
                d, tot = piece_scores[h]
                r = r_ref[h, row0:row0 + n_rows, :]
                ws.append(_sb_weights(d, r, mask).astype(BF16))
                r_ref[h, row0:row0 + n_rows, :] = r + tot
            acc_ref[g, row0:row0 + n_rows, :] += _dot(jnp.concatenate(ws, axis=1), values(n_old + kb, g))

    assert n_sub % 4 == 0
    def block_of(t):
        return jnp.maximum(n_old - 1 - t, 0)

    def older_blocks(g):
        heads = range(HEAD_PAIR * g, HEAD_PAIR * (g + 1))

        def stage1(p, z_slot):
            j0, j1 = block_of(2 * p), block_of(2 * p + 1)
            for k, h in enumerate(heads):
                rows_h = slice(HEAD_DIM * h, HEAD_DIM * (h + 1))
                kt_pair = jnp.concatenate([
                    jnp.concatenate([kt_ref[j0, rows_h, :], kt_ref[j1, rows_h, :]], axis=1),
                    brows_pair[h]], axis=0)
                z = _dot(q_ext[h], kt_pair)
                z_s[z_slot, 0, k] = z[:, :ATT_BLK]
                z_s[z_slot, 1, k] = z[:, ATT_BLK:]

        def stage2(z_slot, blk, d_slot):
            for k in range(HEAD_PAIR):
                d, tot = _sb_scores(z_s[z_slot, blk, k], uo_ref[...], None)
                d_s[d_slot, k] = d
                t_s[d_slot, k] = tot

        def stage3(t, d_slot):
            ws = []
            for k, h in enumerate(heads):
                r = r_ref[h]
                ws.append(_sb_weights(d_s[d_slot, k], r, None).astype(BF16))
                r_ref[h] = r + t_s[d_slot, k]
            acc_ref[g] += _dot(jnp.concatenate(ws, axis=1), values(block_of(t), g))

        stage1(0, 0)
        stage2(0, 0, 0)

        def four_blocks(v, carry):
            for half in range(2):
                p = 2 * v + half
                stage1(p + 1, 1 - half)
                stage2(half, 1, 1)
                stage3(2 * p, 0)
                stage2(1 - half, 0, 0)
                stage3(2 * p + 1, 1)
            return carry

        lax.fori_loop(0, n_old // 4, four_blocks, 0)

    for g in range(HEADS // HEAD_PAIR):
        older_blocks(g)
        o_ref[:, PAIR_W * g:PAIR_W * (g + 1)] = acc_ref[g]


def _attn_call(brows, uo, q, kt, vb):
    length = q.shape[0]
    return pl.pallas_call(
        _attn_kernel,
        out_shape=jax.ShapeDtypeStruct((length, BRANCH_W), F32),
        grid=(length // ATT_Q,),
        in_specs=[
            _const_spec(brows.shape), _const_spec(uo.shape),
            pl.BlockSpec((ATT_Q, BRANCH_W), lambda i: (i, 0)),
            _const_spec(kt.shape), _const_spec(vb.shape),
        ],
        out_specs=pl.BlockSpec((ATT_Q, BRANCH_W), lambda i: (i, 0)),
        scratch_shapes=[
            pltpu.VMEM((HEADS // HEAD_PAIR, ATT_Q, PAIR_W), F32),
            pltpu.VMEM((HEADS, ATT_Q, ATT_BLK), F32),
            pltpu.VMEM((2, 2, HEAD_PAIR, ATT_Q, ATT_BLK), F32),
            pltpu.VMEM((2, HEAD_PAIR, ATT_Q, ATT_BLK), F32),
            pltpu.VMEM((2, HEAD_PAIR, ATT_Q, ATT_BLK), F32),
        ],
        compiler_params=pltpu.CompilerParams(
            dimension_semantics=("arbitrary",), vmem_limit_bytes=VMEM_LIMIT),
        name="prompt_attn",
    )(brows, uo, q, kt, vb)


def _back_kernel(x_ref, ob_ref, macd_ref, gb_ref, g1_ref, n2_ref, sc2_ref, sh2_ref, g2_ref,
                 wb1_ref, wo_ref, w1_ref, w2_ref, y_ref):
    merged = macd_ref[...] + gb_ref[...] * _dot(ob_ref[...].astype(BF16), wb1_ref[...])
    x1 = x_ref[...] + g1_ref[...] * _dot(merged.astype(BF16), wo_ref[...])
    h2 = (_rms_rows(x1) * n2_ref[...] * (1.0 + sc2_ref[...]) + sh2_ref[...]).astype(BF16)
    hid = jnp.maximum(_dot(h2, w1_ref[...]), 0.0)
    y_ref[...] = x1 + g2_ref[...] * _dot((hid * hid).astype(BF16), w2_ref[...])


def _back_call(x, ob, macd, gb, g1, n2, sc2, sh2, g2, wts, tm):
    rows = x.shape[0]
    mod_rows = g1.shape[0]
    if mod_rows == 1:
        mod_spec = _const_spec((1, D_MODEL))
    else:
        mod_spec = pl.BlockSpec((tm, D_MODEL), lambda i: (i, 0))
    row_spec = lambda w: pl.BlockSpec((tm, w), lambda i: (i, 0))
    weights, weight_specs = _operands([wts['wb1'], wts['wo'], wts['w1'], wts['w2']])
    return pl.pallas_call(
        _back_kernel,
        out_shape=jax.ShapeDtypeStruct((rows, D_MODEL), F32),
        grid=(rows // tm,),
        in_specs=[row_spec(D_MODEL), row_spec(BRANCH_W), row_spec(D_MODEL), row_spec(D_MODEL),
                  mod_spec, _const_spec((1, D_MODEL)), mod_spec, mod_spec, mod_spec] + weight_specs,
        out_specs=row_spec(D_MODEL),
        compiler_params=pltpu.CompilerParams(
            dimension_semantics=("arbitrary",), vmem_limit_bytes=VMEM_LIMIT),
        name="layer_back",
    )(x, ob, macd, gb, g1, n2, sc2, sh2, g2, *weights)


def _sample_front_kernel(x_ref, g1_ref, sc_ref, sh_ref, win_ref, wbr_ref,
                         p_ref, a0_ref, convw_ref, hmask_ref, tdiag_ref, conv_ref, s0_ref,
                         q_ref, k_ref, v_ref, macd_ref, gb_ref, convnew_ref, snew_ref, av_ref):
    nb = x_ref.shape[0]
    wa_ref, wb_ref, wc_ref, wd_ref, wg_ref = _in_proj_views(win_ref)
    prm = p_ref[...]
    row = lambda i: prm[i:i + 1, :]
    hmask = hmask_ref[...]

    x = x_ref[...]
    h = (_rms_rows(x) * g1_ref[...] * (1.0 + sc_ref[...]) + sh_ref[...]).astype(BF16)

    ga = jax.nn.gelu(_dot(h, wa_ref[...]))
    vn = _layer_norm(ga[:, BRANCH_W:], row(P_ALN_G), row(P_ALN_B))
    av_ref[...] = vn
    out_a = ga[:, :BRANCH_W] * (a0_ref[0:1, :] * vn + a0_ref[1:2, :])

    zb = _dot(h, wb_ref[...])
    qn = _head_rms(zb[:, :BRANCH_W], hmask, row(P_QN_G))
    kn = _head_rms(zb[:, BRANCH_W:2 * BRANCH_W], hmask, row(P_KN_G))
    q_ref[...] = qn * QK_SCALE
    k_ref[...] = kn
    v_ref[...] = zb[:, 2 * BRANCH_W:]

    zc = _dot(h, wc_ref[...])
    cin = zc[:, :BRANCH_W] * _sigmoid(zc[:, BRANCH_W:])
    acc = row(P_CONV_B) + convw_ref[CONV_W - 1:CONV_W, :] * cin
    for j in range(CONV_W - 1):
        acc = acc + convw_ref[j:j + 1, :] * conv_ref[j]
        if j >= 1:
            convnew_ref[j - 1] = conv_ref[j]
    convnew_ref[CONV_W - 2] = cin
    out_c = _silu(_layer_norm(acc, row(P_CLN_G), row(P_CLN_B)))

    zd = _dot(h, wd_ref[...])
    qd = _silu(zd[:, :BRANCH_W]) * QK_SCALE
    logf = _log_forget(zd[:, BRANCH_W:2 * BRANCH_W], row(P_LB))
    ea = jnp.exp(logf)
    kd = _one_minus_forget(zd[:, BRANCH_W:2 * BRANCH_W], row(P_LB))
    di = zd[:, 2 * BRANCH_W:3 * BRANCH_W]
    tdiag = tdiag_ref[...]

    def spread(r):
        t3 = jnp.broadcast_to(r[:, None, :], (nb, HEAD_DIM, BRANCH_W)) * tdiag[None]
        flat = _dot_split_lhs(t3.reshape(nb * HEAD_DIM, BRANCH_W), hmask, 3)
        return flat.reshape(nb, HEAD_DIM, BRANCH_W)

    s0 = s0_ref[...]
    od = jnp.sum(spread(qd * ea) * s0, axis=1) + _dot_split_lhs(qd * kd, hmask, 3) * di
    snew_ref[...] = spread(ea) * s0 + spread(kd) * di[:, None, :]
    out_d = _head_rms(od, hmask, row(P_DN_G)) * _silu(zd[:, 3 * BRANCH_W:])

    macd, gb = _gated_local_merge(h, wg_ref, out_a, out_c, out_d, wbr_ref)
    macd_ref[...] = macd
    gb_ref[...] = gb


def _sample_front_call(x, g1, sc, sh, wts, consts, conv_t, s0_r):
    nb = x.shape[0]
    in_arrays, in_specs = _operands([x, g1, sc, sh, wts['win'], wts['wbr'],
                                     wts['ptab'], wts['a0'], wts['conv_w'], consts['hmask'], consts['tdiag'],
                                     conv_t, s0_r])
    out_shape = [
        jax.ShapeDtypeStruct((nb, BRANCH_W), F32),
        jax.ShapeDtypeStruct((nb, BRANCH_W), F32),
        jax.ShapeDtypeStruct((nb, BRANCH_W), F32),
        jax.ShapeDtypeStruct((nb, D_MODEL), F32),
        jax.ShapeDtypeStruct((nb, D_MODEL), F32),
        jax.ShapeDtypeStruct(conv_t.shape, F32),
        jax.ShapeDtypeStruct(s0_r.shape, F32),
        jax.ShapeDtypeStruct((nb, BRANCH_W), F32),
    ]
    return pl.pallas_call(
        _sample_front_kernel,
        out_shape=out_shape,
        grid=(1,),
        in_specs=in_specs,
        out_specs=[_const_spec(s.shape) for s in out_shape],
        compiler_params=pltpu.CompilerParams(vmem_limit_bytes=VMEM_LIMIT),
        name="sample_front",
    )(*in_arrays)


Q_ROWS = 16


def _paged_attn_kernel(pt_ref, q_ref, bias_ref, uo_ref, *refs):
    k_refs = refs[:PAGES_PER_STEP]
    v_refs = refs[PAGES_PER_STEP:2 * PAGES_PER_STEP]
    o_ref, acc_ref, r_ref = refs[2 * PAGES_PER_STEP:]
    jc = pl.program_id(1)

    @pl.when(jc == 0)
    def _():
        acc_ref[...] = jnp.zeros_like(acc_ref)
        r_ref[...] = jnp.zeros_like(r_ref)

    row_id = lax.broadcasted_iota(jnp.int32, (Q_ROWS, BRANCH_W), 0)
    own_head = row_id == lax.broadcasted_iota(jnp.int32, (Q_ROWS, BRANCH_W), 1) // HEAD_DIM
    qrows = jnp.where(own_head, jnp.broadcast_to(q_ref[...], (Q_ROWS, BRANCH_W)), 0.0).astype(BF16)
    as_matrix = lambda ref: ref[...].reshape(BRANCH_W, CHUNK).astype(BF16)
    pages = range(PAGES_PER_STEP)
    kt_all = jnp.concatenate([as_matrix(k_refs[p]) for p in pages], axis=1)
    z = _dot(qrows, kt_all)
    z_rows = jnp.concatenate(
        [z[:, p * CHUNK:(p + 1) * CHUNK] + bias_ref[...] for p in pages], axis=0)
    d, tot = _sb_scores(z_rows, uo_ref[...], None)
    r = r_ref[...]
    r_pages = [None] * PAGES_PER_STEP
    for p in reversed(pages):
        r_pages[p] = r
        r = r + tot[p * Q_ROWS:(p + 1) * Q_ROWS, :]
    r_ref[...] = r
    w = _sb_weights(d, jnp.concatenate(r_pages, axis=0), None).astype(BF16)
    w_all = jnp.concatenate([w[p * Q_ROWS:(p + 1) * Q_ROWS, :] for p in pages], axis=1)
    vt_all = jnp.concatenate([as_matrix(v_refs[p]) for p in pages], axis=1)
    acc = acc_ref[...] + _dot_nt(w_all, vt_all)
    acc_ref[...] = acc

    @pl.when(jc == pl.num_programs(1) - 1)
    def _():
        o_ref[...] = jnp.sum(jnp.where(own_head, acc, 0.0), axis=0, keepdims=True)


def _paged_attn_call(page_table, q, bias_rows, uo, cache_k, cache_v, layer):
    nb, n_pages = page_table.shape
    n_chunks = n_pages // PAGES_PER_STEP

    def page_spec(p):
        def index_map(b, jc, pt):
            return (pt[b, (n_chunks - 1 - jc) * PAGES_PER_STEP + p], layer, 0, 0, 0)
        return pl.BlockSpec((None, None, HEADS, HEAD_DIM, CHUNK), index_map)

    page_specs = [page_spec(p) for p in range(PAGES_PER_STEP)]
    grid_spec = pltpu.PrefetchScalarGridSpec(
        num_scalar_prefetch=1,
        grid=(nb, n_chunks),
        in_specs=[
            pl.BlockSpec((None, 1, BRANCH_W), lambda b, jc, pt: (b, 0, 0)),
            pl.BlockSpec(bias_rows.shape, lambda b, jc, pt: (0, 0)),
            pl.BlockSpec(uo.shape, lambda b, jc, pt: (0, 0)),
        ] + page_specs + page_specs,
        out_specs=pl.BlockSpec((None, 1, BRANCH_W), lambda b, jc, pt: (b, 0, 0)),
        scratch_shapes=[
            pltpu.VMEM((Q_ROWS, BRANCH_W), F32),
            pltpu.VMEM((Q_ROWS, CHUNK), F32),
        ],
    )
    return pl.pallas_call(
        _paged_attn_kernel,
        out_shape=jax.ShapeDtypeStruct((nb, 1, BRANCH_W), F32),
        grid_spec=grid_spec,
        compiler_params=pltpu.CompilerParams(
            dimension_semantics=("arbitrary", "arbitrary"), vmem_limit_bytes=VMEM_LIMIT),
        name="sample_paged_attn",
    )(page_table, q.reshape(nb, 1, BRANCH_W), bias_rows, uo,
      *([cache_k] * PAGES_PER_STEP), *([cache_v] * PAGES_PER_STEP))


def _constants():
    lane = jnp.arange(BRANCH_W)
    hmask = (lane[:, None] // HEAD_DIM == lane[None, :] // HEAD_DIM).astype(BF16)
    r = jnp.arange(ROW_TILE)
    tri = ((r[:, None] // SUB == r[None, :] // SUB) & (r[None, :] <= r[:, None])).astype(BF16)
    tdiag = (jnp.arange(HEAD_DIM)[:, None] == lane[None, :] % HEAD_DIM).astype(F32)
    j = jnp.arange(2 * ATT_BLK) % ATT_BLK
    c = jnp.arange(2 * ATT_BLK)
    uo = ((c[None, :] >= ATT_BLK) | (j[:, None] >= c[None, :])).astype(BF16)
    return {'hmask': hmask, 'tri': tri, 'tdiag': tdiag, 'uo': uo}


def _layer_weights(l, lb, big, a_ln_g, a_ln_b, a_ws, a_bs, b_qn_g, b_kn_g, c_conv_w, c_conv_b,
                   c_ln_g, c_ln_b, d_norm_g):
    tile4 = lambda g: jnp.tile(g, HEADS)
    rows = [a_ln_g[l], a_ln_b[l], tile4(b_qn_g[l]), tile4(b_kn_g[l]), c_conv_b[l], c_ln_g[l], c_ln_b[l],
            lb, tile4(d_norm_g[l])]
    ptab = jnp.zeros((16, BRANCH_W), F32).at[:len(rows)].set(jnp.stack(rows))
    grp = BRANCH_W // HEADS
    return {
        'win': _Stacked(big['w_in'], l),
        'wbr': _Stacked(big['w_branch'], l), 'wb1': _Stacked(big['w_branch'], l, 1), 'ptab': ptab,
        'a_ws': a_ws[l], 'a_bias': jnp.repeat(a_bs[l].T, grp, axis=1),
        'a0': jnp.stack([jnp.repeat(a_ws[l][:, 0, 0], grp), jnp.repeat(a_bs[l][:, 0], grp)]),
        'conv_w': jnp.zeros((32, BRANCH_W), F32).at[:CONV_W].set(c_conv_w[l]),
        'wo': _Stacked(big['w_out'], l), 'w1': _Stacked(big['w_ff1'], l), 'w2': _Stacked(big['w_ff2'], l),
    }


def kernel(x_prompt, x_sample, c_prompt, c_sample, cache_k, cache_v, state_conv, state_hgrn, page_table,
           ada_w, ada_b, norm1_g, norm2_g, w_in, a_ln_g, a_ln_b, a_ws, a_bs, b_qn_g, b_kn_g, b_bias,
           c_conv_w, c_conv_b, c_ln_g, c_ln_b, d_lb_logits, d_norm_g, w_branch, w_out, w_ff1, w_ff2):
    depth = w_in.shape[0]
    seq = x_prompt.shape[1]
    nb = x_sample.shape[0]
    n_pool = cache_k.shape[0]
    consts = _constants()

    cp = jnp.cumsum(jax.nn.softmax(d_lb_logits.astype(F32), axis=0), axis=0)
    lbs = cp - cp[0:1]

    ada_rows = 8 * ((1 + nb + 7) // 8)
    c_all = jnp.zeros((ada_rows, D_MODEL), F32).at[0:1].set(c_prompt).at[1:1 + nb].set(c_sample)
    mods = _ada_call(c_all, ada_w, ada_b)

    ck = cache_k.transpose(0, 1, 3, 4, 2)
    cv = cache_v.transpose(0, 1, 3, 4, 2)

    yp = x_prompt[0]
    ys = x_sample[:, 0]
    outs = {n: [] for n in ('ks', 'vs', 'cp', 'cs', 'sp', 'ss', 'av')}
    kv_all = None
    big = {'w_in': w_in.astype(BF16), 'w_branch': w_branch.astype(BF16), 'w_out': w_out.astype(BF16),
           'w_ff1': w_ff1.astype(BF16), 'w_ff2': w_ff2.astype(BF16)}
    for l in range(depth):
        wts = _layer_weights(l, lbs[l], big, a_ln_g, a_ln_b, a_ws, a_bs, b_qn_g, b_kn_g, c_conv_w,
                             c_conv_b, c_ln_g, c_ln_b, d_norm_g)
        mp = [mods[l, 0:1, i * D_MODEL:(i + 1) * D_MODEL] for i in range(6)]
        ms = [mods[l, 1:1 + nb, i * D_MODEL:(i + 1) * D_MODEL] for i in range(6)]
        n1 = norm1_g[l][None]
        n2 = norm2_g[l][None]
        bias_rows = jnp.zeros((Q_ROWS, ATT_BLK), F32).at[:HEADS].set(
            jnp.broadcast_to(b_bias[l][:, None], (HEADS, ATT_BLK)))
        b_hi = b_bias[l].astype(BF16)
        b_lo = (b_bias[l] - b_hi.astype(F32)).astype(BF16)
        brows = jnp.zeros((HEADS, HEAD_DIM, ATT_BLK), BF16)
        brows = brows.at[:, 0, :].set(b_hi[:, None]).at[:, 1, :].set(b_lo[:, None])

        q, k_all, v_all, kt, vb, macd, gb, conv_tail, st = _front_call(
            yp, n1, mp[1], mp[0], wts, consts, l, depth, kv_all)
        kv_all = (k_all, v_all)
        ob = _attn_call(brows, consts['uo'], q, kt, vb)
        yp = _back_call(yp, ob, macd, gb, mp[2], n2, mp[4], mp[3], mp[5], wts, BACK_TILE)
        outs['cp'].append(conv_tail[CONV_HALO - (CONV_W - 1):][None])
        st4 = st.reshape(HEADS, HEAD_DIM, HEADS, HEAD_DIM)
        outs['sp'].append(jnp.stack([st4[h, :, h, :].T for h in range(HEADS)])[None])

        conv_t = state_conv[:, l].transpose(1, 0, 2)
        s0_r = state_hgrn[:, l].transpose(0, 2, 1, 3).reshape(nb, HEAD_DIM, BRANCH_W)
        qs, ksn, vsn, macd_s, gb_s, conv_new, s_new, avn = _sample_front_call(
            ys, n1, ms[1], ms[0], wts, consts, conv_t, s0_r)
        obs = _paged_attn_call(page_table, qs, bias_rows, consts['uo'], ck, cv, l)
        ys = _back_call(ys, obs[:, 0], macd_s, gb_s, ms[2], n2, ms[4], ms[3], ms[5], wts, nb)
        outs['ks'].append(ksn.reshape(nb, 1, HEADS, HEAD_DIM))
        outs['vs'].append(vsn.reshape(nb, 1, HEADS, HEAD_DIM))
        outs['cs'].append(conv_new.transpose(1, 0, 2))
        outs['ss'].append(s_new.reshape(nb, HEAD_DIM, HEADS, HEAD_DIM).transpose(0, 2, 1, 3))
        outs['av'].append(avn[:, None, :])

    stack = lambda n: jnp.stack(outs[n], axis=1)
    rows_out = lambda a: a.reshape(1, depth, HEADS, HEAD_DIM, seq).transpose(0, 1, 4, 2, 3)
    return (yp[None], ys[:, None, :],
            rows_out(kv_all[0]), rows_out(kv_all[1]), stack('ks'), stack('vs'),
            stack('cp'), stack('cs'), stack('sp'), stack('ss'), stack('av'))
```

```python
import functools

import jax
import jax.numpy as jnp
from jax import lax
from jax.experimental import pallas as pl
from jax.experimental.pallas import tpu as pltpu

F32 = jnp.float32
BF16 = jnp.bfloat16

D_MODEL = 1024
BRANCH_W = 256
HEADS = 4
HEAD_DIM = 64
CHUNK = 128
CONV_W = 31
D_FF = 4096
EPS = 1e-6
QK_SCALE = HEAD_DIM ** -0.5
LOG2E = 1.4426950408889634
MASKED_EXPONENT = -1e30

ROW_TILE = 256
BACK_TILE = 512
SUB = 16
CONV_HALO = 32
CONV_SHIFTS = 8
CONV_BASE = CONV_HALO + CONV_SHIFTS
ATT_BLK = 128
ATT_Q = 512
HEAD_PAIR = 2
PAIR_W = HEAD_PAIR * HEAD_DIM
PAGES_PER_STEP = 64
VMEM_LIMIT = 56 * 1024 * 1024


def _dot(a, b):
    return jnp.dot(a, b, preferred_element_type=F32)


def _bf16_pieces(a, n):
    pieces, rem = [], a
    for _ in range(n):
        p = rem.astype(BF16)
        pieces.append(p)
        rem = rem - p.astype(F32)
    return pieces


def _dot_split_lhs(a, ones_bf, n):
    return _dot(jnp.concatenate(_bf16_pieces(a, n), axis=1), jnp.concatenate([ones_bf] * n, axis=0))


def _dot_split_rhs(ones_bf, b, n):
    return _dot(jnp.concatenate([ones_bf] * n, axis=1), jnp.concatenate(_bf16_pieces(b, n), axis=0))


def _dot_nt(a, b):
    return lax.dot_general(a, b, (((1,), (1,)), ((), ())), preferred_element_type=F32)


def _dot_tn(a, b):
    return lax.dot_general(a, b, (((0,), (0,)), ((), ())), preferred_element_type=F32)


def _sigmoid(x):
    return 0.5 * jnp.tanh(0.5 * x) + 0.5


def _silu(x):
    return x * _sigmoid(x)


def _softplus(x):
    return jnp.maximum(x, 0.0) + jnp.log1p(jnp.exp(-jnp.abs(x)))


def _rms_rows(x):
    return x * lax.rsqrt(jnp.mean(x * x, axis=-1, keepdims=True) + EPS)


def _layer_norm(x, g, b):
    mu = jnp.mean(x, axis=-1, keepdims=True)
    xc = x - mu
    var = jnp.mean(xc * xc, axis=-1, keepdims=True)
    return xc * lax.rsqrt(var + EPS) * g + b


def _head_rms(x, hmask, g):
    ms = _dot_split_lhs(x * x, hmask, 2) * (1.0 / HEAD_DIM)
    return x * lax.rsqrt(ms + EPS) * g


def _log_forget(df, lb):
    ls = -_softplus(-df)
    b = jnp.log1p(-lb) + ls
    pos = lb > 0.0
    a = jnp.log(jnp.where(pos, lb, 1.0))
    lae = jnp.maximum(a, b) + jnp.log1p(jnp.exp(-jnp.abs(a - b)))
    return jnp.where(pos, lae, b)


def _one_minus_forget(df, lb):
    return (1.0 - lb) * _sigmoid(-df)


def _const_spec(shape):
    nd = len(shape)
    return pl.BlockSpec(shape, lambda *_: (0,) * nd, pipeline_mode=pl.Buffered(1))


class _Stacked:
    def __init__(self, arr, *prefix):
        self.arr, self.prefix = arr, prefix


def _operands(items):
    arrays, specs = [], []
    for a in items:
        if isinstance(a, _Stacked):
            tail = a.arr.shape[len(a.prefix):]
            index = a.prefix + (0,) * len(tail)
            arrays.append(a.arr)
            specs.append(pl.BlockSpec((None,) * len(a.prefix) + tail, lambda *_, index=index: index,
                                      pipeline_mode=pl.Buffered(1)))
        else:
            arrays.append(a)
            specs.append(_const_spec(a.shape))
    return arrays, specs


def _ada_kernel(c_ref, w_ref, b_ref, o_ref):
    s = _silu(c_ref[...]).astype(BF16)
    o_ref[...] = _dot(s, w_ref[...].astype(BF16)) + b_ref[...]


def _ada_call(c_all, ada_w, ada_b):
    depth = ada_w.shape[0]
    rows = c_all.shape[0]
    n_col = ada_w.shape[2] // D_MODEL
    return pl.pallas_call(
        _ada_kernel,
        out_shape=jax.ShapeDtypeStruct((depth, rows, ada_w.shape[2]), F32),
        grid=(depth, n_col),
        in_specs=[
            pl.BlockSpec((rows, D_MODEL), lambda l, j: (0, 0)),
            pl.BlockSpec((None, D_MODEL, D_MODEL), lambda l, j: (l, 0, j)),
            pl.BlockSpec((None, 1, D_MODEL), lambda l, j: (l, 0, j)),
        ],
        out_specs=pl.BlockSpec((None, rows, D_MODEL), lambda l, j: (l, 0, j)),
        compiler_params=pltpu.CompilerParams(vmem_limit_bytes=VMEM_LIMIT),
        name="ada_params",
    )(c_all, ada_w, ada_b.reshape(depth, 1, -1))


P_ALN_G, P_ALN_B, P_QN_G, P_KN_G, P_CONV_B, P_CLN_G, P_CLN_B, P_LB, P_DN_G = range(9)


def _gated_local_merge(h, wg_ref, out_a, out_c, out_d, wbr_ref):
    gate = lambda k: _sigmoid(_dot(h, wg_ref[:, k * D_MODEL:(k + 1) * D_MODEL]))
    macd = gate(0) * _dot(out_a.astype(BF16), wbr_ref[0])
    macd = macd + gate(2) * _dot(out_c.astype(BF16), wbr_ref[2])
    macd = macd + gate(3) * _dot(out_d.astype(BF16), wbr_ref[3])
    return macd, gate(1)


def _in_proj_views(win_ref):
    bounds, start = [], 0
    for width in (2 * BRANCH_W, 3 * BRANCH_W, 2 * BRANCH_W, 4 * BRANCH_W, HEADS * D_MODEL):
        bounds.append((start, start + width))
        start += width
    return [win_ref.at[:, lo:hi] for lo, hi in bounds]


def _front_kernel(n_aliased, *refs):
    (x_ref, g1_ref, sc_ref, sh_ref, win_ref, wbr_ref,
     p_ref, aws_ref, abias_ref, convw_ref, hmask_ref, tri_ref,
     q_ref, ktf_ref, vtf_ref, kt_ref, vb_ref, macd_ref, gb_ref, convnew_ref, state_ref,
     convbuf, st_ref) = refs[n_aliased:]
    tm = x_ref.shape[0]
    step = pl.program_id(0)
    wa_ref, wb_ref, wc_ref, wd_ref, wg_ref = _in_proj_views(win_ref)

    @pl.when(step == 0)
    def _():
        convbuf[...] = jnp.zeros_like(convbuf)
        st_ref[...] = jnp.zeros_like(st_ref)

    prm = p_ref[...]
    row = lambda i: prm[i:i + 1, :]
    hmask = hmask_ref[...]
    hmask_f32 = hmask.astype(F32)

    x = x_ref[...]
    h = (_rms_rows(x) * g1_ref[...] * (1.0 + sc_ref[...]) + sh_ref[...]).astype(BF16)

    gate_cols = {}

    def gate_slices(first, count):
        for s in range(first, first + count):
            gate_cols[s] = _sigmoid(_dot(h, wg_ref[:, s * BRANCH_W:(s + 1) * BRANCH_W]))

    za = _dot(h, wa_ref[...])
    zb = _dot(h, wb_ref[...])

    ga = jax.nn.gelu(za)
    u = ga[:, :BRANCH_W]
    vn = _layer_norm(ga[:, BRANCH_W:], row(P_ALN_G), row(P_ALN_B))
    zc = _dot(h, wc_ref[...])
    zd = _dot(h, wd_ref[...])
    lane_grp = lax.broadcasted_iota(jnp.int32, (CHUNK, BRANCH_W), 1) // (BRANCH_W // HEADS)
    tril = (lax.broadcasted_iota(jnp.int32, (CHUNK, CHUNK), 1)
            <= lax.broadcasted_iota(jnp.int32, (CHUNK, CHUNK), 0))
    w_causal = [jnp.where(tril, aws_ref[g], 0.0).astype(BF16) for g in range(HEADS)]
    mixed_chunks = []
    for c in range(tm // CHUNK):
        vc = vn[c * CHUNK:(c + 1) * CHUNK, :].astype(BF16)
        mixed = None
        for g in range(HEADS):
            mg = _dot(w_causal[g], vc)
            mixed = mg if mixed is None else jnp.where(lane_grp == g, mg, mixed)
        mixed_chunks.append(mixed + abias_ref[...])
    out_a = u * jnp.concatenate(mixed_chunks, axis=0)

    gate_slices(0, 2)

    qn = _head_rms(zb[:, :BRANCH_W], hmask, row(P_QN_G))
    kn = _head_rms(zb[:, BRANCH_W:2 * BRANCH_W], hmask, row(P_KN_G))
    vv = zb[:, 2 * BRANCH_W:]
    q_ref[...] = (qn * QK_SCALE).astype(BF16)
    knt_f32 = kn.T
    ktf_ref[...] = knt_f32
    vtf_ref[...] = vv.T
    vb_ref[...] = vv.astype(BF16)
    knt = knt_f32.astype(BF16)
    for c in range(tm // ATT_BLK):
        kt_ref[c] = knt[:, c * ATT_BLK:(c + 1) * ATT_BLK]

    gate_slices(2, 3)

    cin = zc[:, :BRANCH_W] * _sigmoid(zc[:, BRANCH_W:])
    for r in range(CONV_SHIFTS):
        convbuf[r, CONV_BASE - r:CONV_BASE - r + tm, :] = cin
    acc = jnp.broadcast_to(row(P_CONV_B), (tm, BRANCH_W))
    for j in range(CONV_W):
        lead = CONV_BASE - (CONV_W - 1) + j
        r = lead % CONV_SHIFTS
        acc = acc + convw_ref[j:j + 1, :] * convbuf[r, lead - r:lead - r + tm, :]
    out_c = _silu(_layer_norm(acc, row(P_CLN_G), row(P_CLN_B)))
    convnew_ref[...] = convbuf[0, tm + CONV_BASE - CONV_HALO:tm + CONV_BASE, :]
    for r in range(CONV_SHIFTS):
        convbuf[r, 0:CONV_BASE, :] = convbuf[r, tm:tm + CONV_BASE, :]

    gate_slices(5, 3)

    n_c = tm // SUB
    in_chunks = lambda a: a.reshape(n_c, SUB, BRANCH_W)
    qd = _silu(zd[:, :BRANCH_W]) * QK_SCALE
    logf = _log_forget(zd[:, BRANCH_W:2 * BRANCH_W], row(P_LB))
    kd = _one_minus_forget(zd[:, BRANCH_W:2 * BRANCH_W], row(P_LB))
    di = zd[:, 2 * BRANCH_W:3 * BRANCH_W]
    b = _dot_split_rhs(tri_ref[...], logf, 3)
    b3, qd3, kd3, di3 = in_chunks(b), in_chunks(qd), in_chunks(kd), in_chunks(di)
    bl3 = b3[:, SUB - 1:SUB, :]
    qt = (qd * jnp.exp(b)).astype(BF16)
    kt = (kd3 * jnp.exp(bl3 - b3)).reshape(tm, BRANCH_W).astype(BF16)
    el = jnp.exp(bl3)
    di_bf = di.astype(BF16)
    increments = [_dot_tn(di_bf[c * SUB:(c + 1) * SUB], kt[c * SUB:(c + 1) * SUB]) * hmask_f32
                  for c in range(n_c)]

    t_idx = lax.broadcasted_iota(jnp.int32, (n_c, SUB, BRANCH_W), 1)
    od = jnp.zeros((tm, BRANCH_W), F32)
    n_gate_cols = HEADS * D_MODEL // BRANCH_W
    for s in range(SUB):
        e = jnp.exp(jnp.where(t_idx >= s, b3 - b3[:, s:s + 1, :], MASKED_EXPONENT))
        wgt = e * qd3 * kd3[:, s:s + 1, :]
        coef = _dot(wgt.reshape(tm, BRANCH_W).astype(BF16), hmask)
        i_s = jnp.broadcast_to(di3[:, s:s + 1, :], (n_c, SUB, BRANCH_W)).reshape(tm, BRANCH_W)
        od = od + coef * i_s
        if s % 2 == 0 and 8 + s // 2 < n_gate_cols:
            gate_slices(8 + s // 2, 1)
    assert len(gate_cols) == n_gate_cols
    per_gate = D_MODEL // BRANCH_W
    gate = lambda k: jnp.concatenate([gate_cols[c] for c in range(k * per_gate, (k + 1) * per_gate)], axis=1)
    gate_a, gate_c, gate_d = gate(0), gate(2), gate(3)
    gb_ref[...] = gate(1)

    st = st_ref[...]
    o_inter = []
    for c in range(n_c):
        o_inter.append(_dot_nt(qt[c * SUB:(c + 1) * SUB], st.astype(BF16)))
        st = st * el[c] + increments[c]
    st_ref[...] = st
    state_ref[...] = st
    od = od + jnp.concatenate(o_inter, axis=0)
    out_d = _head_rms(od, hmask, row(P_DN_G)) * _silu(zd[:, 3 * BRANCH_W:])

    macd = gate_a * _dot(out_a.astype(BF16), wbr_ref[0])
    macd = macd + gate_c * _dot(out_c.astype(BF16), wbr_ref[2])
    macd_ref[...] = macd + gate_d * _dot(out_d.astype(BF16), wbr_ref[3])


def _front_call(x, g1, sc, sh, wts, consts, layer, depth, kv_prev):
    length = x.shape[0]
    tm = ROW_TILE
    n_steps = length // tm
    row_spec = lambda w: pl.BlockSpec((tm, w), lambda i: (i, 0))
    const_arrays, const_specs = _operands([g1, sc, sh, wts['win'], wts['wbr'],
                                           wts['ptab'], wts['a_ws'], wts['a_bias'], wts['conv_w'],
                                           consts['hmask'], consts['tri']])
    aliased = [] if kv_prev is None else list(kv_prev)
    in_arrays = aliased + [x] + const_arrays
    in_specs = [pl.BlockSpec(memory_space=pl.ANY)] * len(aliased) + [row_spec(D_MODEL)] + const_specs
    kv_spec = pl.BlockSpec((None, BRANCH_W, tm), lambda i: (layer, 0, i))
    out_shape = [
        jax.ShapeDtypeStruct((length, BRANCH_W), BF16),
        jax.ShapeDtypeStruct((depth, BRANCH_W, length), F32),
        jax.ShapeDtypeStruct((depth, BRANCH_W, length), F32),
        jax.ShapeDtypeStruct((length // ATT_BLK, BRANCH_W, ATT_BLK), BF16),
        jax.ShapeDtypeStruct((length, BRANCH_W), BF16),
        jax.ShapeDtypeStruct((length, D_MODEL), F32),
        jax.ShapeDtypeStruct((length, D_MODEL), F32),
        jax.ShapeDtypeStruct((CONV_HALO, BRANCH_W), F32),
        jax.ShapeDtypeStruct((BRANCH_W, BRANCH_W), F32),
    ]
    out_specs = [
        row_spec(BRANCH_W), kv_spec, kv_spec,
        pl.BlockSpec((tm // ATT_BLK, BRANCH_W, ATT_BLK), lambda i: (i, 0, 0)),
        row_spec(BRANCH_W), row_spec(D_MODEL), row_spec(D_MODEL),
        _const_spec((CONV_HALO, BRANCH_W)), _const_spec((BRANCH_W, BRANCH_W)),
    ]
    scratch = [
        pltpu.VMEM((CONV_SHIFTS, tm + CONV_BASE, BRANCH_W), F32),
        pltpu.VMEM((BRANCH_W, BRANCH_W), F32),
    ]
    return pl.pallas_call(
        functools.partial(_front_kernel, len(aliased)),
        out_shape=out_shape,
        grid=(n_steps,),
        in_specs=in_specs,
        out_specs=out_specs,
        scratch_shapes=scratch,
        input_output_aliases={n: 1 + n for n in range(len(aliased))},
        compiler_params=pltpu.CompilerParams(
            dimension_semantics=("arbitrary",), vmem_limit_bytes=VMEM_LIMIT),
        name="prompt_front",
    )(*in_arrays)


def _sb_scores(z, uo, causal):
    sp = jnp.maximum(z, 0.0) + jnp.log(1.0 + jnp.exp2(jnp.abs(z) * (-LOG2E)))
    if causal is not None:
        sp = jnp.where(causal, sp, 0.0)
    hi = sp.astype(BF16)
    lo = (sp - hi.astype(F32)).astype(BF16)
    cs = _dot(jnp.concatenate([hi, lo], axis=1), uo)
    return z - cs[:, :ATT_BLK], cs[:, ATT_BLK:]


def _sb_weights(d, r, causal):
    w = jnp.exp(d - r)
    if causal is not None:
        w = jnp.where(causal, w, 0.0)
    return w


def _attn_kernel(brows_ref, uo_ref, q_ref, kt_ref, v_ref, o_ref, acc_ref, r_ref, z_s, d_s, t_s):
    i = pl.program_id(0)
    n_sub = ATT_Q // ATT_BLK
    n_old = i * n_sub
    q = q_ref[...]
    unit_cols = jnp.where(lax.broadcasted_iota(jnp.int32, (ATT_Q, HEAD_DIM), 1) < 2, 1.0, 0.0).astype(BF16)
    q_ext = [jnp.concatenate([q[:, HEAD_DIM * h:HEAD_DIM * (h + 1)], unit_cols], axis=1)
             for h in range(HEADS)]
    acc_ref[...] = jnp.zeros_like(acc_ref)
    r_ref[...] = jnp.zeros_like(r_ref)
    lane_half = lax.broadcasted_iota(jnp.int32, (ATT_BLK, PAIR_W), 1) // HEAD_DIM
    half_lanes = [jnp.where(lane_half == k, 1.0, 0.0).astype(BF16) for k in range(HEAD_PAIR)]
    causal = (lax.broadcasted_iota(jnp.int32, (ATT_BLK, ATT_BLK), 1)
              < lax.broadcasted_iota(jnp.int32, (ATT_BLK, ATT_BLK), 0))

    brows_pair = [jnp.concatenate([brows_ref[h], brows_ref[h]], axis=1) for h in range(HEADS)]

    def logits(j, h, row0, n_rows):
        kt_h = jnp.concatenate([kt_ref[j, HEAD_DIM * h:HEAD_DIM * (h + 1), :], brows_ref[h]], axis=0)
        return _dot(q_ext[h][row0:row0 + n_rows, :], kt_h)

    def values(j, g):
        vb = v_ref[pl.ds(pl.multiple_of(j * ATT_BLK, ATT_BLK), ATT_BLK), PAIR_W * g:PAIR_W * (g + 1)]
        return jnp.concatenate([vb * half_lanes[k] for k in range(HEAD_PAIR)], axis=0)

    pieces = []
    for kb in reversed(range(n_sub)):
        pieces.append((kb, kb * ATT_BLK, ATT_BLK, causal))
        if kb + 1 < n_sub:
            pieces.append((kb, (kb + 1) * ATT_BLK, (n_sub - kb - 1) * ATT_BLK, None))
    scores = [[_sb_scores(logits(n_old + kb, h, row0, n_rows), uo_ref[...], mask) for h in range(HEADS)]
              for kb, row0, n_rows, mask in pieces]
    for (kb, row0, n_rows, mask), piece_scores in zip(pieces, scores):
        for g in range(HEADS // HEAD_PAIR):
            ws = []
            for h in range(HEAD_PAIR * g, HEAD_PAIR * (g + 1)):
                d, tot = piece_scores[h]
                r = r_ref[h, row0:row0 + n_rows, :]
                ws.append(_sb_weights(d, r, mask).astype(BF16))
                r_ref[h, row0:row0 + n_rows, :] = r + tot
            acc_ref[g, row0:row0 + n_rows, :] += _dot(jnp.concatenate(ws, axis=1), values(n_old + kb, g))

    assert n_sub % 4 == 0
    def block_of(t):
        return jnp.maximum(n_old - 1 - t, 0)

    def older_blocks(g):
        heads = range(HEAD_PAIR * g, HEAD_PAIR * (g + 1))

        def stage1(p, z_slot):
            j0, j1 = block_of(2 * p), block_of(2 * p + 1)
            for k, h in enumerate(heads):
                rows_h = slice(HEAD_DIM * h, HEAD_DIM * (h + 1))
                kt_pair = jnp.concatenate([
                    jnp.concatenate([kt_ref[j0, rows_h, :], kt_ref[j1, rows_h, :]], axis=1),
                    brows_pair[h]], axis=0)
                z = _dot(q_ext[h], kt_pair)
                z_s[z_slot, 0, k] = z[:, :ATT_BLK]
                z_s[z_slot, 1, k] = z[:, ATT_BLK:]

        def stage2(z_slot, blk, d_slot):
            for k in range(HEAD_PAIR):
                d, tot = _sb_scores(z_s[z_slot, blk, k], uo_ref[...], None)
                d_s[d_slot, k] = d
                t_s[d_slot, k] = tot

        def stage3(t, d_slot):
            ws = []
            for k, h in enumerate(heads):
                r = r_ref[h]
                ws.append(_sb_weights(d_s[d_slot, k], r, None).astype(BF16))
                r_ref[h] = r + t_s[d_slot, k]
            acc_ref[g] += _dot(jnp.concatenate(ws, axis=1), values(block_of(t), g))

        stage1(0, 0)
        stage2(0, 0, 0)

        def four_blocks(v, carry):
            for half in range(2):
                p = 2 * v + half
                stage1(p + 1, 1 - half)
                stage2(half, 1, 1)
                stage3(2 * p, 0)
                stage2(1 - half, 0, 0)
                stage3(2 * p + 1, 1)
            return carry

        lax.fori_loop(0, n_old // 4, four_blocks, 0)

    for g in range(HEADS // HEAD_PAIR):
        older_blocks(g)
        o_ref[:, PAIR_W * g:PAIR_W * (g + 1)] = acc_ref[g]


def _attn_call(brows, uo, q, kt, vb):
    length = q.shape[0]
    return pl.pallas_call(
        _attn_kernel,
        out_shape=jax.ShapeDtypeStruct((length, BRANCH_W), F32),
        grid=(length // ATT_Q,),
        in_specs=[
            _const_spec(brows.shape), _const_spec(uo.shape),
            pl.BlockSpec((ATT_Q, BRANCH_W), lambda i: (i, 0)),
            _const_spec(kt.shape), _const_spec(vb.shape),
        ],
        out_specs=pl.BlockSpec((ATT_Q, BRANCH_W), lambda i: (i, 0)),
        scratch_shapes=[
            pltpu.VMEM((HEADS // HEAD_PAIR, ATT_Q, PAIR_W), F32),
            pltpu.VMEM((HEADS, ATT_Q, ATT_BLK), F32),
            pltpu.VMEM((2, 2, HEAD_PAIR, ATT_Q, ATT_BLK), F32),
            pltpu.VMEM((2, HEAD_PAIR, ATT_Q, ATT_BLK), F32),
            pltpu.VMEM((2, HEAD_PAIR, ATT_Q, ATT_BLK), F32),
        ],
        compiler_params=pltpu.CompilerParams(
            dimension_semantics=("arbitrary",), vmem_limit_bytes=VMEM_LIMIT),
        name="prompt_attn",
    )(brows, uo, q, kt, vb)


def _back_kernel(x_ref, ob_ref, macd_ref, gb_ref, g1_ref, n2_ref, sc2_ref, sh2_ref, g2_ref,
                 wb1_ref, wo_ref, w1_ref, w2_ref, y_ref):
    merged = macd_ref[...] + gb_ref[...] * _dot(ob_ref[...].astype(BF16), wb1_ref[...])
    x1 = x_ref[...] + g1_ref[...] * _dot(merged.astype(BF16), wo_ref[...])
    h2 = (_rms_rows(x1) * n2_ref[...] * (1.0 + sc2_ref[...]) + sh2_ref[...]).astype(BF16)
    hid = jnp.maximum(_dot(h2, w1_ref[...]), 0.0)
    y_ref[...] = x1 + g2_ref[...] * _dot((hid * hid).astype(BF16), w2_ref[...])


def _back_call(x, ob, macd, gb, g1, n2, sc2, sh2, g2, wts, tm):
    rows = x.shape[0]
    mod_rows = g1.shape[0]
    if mod_rows == 1:
        mod_spec = _const_spec((1, D_MODEL))
    else:
        mod_spec = pl.BlockSpec((tm, D_MODEL), lambda i: (i, 0))
    row_spec = lambda w: pl.BlockSpec((tm, w), lambda i: (i, 0))
    weights, weight_specs = _operands([wts['wb1'], wts['wo'], wts['w1'], wts['w2']])
    return pl.pallas_call(
        _back_kernel,
        out_shape=jax.ShapeDtypeStruct((rows, D_MODEL), F32),
        grid=(rows // tm,),
        in_specs=[row_spec(D_MODEL), row_spec(BRANCH_W), row_spec(D_MODEL), row_spec(D_MODEL),
                  mod_spec, _const_spec((1, D_MODEL)), mod_spec, mod_spec, mod_spec] + weight_specs,
        out_specs=row_spec(D_MODEL),
        compiler_params=pltpu.CompilerParams(
            dimension_semantics=("arbitrary",), vmem_limit_bytes=VMEM_LIMIT),
        name="layer_back",
    )(x, ob, macd, gb, g1, n2, sc2, sh2, g2, *weights)


def _sample_front_kernel(x_ref, g1_ref, sc_ref, sh_ref, win_ref, wbr_ref,
                         p_ref, a0_ref, convw_ref, hmask_ref, tdiag_ref, conv_ref, s0_ref,
                         q_ref, k_ref, v_ref, macd_ref, gb_ref, convnew_ref, snew_ref, av_ref):
    nb = x_ref.shape[0]
    wa_ref, wb_ref, wc_ref, wd_ref, wg_ref = _in_proj_views(win_ref)
    prm = p_ref[...]
    row = lambda i: prm[i:i + 1, :]
    hmask = hmask_ref[...]

    x = x_ref[...]
    h = (_rms_rows(x) * g1_ref[...] * (1.0 + sc_ref[...]) + sh_ref[...]).astype(BF16)

    ga = jax.nn.gelu(_dot(h, wa_ref[...]))
    vn = _layer_norm(ga[:, BRANCH_W:], row(P_ALN_G), row(P_ALN_B))
    av_ref[...] = vn
    out_a = ga[:, :BRANCH_W] * (a0_ref[0:1, :] * vn + a0_ref[1:2, :])

    zb = _dot(h, wb_ref[...])
    qn = _head_rms(zb[:, :BRANCH_W], hmask, row(P_QN_G))
    kn = _head_rms(zb[:, BRANCH_W:2 * BRANCH_W], hmask, row(P_KN_G))
    q_ref[...] = qn * QK_SCALE
    k_ref[...] = kn
    v_ref[...] = zb[:, 2 * BRANCH_W:]

    zc = _dot(h, wc_ref[...])
    cin = zc[:, :BRANCH_W] * _sigmoid(zc[:, BRANCH_W:])
    acc = row(P_CONV_B) + convw_ref[CONV_W - 1:CONV_W, :] * cin
    for j in range(CONV_W - 1):
        acc = acc + convw_ref[j:j + 1, :] * conv_ref[j]
        if j >= 1:
            convnew_ref[j - 1] = conv_ref[j]
    convnew_ref[CONV_W - 2] = cin
    out_c = _silu(_layer_norm(acc, row(P_CLN_G), row(P_CLN_B)))

    zd = _dot(h, wd_ref[...])
    qd = _silu(zd[:, :BRANCH_W]) * QK_SCALE
    logf = _log_forget(zd[:, BRANCH_W:2 * BRANCH_W], row(P_LB))
    ea = jnp.exp(logf)
    kd = _one_minus_forget(zd[:, BRANCH_W:2 * BRANCH_W], row(P_LB))
    di = zd[:, 2 * BRANCH_W:3 * BRANCH_W]
    tdiag = tdiag_ref[...]

    def spread(r):
        t3 = jnp.broadcast_to(r[:, None, :], (nb, HEAD_DIM, BRANCH_W)) * tdiag[None]
        flat = _dot_split_lhs(t3.reshape(nb * HEAD_DIM, BRANCH_W), hmask, 3)
        return flat.reshape(nb, HEAD_DIM, BRANCH_W)

    s0 = s0_ref[...]
    od = jnp.sum(spread(qd * ea) * s0, axis=1) + _dot_split_lhs(qd * kd, hmask, 3) * di
    snew_ref[...] = spread(ea) * s0 + spread(kd) * di[:, None, :]
    out_d = _head_rms(od, hmask, row(P_DN_G)) * _silu(zd[:, 3 * BRANCH_W:])

    macd, gb = _gated_local_merge(h, wg_ref, out_a, out_c, out_d, wbr_ref)
    macd_ref[...] = macd
    gb_ref[...] = gb


def _sample_front_call(x, g1, sc, sh, wts, consts, conv_t, s0_r):
    nb = x.shape[0]
    in_arrays, in_specs = _operands([x, g1, sc, sh, wts['win'], wts['wbr'],
                                     wts['ptab'], wts['a0'], wts['conv_w'], consts['hmask'], consts['tdiag'],
                                     conv_t, s0_r])
    out_shape = [
        jax.ShapeDtypeStruct((nb, BRANCH_W), F32),
        jax.ShapeDtypeStruct((nb, BRANCH_W), F32),
        jax.ShapeDtypeStruct((nb, BRANCH_W), F32),
        jax.ShapeDtypeStruct((nb, D_MODEL), F32),
        jax.ShapeDtypeStruct((nb, D_MODEL), F32),
        jax.ShapeDtypeStruct(conv_t.shape, F32),
        jax.ShapeDtypeStruct(s0_r.shape, F32),
        jax.ShapeDtypeStruct((nb, BRANCH_W), F32),
    ]
    return pl.pallas_call(
        _sample_front_kernel,
        out_shape=out_shape,
        grid=(1,),
        in_specs=in_specs,
        out_specs=[_const_spec(s.shape) for s in out_shape],
        compiler_params=pltpu.CompilerParams(vmem_limit_bytes=VMEM_LIMIT),
        name="sample_front",
    )(*in_arrays)


Q_ROWS = 16


def _paged_attn_kernel(pt_ref, q_ref, bias_ref, uo_ref, *refs):
    k_refs = refs[:PAGES_PER_STEP]
    v_refs = refs[PAGES_PER_STEP:2 * PAGES_PER_STEP]
    o_ref, acc_ref, r_ref = refs[2 * PAGES_PER_STEP:]
    jc = pl.program_id(1)

    @pl.when(jc == 0)
    def _():
        acc_ref[...] = jnp.zeros_like(acc_ref)
        r_ref[...] = jnp.zeros_like(r_ref)

    row_id = lax.broadcasted_iota(jnp.int32, (Q_ROWS, BRANCH_W), 0)
    own_head = row_id == lax.broadcasted_iota(jnp.int32, (Q_ROWS, BRANCH_W), 1) // HEAD_DIM
    qrows = jnp.where(own_head, jnp.broadcast_to(q_ref[...], (Q_ROWS, BRANCH_W)), 0.0).astype(BF16)
    as_matrix = lambda ref: ref[...].reshape(BRANCH_W, CHUNK).astype(BF16)
    pages = range(PAGES_PER_STEP)
    kt_all = jnp.concatenate([as_matrix(k_refs[p]) for p in pages], axis=1)
    z = _dot(qrows, kt_all)
    z_rows = jnp.concatenate(
        [z[:, p * CHUNK:(p + 1) * CHUNK] + bias_ref[...] for p in pages], axis=0)
    d, tot = _sb_scores(z_rows, uo_ref[...], None)
    r = r_ref[...]
    r_pages = [None] * PAGES_PER_STEP
    for p in reversed(pages):
        r_pages[p] = r
        r = r + tot[p * Q_ROWS:(p + 1) * Q_ROWS, :]
    r_ref[...] = r
    w = _sb_weights(d, jnp.concatenate(r_pages, axis=0), None).astype(BF16)
    w_all = jnp.concatenate([w[p * Q_ROWS:(p + 1) * Q_ROWS, :] for p in pages], axis=1)
    vt_all = jnp.concatenate([as_matrix(v_refs[p]) for p in pages], axis=1)
    acc = acc_ref[...] + _dot_nt(w_all, vt_all)
    acc_ref[...] = acc

    @pl.when(jc == pl.num_programs(1) - 1)
    def _():
        o_ref[...] = jnp.sum(jnp.where(own_head, acc, 0.0), axis=0, keepdims=True)


def _paged_attn_call(page_table, q, bias_rows, uo, cache_k, cache_v, layer):
    nb, n_pages = page_table.shape
    n_chunks = n_pages // PAGES_PER_STEP

    def page_spec(p):
        def index_map(b, jc, pt):
            return (pt[b, (n_chunks - 1 - jc) * PAGES_PER_STEP + p], layer, 0, 0, 0)
        return pl.BlockSpec((None, None, HEADS, HEAD_DIM, CHUNK), index_map)

    page_specs = [page_spec(p) for p in range(PAGES_PER_STEP)]
    grid_spec = pltpu.PrefetchScalarGridSpec(
        num_scalar_prefetch=1,
        grid=(nb, n_chunks),
        in_specs=[
            pl.BlockSpec((None, 1, BRANCH_W), lambda b, jc, pt: (b, 0, 0)),
            pl.BlockSpec(bias_rows.shape, lambda b, jc, pt: (0, 0)),
            pl.BlockSpec(uo.shape, lambda b, jc, pt: (0, 0)),
        ] + page_specs + page_specs,
        out_specs=pl.BlockSpec((None, 1, BRANCH_W), lambda b, jc, pt: (b, 0, 0)),
        scratch_shapes=[
            pltpu.VMEM((Q_ROWS, BRANCH_W), F32),
            pltpu.VMEM((Q_ROWS, CHUNK), F32),
        ],
    )
    return pl.pallas_call(
        _paged_attn_kernel,
        out_shape=jax.ShapeDtypeStruct((nb, 1, BRANCH_W), F32),
        grid_spec=grid_spec,
        compiler_params=pltpu.CompilerParams(
            dimension_semantics=("arbitrary", "arbitrary"), vmem_limit_bytes=VMEM_LIMIT),
        name="sample_paged_attn",
    )(page_table, q.reshape(nb, 1, BRANCH_W), bias_rows, uo,
      *([cache_k] * PAGES_PER_STEP), *([cache_v] * PAGES_PER_STEP))


def _constants():
    lane = jnp.arange(BRANCH_W)
    hmask = (lane[:, None] // HEAD_DIM == lane[None, :] // HEAD_DIM).astype(BF16)
    r = jnp.arange(ROW_TILE)
    tri = ((r[:, None] // SUB == r[None, :] // SUB) & (r[None, :] <= r[:, None])).astype(BF16)
    tdiag = (jnp.arange(HEAD_DIM)[:, None] == lane[None, :] % HEAD_DIM).astype(F32)
    j = jnp.arange(2 * ATT_BLK) % ATT_BLK
    c = jnp.arange(2 * ATT_BLK)
    uo = ((c[None, :] >= ATT_BLK) | (j[:, None] >= c[None, :])).astype(BF16)
    return {'hmask': hmask, 'tri': tri, 'tdiag': tdiag, 'uo': uo}


def _layer_weights(l, lb, big, a_ln_g, a_ln_b, a_ws, a_bs, b_qn_g, b_kn_g, c_conv_w, c_conv_b,
                   c_ln_g, c_ln_b, d_norm_g):
    tile4 = lambda g: jnp.tile(g, HEADS)
    rows = [a_ln_g[l], a_ln_b[l], tile4(b_qn_g[l]), tile4(b_kn_g[l]), c_conv_b[l], c_ln_g[l], c_ln_b[l],
            lb, tile4(d_norm_g[l])]
    ptab = jnp.zeros((16, BRANCH_W), F32).at[:len(rows)].set(jnp.stack(rows))
    grp = BRANCH_W // HEADS
    return {
        'win': _Stacked(big['w_in'], l),
        'wbr': _Stacked(big['w_branch'], l), 'wb1': _Stacked(big['w_branch'], l, 1), 'ptab': ptab,
        'a_ws': a_ws[l], 'a_bias': jnp.repeat(a_bs[l].T, grp, axis=1),
        'a0': jnp.stack([jnp.repeat(a_ws[l][:, 0, 0], grp), jnp.repeat(a_bs[l][:, 0], grp)]),
        'conv_w': jnp.zeros((32, BRANCH_W), F32).at[:CONV_W].set(c_conv_w[l]),
        'wo': _Stacked(big['w_out'], l), 'w1': _Stacked(big['w_ff1'], l), 'w2': _Stacked(big['w_ff2'], l),
    }


def kernel(x_prompt, x_sample, c_prompt, c_sample, cache_k, cache_v, state_conv, state_hgrn, page_table,
           ada_w, ada_b, norm1_g, norm2_g, w_in, a_ln_g, a_ln_b, a_ws, a_bs, b_qn_g, b_kn_g, b_bias,
           c_conv_w, c_conv_b, c_ln_g, c_ln_b, d_lb_logits, d_norm_g, w_branch, w_out, w_ff1, w_ff2):
    depth = w_in.shape[0]
    seq = x_prompt.shape[1]
    nb = x_sample.shape[0]
    n_pool = cache_k.shape[0]
    consts = _constants()

    cp = jnp.cumsum(jax.nn.softmax(d_lb_logits.astype(F32), axis=0), axis=0)
    lbs = cp - cp[0:1]

    ada_rows = 8 * ((1 + nb + 7) // 8)
    c_all = jnp.zeros((ada_rows, D_MODEL), F32).at[0:1].set(c_prompt).at[1:1 + nb].set(c_sample)
    mods = _ada_call(c_all, ada_w, ada_b)

    ck = cache_k.transpose(0, 1, 3, 4, 2)
    cv = cache_v.transpose(0, 1, 3, 4, 2)

    yp = x_prompt[0]
    ys = x_sample[:, 0]
    outs = {n: [] for n in ('ks', 'vs', 'cp', 'cs', 'sp', 'ss', 'av')}
    kv_all = None
    big = {'w_in': w_in.astype(BF16), 'w_branch': w_branch.astype(BF16), 'w_out': w_out.astype(BF16),
           'w_ff1': w_ff1.astype(BF16), 'w_ff2': w_ff2.astype(BF16)}
    for l in range(depth):
        wts = _layer_weights(l, lbs[l], big, a_ln_g, a_ln_b, a_ws, a_bs, b_qn_g, b_kn_g, c_conv_w,
                             c_conv_b, c_ln_g, c_ln_b, d_norm_g)
        mp = [mods[l, 0:1, i * D_MODEL:(i + 1) * D_MODEL] for i in range(6)]
        ms = [mods[l, 1:1 + nb, i * D_MODEL:(i + 1) * D_MODEL] for i in range(6)]
        n1 = norm1_g[l][None]
        n2 = norm2_g[l][None]
        bias_rows = jnp.zeros((Q_ROWS, ATT_BLK), F32).at[:HEADS].set(
            jnp.broadcast_to(b_bias[l][:, None], (HEADS, ATT_BLK)))
        b_hi = b_bias[l].astype(BF16)
        b_lo = (b_bias[l] - b_hi.astype(F32)).astype(BF16)
        brows = jnp.zeros((HEADS, HEAD_DIM, ATT_BLK), BF16)
        brows = brows.at[:, 0, :].set(b_hi[:, None]).at[:, 1, :].set(b_lo[:, None])

        q, k_all, v_all, kt, vb, macd, gb, conv_tail, st = _front_call(
            yp, n1, mp[1], mp[0], wts, consts, l, depth, kv_all)
        kv_all = (k_all, v_all)
        ob = _attn_call(brows, consts['uo'], q, kt, vb)
        yp = _back_call(yp, ob, macd, gb, mp[2], n2, mp[4], mp[3], mp[5], wts, BACK_TILE)
        outs['cp'].append(conv_tail[CONV_HALO - (CONV_W - 1):][None])
        st4 = st.reshape(HEADS, HEAD_DIM, HEADS, HEAD_DIM)
        outs['sp'].append(jnp.stack([st4[h, :, h, :].T for h in range(HEADS)])[None])

        conv_t = state_conv[:, l].transpose(1, 0, 2)
        s0_r = state_hgrn[:, l].transpose(0, 2, 1, 3).reshape(nb, HEAD_DIM, BRANCH_W)
        qs, ksn, vsn, macd_s, gb_s, conv_new, s_new, avn = _sample_front_call(
            ys, n1, ms[1], ms[0], wts, consts, conv_t, s0_r)
        obs = _paged_attn_call(page_table, qs, bias_rows, consts['uo'], ck, cv, l)
        ys = _back_call(ys, obs[:, 0], macd_s, gb_s, ms[2], n2, ms[4], ms[3], ms[5], wts, nb)
        outs['ks'].append(ksn.reshape(nb, 1, HEADS, HEAD_DIM))
        outs['vs'].append(vsn.reshape(nb, 1, HEADS, HEAD_DIM))
        outs['cs'].append(conv_new.transpose(1, 0, 2))
        outs['ss'].append(s_new.reshape(nb, HEAD_DIM, HEADS, HEAD_DIM).transpose(0, 2, 1, 3))
        outs['av'].append(avn[:, None, :])

    stack = lambda n: jnp.stack(outs[n], axis=1)
    rows_out = lambda a: a.reshape(1, depth, HEADS, HEAD_DIM, seq).transpose(0, 1, 4, 2, 3)
    return (yp[None], ys[:, None, :],
            rows_out(kv_all[0]), rows_out(kv_all[1]), stack('ks'), stack('vs'),
            stack('cp'), stack('cs'), stack('sp'), stack('ss'), stack('av'))
```

```python
import functools

import jax
import jax.numpy as jnp
from jax import lax
from jax.experimental import pallas as pl
from jax.experimental.pallas import tpu as pltpu

F32 = jnp.float32
BF16 = jnp.bfloat16

D_MODEL = 1024
BRANCH_W = 256
HEADS = 4
HEAD_DIM = 64
CHUNK = 128
CONV_W = 31
D_FF = 4096
EPS = 1e-6
QK_SCALE = HEAD_DIM ** -0.5
LOG2E = 1.4426950408889634
MASKED_EXPONENT = -1e30

ROW_TILE = 256
BACK_TILE = 512
SUB = 16
CONV_HALO = 32
CONV_SHIFTS = 8
CONV_BASE = CONV_HALO + CONV_SHIFTS
ATT_BLK = 128
ATT_Q = 512
HEAD_PAIR = 2
PAIR_W = HEAD_PAIR * HEAD_DIM
PAGES_PER_STEP = 64
VMEM_LIMIT = 56 * 1024 * 1024


def _dot(a, b):
    return jnp.dot(a, b, preferred_element_type=F32)


def _bf16_pieces(a, n):
    pieces, rem = [], a
    for _ in range(n):
        p = rem.astype(BF16)
        pieces.append(p)
        rem = rem - p.astype(F32)
    return pieces


def _dot_split_lhs(a, ones_bf, n):
    return _dot(jnp.concatenate(_bf16_pieces(a, n), axis=1), jnp.concatenate([ones_bf] * n, axis=0))


def _dot_split_rhs(ones_bf, b, n):
    return _dot(jnp.concatenate([ones_bf] * n, axis=1), jnp.concatenate(_bf16_pieces(b, n), axis=0))


def _dot_nt(a, b):
    return lax.dot_general(a, b, (((1,), (1,)), ((), ())), preferred_element_type=F32)


def _dot_tn(a, b):
    return lax.dot_general(a, b, (((0,), (0,)), ((), ())), preferred_element_type=F32)


def _sigmoid(x):
    return 0.5 * jnp.tanh(0.5 * x) + 0.5


def _silu(x):
    return x * _sigmoid(x)


def _softplus(x):
    return jnp.maximum(x, 0.0) + jnp.log1p(jnp.exp(-jnp.abs(x)))


def _rms_rows(x):
    return x * lax.rsqrt(jnp.mean(x * x, axis=-1, keepdims=True) + EPS)


def _layer_norm(x, g, b):
    mu = jnp.mean(x, axis=-1, keepdims=True)
    xc = x - mu
    var = jnp.mean(xc * xc, axis=-1, keepdims=True)
    return xc * lax.rsqrt(var + EPS) * g + b


def _head_rms(x, hmask, g):
    ms = _dot_split_lhs(x * x, hmask, 2) * (1.0 / HEAD_DIM)
    return x * lax.rsqrt(ms + EPS) * g


def _log_forget(df, lb):
    ls = -_softplus(-df)
    b = jnp.log1p(-lb) + ls
    pos = lb > 0.0
    a = jnp.log(jnp.where(pos, lb, 1.0))
    lae = jnp.maximum(a, b) + jnp.log1p(jnp.exp(-jnp.abs(a - b)))
    return jnp.where(pos, lae, b)


def _one_minus_forget(df, lb):
    return (1.0 - lb) * _sigmoid(-df)


def _const_spec(shape):
    nd = len(shape)
    return pl.BlockSpec(shape, lambda *_: (0,) * nd, pipeline_mode=pl.Buffered(1))


class _Stacked:
    def __init__(self, arr, *prefix):
        self.arr, self.prefix = arr, prefix


def _operands(items):
    arrays, specs = [], []
    for a in items:
        if isinstance(a, _Stacked):
            tail = a.arr.shape[len(a.prefix):]
            index = a.prefix + (0,) * len(tail)
            arrays.append(a.arr)
            specs.append(pl.BlockSpec((None,) * len(a.prefix) + tail, lambda *_, index=index: index,
                                      pipeline_mode=pl.Buffered(1)))
        else:
            arrays.append(a)
            specs.append(_const_spec(a.shape))
    return arrays, specs


def _ada_kernel(c_ref, w_ref, b_ref, o_ref):
    s = _silu(c_ref[...]).astype(BF16)
    o_ref[...] = _dot(s, w_ref[...].astype(BF16)) + b_ref[...]


def _ada_call(c_all, ada_w, ada_b):
    depth = ada_w.shape[0]
    rows = c_all.shape[0]
    n_col = ada_w.shape[2] // D_MODEL
    return pl.pallas_call(
        _ada_kernel,
        out_shape=jax.ShapeDtypeStruct((depth, rows, ada_w.shape[2]), F32),
        grid=(depth, n_col),
        in_specs=[
            pl.BlockSpec((rows, D_MODEL), lambda l, j: (0, 0)),
            pl.BlockSpec((None, D_MODEL, D_MODEL), lambda l, j: (l, 0, j)),
            pl.BlockSpec((None, 1, D_MODEL), lambda l, j: (l, 0, j)),
        ],
        out_specs=pl.BlockSpec((None, rows, D_MODEL), lambda l, j: (l, 0, j)),
        compiler_params=pltpu.CompilerParams(vmem_limit_bytes=VMEM_LIMIT),
        name="ada_params",
    )(c_all, ada_w, ada_b.reshape(depth, 1, -1))


P_ALN_G, P_ALN_B, P_QN_G, P_KN_G, P_CONV_B, P_CLN_G, P_CLN_B, P_LB, P_DN_G = range(9)


def _gated_local_merge(h, wg_ref, out_a, out_c, out_d, wbr_ref):
    gate = lambda k: _sigmoid(_dot(h, wg_ref[:, k * D_MODEL:(k + 1) * D_MODEL]))
    macd = gate(0) * _dot(out_a.astype(BF16), wbr_ref[0])
    macd = macd + gate(2) * _dot(out_c.astype(BF16), wbr_ref[2])
    macd = macd + gate(3) * _dot(out_d.astype(BF16), wbr_ref[3])
    return macd, gate(1)


def _in_proj_views(win_ref):
    bounds, start = [], 0
    for width in (2 * BRANCH_W, 3 * BRANCH_W, 2 * BRANCH_W, 4 * BRANCH_W, HEADS * D_MODEL):
        bounds.append((start, start + width))
        start += width
    return [win_ref.at[:, lo:hi] for lo, hi in bounds]


def _front_kernel(n_aliased, *refs):
    (x_ref, g1_ref, sc_ref, sh_ref, win_ref, wbr_ref,
     p_ref, aws_ref, abias_ref, convw_ref, hmask_ref, tri_ref,
     q_ref, ktf_ref, vtf_ref, kt_ref, vb_ref, macd_ref, gb_ref, convnew_ref, state_ref,
     convbuf, st_ref) = refs[n_aliased:]
    tm = x_ref.shape[0]
    step = pl.program_id(0)
    wa_ref, wb_ref, wc_ref, wd_ref, wg_ref = _in_proj_views(win_ref)

    @pl.when(step == 0)
    def _():
        convbuf[...] = jnp.zeros_like(convbuf)
        st_ref[...] = jnp.zeros_like(st_ref)

    prm = p_ref[...]
    row = lambda i: prm[i:i + 1, :]
    hmask = hmask_ref[...]
    hmask_f32 = hmask.astype(F32)

    x = x_ref[...]
    h = (_rms_rows(x) * g1_ref[...] * (1.0 + sc_ref[...]) + sh_ref[...]).astype(BF16)

    gate_cols = {}

    def gate_slices(first, count):
        for s in range(first, first + count):
            gate_cols[s] = _sigmoid(_dot(h, wg_ref[:, s * BRANCH_W:(s + 1) * BRANCH_W]))

    za = _dot(h, wa_ref[...])
    zb = _dot(h, wb_ref[...])

    ga = jax.nn.gelu(za)
    u = ga[:, :BRANCH_W]
    vn = _layer_norm(ga[:, BRANCH_W:], row(P_ALN_G), row(P_ALN_B))
    zc = _dot(h, wc_ref[...])
    zd = _dot(h, wd_ref[...])
    lane_grp = lax.broadcasted_iota(jnp.int32, (CHUNK, BRANCH_W), 1) // (BRANCH_W // HEADS)
    tril = (lax.broadcasted_iota(jnp.int32, (CHUNK, CHUNK), 1)
            <= lax.broadcasted_iota(jnp.int32, (CHUNK, CHUNK), 0))
    w_causal = [jnp.where(tril, aws_ref[g], 0.0).astype(BF16) for g in range(HEADS)]
    mixed_chunks = []
    for c in range(tm // CHUNK):
        vc = vn[c * CHUNK:(c + 1) * CHUNK, :].astype(BF16)
        mixed = None
        for g in range(HEADS):
            mg = _dot(w_causal[g], vc)
            mixed = mg if mixed is None else jnp.where(lane_grp == g, mg, mixed)
        mixed_chunks.append(mixed + abias_ref[...])
    out_a = u * jnp.concatenate(mixed_chunks, axis=0)

    gate_slices(0, 2)

    qn = _head_rms(zb[:, :BRANCH_W], hmask, row(P_QN_G))
    kn = _head_rms(zb[:, BRANCH_W:2 * BRANCH_W], hmask, row(P_KN_G))
    vv = zb[:, 2 * BRANCH_W:]
    q_ref[...] = (qn * QK_SCALE).astype(BF16)
    knt_f32 = kn.T
    ktf_ref[...] = knt_f32
    vtf_ref[...] = vv.T
    vb_ref[...] = vv.astype(BF16)
    knt = knt_f32.astype(BF16)
    for c in range(tm // ATT_BLK):
        kt_ref[c] = knt[:, c * ATT_BLK:(c + 1) * ATT_BLK]

    gate_slices(2, 3)

    cin = zc[:, :BRANCH_W] * _sigmoid(zc[:, BRANCH_W:])
    for r in range(CONV_SHIFTS):
        convbuf[r, CONV_BASE - r:CONV_BASE - r + tm, :] = cin
    acc = jnp.broadcast_to(row(P_CONV_B), (tm, BRANCH_W))
    for j in range(CONV_W):
        lead = CONV_BASE - (CONV_W - 1) + j
        r = lead % CONV_SHIFTS
        acc = acc + convw_ref[j:j + 1, :] * convbuf[r, lead - r:lead - r + tm, :]
    out_c = _silu(_layer_norm(acc, row(P_CLN_G), row(P_CLN_B)))
    convnew_ref[...] = convbuf[0, tm + CONV_BASE - CONV_HALO:tm + CONV_BASE, :]
    for r in range(CONV_SHIFTS):
        convbuf[r, 0:CONV_BASE, :] = convbuf[r, tm:tm + CONV_BASE, :]

    gate_slices(5, 3)

    n_c = tm // SUB
    in_chunks = lambda a: a.reshape(n_c, SUB, BRANCH_W)
    qd = _silu(zd[:, :BRANCH_W]) * QK_SCALE
    logf = _log_forget(zd[:, BRANCH_W:2 * BRANCH_W], row(P_LB))
    kd = _one_minus_forget(zd[:, BRANCH_W:2 * BRANCH_W], row(P_LB))
    di = zd[:, 2 * BRANCH_W:3 * BRANCH_W]
    b = _dot_split_rhs(tri_ref[...], logf, 3)
    b3, qd3, kd3, di3 = in_chunks(b), in_chunks(qd), in_chunks(kd), in_chunks(di)
    bl3 = b3[:, SUB - 1:SUB, :]
    qt = (qd * jnp.exp(b)).astype(BF16)
    kt = (kd3 * jnp.exp(bl3 - b3)).reshape(tm, BRANCH_W).astype(BF16)
    el = jnp.exp(bl3)
    di_bf = di.astype(BF16)
    increments = [_dot_tn(di_bf[c * SUB:(c + 1) * SUB], kt[c * SUB:(c + 1) * SUB]) * hmask_f32
                  for c in range(n_c)]

    t_idx = lax.broadcasted_iota(jnp.int32, (n_c, SUB, BRANCH_W), 1)
    od = jnp.zeros((tm, BRANCH_W), F32)
    n_gate_cols = HEADS * D_MODEL // BRANCH_W
    for s in range(SUB):
        e = jnp.exp(jnp.where(t_idx >= s, b3 - b3[:, s:s + 1, :], MASKED_EXPONENT))
        wgt = e * qd3 * kd3[:, s:s + 1, :]
        coef = _dot(wgt.reshape(tm, BRANCH_W).astype(BF16), hmask)
        i_s = jnp.broadcast_to(di3[:, s:s + 1, :], (n_c, SUB, BRANCH_W)).reshape(tm, BRANCH_W)
        od = od + coef * i_s
        if s % 2 == 0 and 8 + s // 2 < n_gate_cols:
            gate_slices(8 + s // 2, 1)
    assert len(gate_cols) == n_gate_cols
    per_gate = D_MODEL // BRANCH_W
    gate = lambda k: jnp.concatenate([gate_cols[c] for c in range(k * per_gate, (k + 1) * per_gate)], axis=1)
    gate_a, gate_c, gate_d = gate(0), gate(2), gate(3)
    gb_ref[...] = gate(1)

    st = st_ref[...]
    o_inter = []
    for c in range(n_c):
        o_inter.append(_dot_nt(qt[c * SUB:(c + 1) * SUB], st.astype(BF16)))
        st = st * el[c] + increments[c]
    st_ref[...] = st
    state_ref[...] = st
    od = od + jnp.concatenate(o_inter, axis=0)
    out_d = _head_rms(od, hmask, row(P_DN_G)) * _silu(zd[:, 3 * BRANCH_W:])

    macd = gate_a * _dot(out_a.astype(BF16), wbr_ref[0])
    macd = macd + gate_c * _dot(out_c.astype(BF16), wbr_ref[2])
    macd_ref[...] = macd + gate_d * _dot(out_d.astype(BF16), wbr_ref[3])


def _front_call(x, g1, sc, sh, wts, consts, layer, depth, kv_prev):
    length = x.shape[0]
    tm = ROW_TILE
    n_steps = length // tm
    row_spec = lambda w: pl.BlockSpec((tm, w), lambda i: (i, 0))
    const_arrays, const_specs = _operands([g1, sc, sh, wts['win'], wts['wbr'],
                                           wts['ptab'], wts['a_ws'], wts['a_bias'], wts['conv_w'],
                                           consts['hmask'], consts['tri']])
    aliased = [] if kv_prev is None else list(kv_prev)
    in_arrays = aliased + [x] + const_arrays
    in_specs = [pl.BlockSpec(memory_space=pl.ANY)] * len(aliased) + [row_spec(D_MODEL)] + const_specs
    kv_spec = pl.BlockSpec((None, BRANCH_W, tm), lambda i: (layer, 0, i))
    out_shape = [
        jax.ShapeDtypeStruct((length, BRANCH_W), BF16),
        jax.ShapeDtypeStruct((depth, BRANCH_W, length), F32),
        jax.ShapeDtypeStruct((depth, BRANCH_W, length), F32),
        jax.ShapeDtypeStruct((length // ATT_BLK, BRANCH_W, ATT_BLK), BF16),
        jax.ShapeDtypeStruct((length, BRANCH_W), BF16),
        jax.ShapeDtypeStruct((length, D_MODEL), F32),
        jax.ShapeDtypeStruct((length, D_MODEL), F32),
        jax.ShapeDtypeStruct((CONV_HALO, BRANCH_W), F32),
        jax.ShapeDtypeStruct((BRANCH_W, BRANCH_W), F32),
    ]
    out_specs = [
        row_spec(BRANCH_W), kv_spec, kv_spec,
        pl.BlockSpec((tm // ATT_BLK, BRANCH_W, ATT_BLK), lambda i: (i, 0, 0)),
        row_spec(BRANCH_W), row_spec(D_MODEL), row_spec(D_MODEL),
        _const_spec((CONV_HALO, BRANCH_W)), _const_spec((BRANCH_W, BRANCH_W)),
    ]
    scratch = [
        pltpu.VMEM((CONV_SHIFTS, tm + CONV_BASE, BRANCH_W), F32),
        pltpu.VMEM((BRANCH_W, BRANCH_W), F32),
    ]
    return pl.pallas_call(
        functools.partial(_front_kernel, len(aliased)),
        out_shape=out_shape,
        grid=(n_steps,),
        in_specs=in_specs,
        out_specs=out_specs,
        scratch_shapes=scratch,
        input_output_aliases={n: 1 + n for n in range(len(aliased))},
        compiler_params=pltpu.CompilerParams(
            dimension_semantics=("arbitrary",), vmem_limit_bytes=VMEM_LIMIT),
        name="prompt_front",
    )(*in_arrays)


def _sb_scores(z, uo, causal):
    sp = jnp.maximum(z, 0.0) + jnp.log(1.0 + jnp.exp2(jnp.abs(z) * (-LOG2E)))
    if causal is not None:
        sp = jnp.where(causal, sp, 0.0)
    hi = sp.astype(BF16)
    lo = (sp - hi.astype(F32)).astype(BF16)
    cs = _dot(jnp.concatenate([hi, lo], axis=1), uo)
    return z - cs[:, :ATT_BLK], cs[:, ATT_BLK:]


def _sb_weights(d, r, causal):
    w = jnp.exp(d - r)
    if causal is not None:
        w = jnp.where(causal, w, 0.0)
    return w


def _attn_kernel(brows_ref, uo_ref, q_ref, kt_ref, v_ref, o_ref, acc_ref, r_ref, z_s, d_s, t_s):
    i = pl.program_id(0)
    n_sub = ATT_Q // ATT_BLK
    n_old = i * n_sub
    q = q_ref[...]
    unit_cols = jnp.where(lax.broadcasted_iota(jnp.int32, (ATT_Q, HEAD_DIM), 1) < 2, 1.0, 0.0).astype(BF16)
    q_ext = [jnp.concatenate([q[:, HEAD_DIM * h:HEAD_DIM * (h + 1)], unit_cols], axis=1)
             for h in range(HEADS)]
    acc_ref[...] = jnp.zeros_like(acc_ref)
    r_ref[...] = jnp.zeros_like(r_ref)
    lane_half = lax.broadcasted_iota(jnp.int32, (ATT_BLK, PAIR_W), 1) // HEAD_DIM
    half_lanes = [jnp.where(lane_half == k, 1.0, 0.0).astype(BF16) for k in range(HEAD_PAIR)]
    causal = (lax.broadcasted_iota(jnp.int32, (ATT_BLK, ATT_BLK), 1)
              < lax.broadcasted_iota(jnp.int32, (ATT_BLK, ATT_BLK), 0))

    brows_pair = [jnp.concatenate([brows_ref[h], brows_ref[h]], axis=1) for h in range(HEADS)]

    def logits(j, h, row0, n_rows):
        kt_h = jnp.concatenate([kt_ref[j, HEAD_DIM * h:HEAD_DIM * (h + 1), :], brows_ref[h]], axis=0)
        return _dot(q_ext[h][row0:row0 + n_rows, :], kt_h)

    def values(j, g):
        vb = v_ref[pl.ds(pl.multiple_of(j * ATT_BLK, ATT_BLK), ATT_BLK), PAIR_W * g:PAIR_W * (g + 1)]
        return jnp.concatenate([vb * half_lanes[k] for k in range(HEAD_PAIR)], axis=0)

    pieces = []
    for kb in reversed(range(n_sub)):
        pieces.append((kb, kb * ATT_BLK, ATT_BLK, causal))
        if kb + 1 < n_sub:
            pieces.append((kb, (kb + 1) * ATT_BLK, (n_sub - kb - 1) * ATT_BLK, None))
    scores = [[_sb_scores(logits(n_old + kb, h, row0, n_rows), uo_ref[...], mask) for h in range(HEADS)]
              for kb, row0, n_rows, mask in pieces]

    assert n_sub % 4 == 0
    def block_of(t):
        return jnp.maximum(n_old - 1 - t, 0)

    def older_blocks(g):
        heads = range(HEAD_PAIR * g, HEAD_PAIR * (g + 1))

        def stage1(p, z_slot):
            j0, j1 = block_of(2 * p), block_of(2 * p + 1)
            for k, h in enumerate(heads):
                rows_h = slice(HEAD_DIM * h, HEAD_DIM * (h + 1))
                kt_pair = jnp.concatenate([
                    jnp.concatenate([kt_ref[j0, rows_h, :], kt_ref[j1, rows_h, :]], axis=1),
                    brows_pair[h]], axis=0)
                z = _dot(q_ext[h], kt_pair)
                z_s[g, z_slot, 0, k] = z[:, :ATT_BLK]
                z_s[g, z_slot, 1, k] = z[:, ATT_BLK:]

        def stage2(z_slot, blk, d_slot):
            for k in range(HEAD_PAIR):
                d, tot = _sb_scores(z_s[g, z_slot, blk, k], uo_ref[...], None)
                d_s[g, d_slot, k] = d
                t_s[g, d_slot, k] = tot

        def stage3(t, d_slot):
            ws = []
            for k, h in enumerate(heads):
                r = r_ref[h]
                ws.append(_sb_weights(d_s[g, d_slot, k], r, None).astype(BF16))
                r_ref[h] = r + t_s[g, d_slot, k]
            acc_ref[g] += _dot(jnp.concatenate(ws, axis=1), values(block_of(t), g))

        def fill():
            stage1(0, 0)
            stage2(0, 0, 0)

        def four_blocks(v, carry):
            for half in range(2):
                p = 2 * v + half
                stage1(p + 1, 1 - half)
                stage2(half, 1, 1)
                stage3(2 * p, 0)
                stage2(1 - half, 0, 0)
                stage3(2 * p + 1, 1)
            return carry

        return fill, lambda: lax.fori_loop(0, n_old // 4, four_blocks, 0)

    pipelines = [older_blocks(g) for g in range(HEADS // HEAD_PAIR)]
    for fill, _ in pipelines:
        fill()

    for (kb, row0, n_rows, mask), piece_scores in zip(pieces, scores):
        for g in range(HEADS // HEAD_PAIR):
            ws = []
            for h in range(HEAD_PAIR * g, HEAD_PAIR * (g + 1)):
                d, tot = piece_scores[h]
                r = r_ref[h, row0:row0 + n_rows, :]
                ws.append(_sb_weights(d, r, mask).astype(BF16))
                r_ref[h, row0:row0 + n_rows, :] = r + tot
            acc_ref[g, row0:row0 + n_rows, :] += _dot(jnp.concatenate(ws, axis=1), values(n_old + kb, g))

    for g, (_, run) in enumerate(pipelines):
        run()
        o_ref[:, PAIR_W * g:PAIR_W * (g + 1)] = acc_ref[g]


def _attn_call(brows, uo, q, kt, vb):
    length = q.shape[0]
    return pl.pallas_call(
        _attn_kernel,
        out_shape=jax.ShapeDtypeStruct((length, BRANCH_W), F32),
        grid=(length // ATT_Q,),
        in_specs=[
            _const_spec(brows.shape), _const_spec(uo.shape),
            pl.BlockSpec((ATT_Q, BRANCH_W), lambda i: (i, 0)),
            _const_spec(kt.shape), _const_spec(vb.shape),
        ],
        out_specs=pl.BlockSpec((ATT_Q, BRANCH_W), lambda i: (i, 0)),
        scratch_shapes=[
            pltpu.VMEM((HEADS // HEAD_PAIR, ATT_Q, PAIR_W), F32),
            pltpu.VMEM((HEADS, ATT_Q, ATT_BLK), F32),
            pltpu.VMEM((HEADS // HEAD_PAIR, 2, 2, HEAD_PAIR, ATT_Q, ATT_BLK), F32),
            pltpu.VMEM((HEADS // HEAD_PAIR, 2, HEAD_PAIR, ATT_Q, ATT_BLK), F32),
            pltpu.VMEM((HEADS // HEAD_PAIR, 2, HEAD_PAIR, ATT_Q, ATT_BLK), F32),
        ],
        compiler_params=pltpu.CompilerParams(
            dimension_semantics=("arbitrary",), vmem_limit_bytes=VMEM_LIMIT),
        name="prompt_attn",
    )(brows, uo, q, kt, vb)


def _back_kernel(x_ref, ob_ref, macd_ref, gb_ref, g1_ref, n2_ref, sc2_ref, sh2_ref, g2_ref,
                 wb1_ref, wo_ref, w1_ref, w2_ref, y_ref):
    merged = macd_ref[...] + gb_ref[...] * _dot(ob_ref[...].astype(BF16), wb1_ref[...])
    x1 = x_ref[...] + g1_ref[...] * _dot(merged.astype(BF16), wo_ref[...])
    h2 = (_rms_rows(x1) * n2_ref[...] * (1.0 + sc2_ref[...]) + sh2_ref[...]).astype(BF16)
    hid = jnp.maximum(_dot(h2, w1_ref[...]), 0.0)
    y_ref[...] = x1 + g2_ref[...] * _dot((hid * hid).astype(BF16), w2_ref[...])


def _back_call(x, ob, macd, gb, g1, n2, sc2, sh2, g2, wts, tm):
    rows = x.shape[0]
    mod_rows = g1.shape[0]
    if mod_rows == 1:
        mod_spec = _const_spec((1, D_MODEL))
    else:
        mod_spec = pl.BlockSpec((tm, D_MODEL), lambda i: (i, 0))
    row_spec = lambda w: pl.BlockSpec((tm, w), lambda i: (i, 0))
    weights, weight_specs = _operands([wts['wb1'], wts['wo'], wts['w1'], wts['w2']])
    return pl.pallas_call(
        _back_kernel,
        out_shape=jax.ShapeDtypeStruct((rows, D_MODEL), F32),
        grid=(rows // tm,),
        in_specs=[row_spec(D_MODEL), row_spec(BRANCH_W), row_spec(D_MODEL), row_spec(D_MODEL),
                  mod_spec, _const_spec((1, D_MODEL)), mod_spec, mod_spec, mod_spec] + weight_specs,
        out_specs=row_spec(D_MODEL),
        compiler_params=pltpu.CompilerParams(
            dimension_semantics=("arbitrary",), vmem_limit_bytes=VMEM_LIMIT),
        name="layer_back",
    )(x, ob, macd, gb, g1, n2, sc2, sh2, g2, *weights)


def _sample_front_kernel(x_ref, g1_ref, sc_ref, sh_ref, win_ref, wbr_ref,
                         p_ref, a0_ref, convw_ref, hmask_ref, tdiag_ref, conv_ref, s0_ref,
                         q_ref, k_ref, v_ref, macd_ref, gb_ref, convnew_ref, snew_ref, av_ref):
    nb = x_ref.shape[0]
    wa_ref, wb_ref, wc_ref, wd_ref, wg_ref = _in_proj_views(win_ref)
    prm = p_ref[...]
    row = lambda i: prm[i:i + 1, :]
    hmask = hmask_ref[...]

    x = x_ref[...]
    h = (_rms_rows(x) * g1_ref[...] * (1.0 + sc_ref[...]) + sh_ref[...]).astype(BF16)

    ga = jax.nn.gelu(_dot(h, wa_ref[...]))
    vn = _layer_norm(ga[:, BRANCH_W:], row(P_ALN_G), row(P_ALN_B))
    av_ref[...] = vn
    out_a = ga[:, :BRANCH_W] * (a0_ref[0:1, :] * vn + a0_ref[1:2, :])

    zb = _dot(h, wb_ref[...])
    qn = _head_rms(zb[:, :BRANCH_W], hmask, row(P_QN_G))
    kn = _head_rms(zb[:, BRANCH_W:2 * BRANCH_W], hmask, row(P_KN_G))
    q_ref[...] = qn * QK_SCALE
    k_ref[...] = kn
    v_ref[...] = zb[:, 2 * BRANCH_W:]

    zc = _dot(h, wc_ref[...])
    cin = zc[:, :BRANCH_W] * _sigmoid(zc[:, BRANCH_W:])
    acc = row(P_CONV_B) + convw_ref[CONV_W - 1:CONV_W, :] * cin
    for j in range(CONV_W - 1):
        acc = acc + convw_ref[j:j + 1, :] * conv_ref[j]
        if j >= 1:
            convnew_ref[j - 1] = conv_ref[j]
    convnew_ref[CONV_W - 2] = cin
    out_c = _silu(_layer_norm(acc, row(P_CLN_G), row(P_CLN_B)))

    zd = _dot(h, wd_ref[...])
    qd = _silu(zd[:, :BRANCH_W]) * QK_SCALE
    logf = _log_forget(zd[:, BRANCH_W:2 * BRANCH_W], row(P_LB))
    ea = jnp.exp(logf)
    kd = _one_minus_forget(zd[:, BRANCH_W:2 * BRANCH_W], row(P_LB))
    di = zd[:, 2 * BRANCH_W:3 * BRANCH_W]
    tdiag = tdiag_ref[...]

    def spread(r):
        t3 = jnp.broadcast_to(r[:, None, :], (nb, HEAD_DIM, BRANCH_W)) * tdiag[None]
        flat = _dot_split_lhs(t3.reshape(nb * HEAD_DIM, BRANCH_W), hmask, 3)
        return flat.reshape(nb, HEAD_DIM, BRANCH_W)

    s0 = s0_ref[...]
    od = jnp.sum(spread(qd * ea) * s0, axis=1) + _dot_split_lhs(qd * kd, hmask, 3) * di
    snew_ref[...] = spread(ea) * s0 + spread(kd) * di[:, None, :]
    out_d = _head_rms(od, hmask, row(P_DN_G)) * _silu(zd[:, 3 * BRANCH_W:])

    macd, gb = _gated_local_merge(h, wg_ref, out_a, out_c, out_d, wbr_ref)
    macd_ref[...] = macd
    gb_ref[...] = gb


def _sample_front_call(x, g1, sc, sh, wts, consts, conv_t, s0_r):
    nb = x.shape[0]
    in_arrays, in_specs = _operands([x, g1, sc, sh, wts['win'], wts['wbr'],
                                     wts['ptab'], wts['a0'], wts['conv_w'], consts['hmask'], consts['tdiag'],
                                     conv_t, s0_r])
    out_shape = [
        jax.ShapeDtypeStruct((nb, BRANCH_W), F32),
        jax.ShapeDtypeStruct((nb, BRANCH_W), F32),
        jax.ShapeDtypeStruct((nb, BRANCH_W), F32),
        jax.ShapeDtypeStruct((nb, D_MODEL), F32),
        jax.ShapeDtypeStruct((nb, D_MODEL), F32),
        jax.ShapeDtypeStruct(conv_t.shape, F32),
        jax.ShapeDtypeStruct(s0_r.shape, F32),
        jax.ShapeDtypeStruct((nb, BRANCH_W), F32),
    ]
    return pl.pallas_call(
        _sample_front_kernel,
        out_shape=out_shape,
        grid=(1,),
        in_specs=in_specs,
        out_specs=[_const_spec(s.shape) for s in out_shape],
        compiler_params=pltpu.CompilerParams(vmem_limit_bytes=VMEM_LIMIT),
        name="sample_front",
    )(*in_arrays)


Q_ROWS = 16


def _paged_attn_kernel(pt_ref, q_ref, bias_ref, uo_ref, *refs):
    k_refs = refs[:PAGES_PER_STEP]
    v_refs = refs[PAGES_PER_STEP:2 * PAGES_PER_STEP]
    o_ref, acc_ref, r_ref = refs[2 * PAGES_PER_STEP:]
    jc = pl.program_id(1)

    @pl.when(jc == 0)
    def _():
        acc_ref[...] = jnp.zeros_like(acc_ref)
        r_ref[...] = jnp.zeros_like(r_ref)

    row_id = lax.broadcasted_iota(jnp.int32, (Q_ROWS, BRANCH_W), 0)
    own_head = row_id == lax.broadcasted_iota(jnp.int32, (Q_ROWS, BRANCH_W), 1) // HEAD_DIM
    qrows = jnp.where(own_head, jnp.broadcast_to(q_ref[...], (Q_ROWS, BRANCH_W)), 0.0).astype(BF16)
    as_matrix = lambda ref: ref[...].reshape(BRANCH_W, CHUNK).astype(BF16)
    pages = range(PAGES_PER_STEP)
    kt_all = jnp.concatenate([as_matrix(k_refs[p]) for p in pages], axis=1)
    z = _dot(qrows, kt_all)
    z_rows = jnp.concatenate(
        [z[:, p * CHUNK:(p + 1) * CHUNK] + bias_ref[...] for p in pages], axis=0)
    d, tot = _sb_scores(z_rows, uo_ref[...], None)
    r = r_ref[...]
    r_pages = [None] * PAGES_PER_STEP
    for p in reversed(pages):
        r_pages[p] = r
        r = r + tot[p * Q_ROWS:(p + 1) * Q_ROWS, :]
    r_ref[...] = r
    w = _sb_weights(d, jnp.concatenate(r_pages, axis=0), None).astype(BF16)
    w_all = jnp.concatenate([w[p * Q_ROWS:(p + 1) * Q_ROWS, :] for p in pages], axis=1)
    vt_all = jnp.concatenate([as_matrix(v_refs[p]) for p in pages], axis=1)
    acc = acc_ref[...] + _dot_nt(w_all, vt_all)
    acc_ref[...] = acc

    @pl.when(jc == pl.num_programs(1) - 1)
    def _():
        o_ref[...] = jnp.sum(jnp.where(own_head, acc, 0.0), axis=0, keepdims=True)


def _paged_attn_call(page_table, q, bias_rows, uo, cache_k, cache_v, layer):
    nb, n_pages = page_table.shape
    n_chunks = n_pages // PAGES_PER_STEP

    def page_spec(p):
        def index_map(b, jc, pt):
            return (pt[b, (n_chunks - 1 - jc) * PAGES_PER_STEP + p], layer, 0, 0, 0)
        return pl.BlockSpec((None, None, HEADS, HEAD_DIM, CHUNK), index_map)

    page_specs = [page_spec(p) for p in range(PAGES_PER_STEP)]
    grid_spec = pltpu.PrefetchScalarGridSpec(
        num_scalar_prefetch=1,
        grid=(nb, n_chunks),
        in_specs=[
            pl.BlockSpec((None, 1, BRANCH_W), lambda b, jc, pt: (b, 0, 0)),
            pl.BlockSpec(bias_rows.shape, lambda b, jc, pt: (0, 0)),
            pl.BlockSpec(uo.shape, lambda b, jc, pt: (0, 0)),
        ] + page_specs + page_specs,
        out_specs=pl.BlockSpec((None, 1, BRANCH_W), lambda b, jc, pt: (b, 0, 0)),
        scratch_shapes=[
            pltpu.VMEM((Q_ROWS, BRANCH_W), F32),
            pltpu.VMEM((Q_ROWS, CHUNK), F32),
        ],
    )
    return pl.pallas_call(
        _paged_attn_kernel,
        out_shape=jax.ShapeDtypeStruct((nb, 1, BRANCH_W), F32),
        grid_spec=grid_spec,
        compiler_params=pltpu.CompilerParams(
            dimension_semantics=("arbitrary", "arbitrary"), vmem_limit_bytes=VMEM_LIMIT),
        name="sample_paged_attn",
    )(page_table, q.reshape(nb, 1, BRANCH_W), bias_rows, uo,
      *([cache_k] * PAGES_PER_STEP), *([cache_v] * PAGES_PER_STEP))


def _constants():
    lane = jnp.arange(BRANCH_W)
    hmask = (lane[:, None] // HEAD_DIM == lane[None, :] // HEAD_DIM).astype(BF16)
    r = jnp.arange(ROW_TILE)
    tri = ((r[:, None] // SUB == r[None, :] // SUB) & (r[None, :] <= r[:, None])).astype(BF16)
    tdiag = (jnp.arange(HEAD_DIM)[:, None] == lane[None, :] % HEAD_DIM).astype(F32)
    j = jnp.arange(2 * ATT_BLK) % ATT_BLK
    c = jnp.arange(2 * ATT_BLK)
    uo = ((c[None, :] >= ATT_BLK) | (j[:, None] >= c[None, :])).astype(BF16)
    return {'hmask': hmask, 'tri': tri, 'tdiag': tdiag, 'uo': uo}


def _layer_weights(l, lb, big, a_ln_g, a_ln_b, a_ws, a_bs, b_qn_g, b_kn_g, c_conv_w, c_conv_b,
                   c_ln_g, c_ln_b, d_norm_g):
    tile4 = lambda g: jnp.tile(g, HEADS)
    rows = [a_ln_g[l], a_ln_b[l], tile4(b_qn_g[l]), tile4(b_kn_g[l]), c_conv_b[l], c_ln_g[l], c_ln_b[l],
            lb, tile4(d_norm_g[l])]
    ptab = jnp.zeros((16, BRANCH_W), F32).at[:len(rows)].set(jnp.stack(rows))
    grp = BRANCH_W // HEADS
    return {
        'win': _Stacked(big['w_in'], l),
        'wbr': _Stacked(big['w_branch'], l), 'wb1': _Stacked(big['w_branch'], l, 1), 'ptab': ptab,
        'a_ws': a_ws[l], 'a_bias': jnp.repeat(a_bs[l].T, grp, axis=1),
        'a0': jnp.stack([jnp.repeat(a_ws[l][:, 0, 0], grp), jnp.repeat(a_bs[l][:, 0], grp)]),
        'conv_w': jnp.zeros((32, BRANCH_W), F32).at[:CONV_W].set(c_conv_w[l]),
        'wo': _Stacked(big['w_out'], l), 'w1': _Stacked(big['w_ff1'], l), 'w2': _Stacked(big['w_ff2'], l),
    }


def kernel(x_prompt, x_sample, c_prompt, c_sample, cache_k, cache_v, state_conv, state_hgrn, page_table,
           ada_w, ada_b, norm1_g, norm2_g, w_in, a_ln_g, a_ln_b, a_ws, a_bs, b_qn_g, b_kn_g, b_bias,
           c_conv_w, c_conv_b, c_ln_g, c_ln_b, d_lb_logits, d_norm_g, w_branch, w_out, w_ff1, w_ff2):
    depth = w_in.shape[0]
    seq = x_prompt.shape[1]
    nb = x_sample.shape[0]
    n_pool = cache_k.shape[0]
    consts = _constants()

    cp = jnp.cumsum(jax.nn.softmax(d_lb_logits.astype(F32), axis=0), axis=0)
    lbs = cp - cp[0:1]

    ada_rows = 8 * ((1 + nb + 7) // 8)
    c_all = jnp.zeros((ada_rows, D_MODEL), F32).at[0:1].set(c_prompt).at[1:1 + nb].set(c_sample)
    mods = _ada_call(c_all, ada_w, ada_b)

    ck = cache_k.transpose(0, 1, 3, 4, 2)
    cv = cache_v.transpose(0, 1, 3, 4, 2)

    yp = x_prompt[0]
    ys = x_sample[:, 0]
    outs = {n: [] for n in ('ks', 'vs', 'cp', 'cs', 'sp', 'ss', 'av')}
    kv_all = None
    big = {'w_in': w_in.astype(BF16), 'w_branch': w_branch.astype(BF16), 'w_out': w_out.astype(BF16),
           'w_ff1': w_ff1.astype(BF16), 'w_ff2': w_ff2.astype(BF16)}
    for l in range(depth):
        wts = _layer_weights(l, lbs[l], big, a_ln_g, a_ln_b, a_ws, a_bs, b_qn_g, b_kn_g, c_conv_w,
                             c_conv_b, c_ln_g, c_ln_b, d_norm_g)
        mp = [mods[l, 0:1, i * D_MODEL:(i + 1) * D_MODEL] for i in range(6)]
        ms = [mods[l, 1:1 + nb, i * D_MODEL:(i + 1) * D_MODEL] for i in range(6)]
        n1 = norm1_g[l][None]
        n2 = norm2_g[l][None]
        bias_rows = jnp.zeros((Q_ROWS, ATT_BLK), F32).at[:HEADS].set(
            jnp.broadcast_to(b_bias[l][:, None], (HEADS, ATT_BLK)))
        b_hi = b_bias[l].astype(BF16)
        b_lo = (b_bias[l] - b_hi.astype(F32)).astype(BF16)
        brows = jnp.zeros((HEADS, HEAD_DIM, ATT_BLK), BF16)
        brows = brows.at[:, 0, :].set(b_hi[:, None]).at[:, 1, :].set(b_lo[:, None])

        q, k_all, v_all, kt, vb, macd, gb, conv_tail, st = _front_call(
            yp, n1, mp[1], mp[0], wts, consts, l, depth, kv_all)
        kv_all = (k_all, v_all)
        ob = _attn_call(brows, consts['uo'], q, kt, vb)
        yp = _back_call(yp, ob, macd, gb, mp[2], n2, mp[4], mp[3], mp[5], wts, BACK_TILE)
        outs['cp'].append(conv_tail[CONV_HALO - (CONV_W - 1):][None])
        st4 = st.reshape(HEADS, HEAD_DIM, HEADS, HEAD_DIM)
        outs['sp'].append(jnp.stack([st4[h, :, h, :].T for h in range(HEADS)])[None])

        conv_t = state_conv[:, l].transpose(1, 0, 2)
        s0_r = state_hgrn[:, l].transpose(0, 2, 1, 3).reshape(nb, HEAD_DIM, BRANCH_W)
        qs, ksn, vsn, macd_s, gb_s, conv_new, s_new, avn = _sample_front_call(
            ys, n1, ms[1], ms[0], wts, consts, conv_t, s0_r)
        obs = _paged_attn_call(page_table, qs, bias_rows, consts['uo'], ck, cv, l)
        ys = _back_call(ys, obs[:, 0], macd_s, gb_s, ms[2], n2, ms[4], ms[3], ms[5], wts, nb)
        outs['ks'].append(ksn.reshape(nb, 1, HEADS, HEAD_DIM))
        outs['vs'].append(vsn.reshape(nb, 1, HEADS, HEAD_DIM))
        outs['cs'].append(conv_new.transpose(1, 0, 2))
        outs['ss'].append(s_new.reshape(nb, HEAD_DIM, HEADS, HEAD_DIM).transpose(0, 2, 1, 3))
        outs['av'].append(avn[:, None, :])

    stack = lambda n: jnp.stack(outs[n], axis=1)
    rows_out = lambda a: a.reshape(1, depth, HEADS, HEAD_DIM, seq).transpose(0, 1, 4, 2, 3)
    return (yp[None], ys[:, None, :],
            rows_out(kv_all[0]), rows_out(kv_all[1]), stack('ks'), stack('vs'),
            stack('cp'), stack('cs'), stack('sp'), stack('ss'), stack('av'))
```
